```python
import math
import jax, jax.numpy as jnp
from jax import lax
import numpy as np

D_MODEL = 1024
BATCH = 16
SEQ = 2048
DEPTH = 1
DEC_BATCH = 32
DEC_SEQ = 4
PAST_LEN = 16384
PAGE_SIZE = 128

MIX_WIDTH = D_MODEL
ATTN_WIDTH = D_MODEL // 2
POOL_WIDTH = MIX_WIDTH - ATTN_WIDTH
HEAD_DIM = 64
N_HEADS = ATTN_WIDTH // HEAD_DIM
IN_WIDTH = 3 * ATTN_WIDTH + POOL_WIDTH
POOL_WINDOWS = (2, 4, 8, 16)
N_POOL_GROUPS = len(POOL_WINDOWS)
POOL_GROUP = POOL_WIDTH // N_POOL_GROUPS
POOL_STATE = max(POOL_WINDOWS) - 1
MOBA_BLOCK = 256
MOBA_TOPK = 3
N_BUCKETS = 32
MAX_DISTANCE = 128
D_FF = 4 * D_MODEL
N_MOD = 6
QCHUNK = 64
EPS = 1e-6
NEG = -1e30
SCALE = HEAD_DIM ** -0.5

kernel_name = 'hymba_moba_pool_adaln_decode_step'


def rms_norm(x, g):
    xf = x.astype(jnp.float32)
    y = xf * lax.rsqrt(jnp.mean(xf * xf, axis=-1, keepdims=True) + EPS)
    return (y * g.astype(jnp.float32)).astype(x.dtype)


def ada_modulation(c, w_ada, b_ada):
    m = jax.nn.silu(c) @ w_ada + b_ada
    return jnp.split(m[:, None, :], N_MOD, axis=-1)


def t5_bucket(rel):
    n = jnp.maximum(rel, 0)
    max_exact = N_BUCKETS // 2
    nf = jnp.maximum(n, max_exact).astype(jnp.float32)
    large = max_exact + (jnp.log(nf / max_exact) / math.log(MAX_DISTANCE / max_exact)
                         * (N_BUCKETS - max_exact)).astype(jnp.int32)
    large = jnp.minimum(large, N_BUCKETS - 1)
    return jnp.where(n < max_exact, n, large)


def moba_attend(qc, qpos, rel_bias, k_own, v_own, kpos_own, sel_part):
    bias_tab = rel_bias.T.astype(jnp.float32)
    rel = qpos[:, None] - kpos_own[None, :]
    s_own = (jnp.einsum('bhcd,bhkd->bhck', qc, k_own, preferred_element_type=jnp.float32) * SCALE
             + bias_tab[:, t5_bucket(rel)])
    s_own = jnp.where(rel >= 0, s_own, NEG)
    if sel_part is None:
        p = jax.nn.softmax(s_own, axis=-1).astype(v_own.dtype)
        return jnp.einsum('bhck,bhkd->bhcd', p, v_own)
    k_sel, v_sel, kpos_sel, sel_ok = sel_part
    B, H, C, n, K = kpos_sel.shape
    h_ix = jnp.arange(H)[None, :, None, None, None]
    rel_s = qpos[None, None, :, None, None] - kpos_sel
    s_sel = (jnp.einsum('bhcd,bhcnkd->bhcnk', qc, k_sel, preferred_element_type=jnp.float32) * SCALE
             + bias_tab[h_ix, t5_bucket(rel_s)])
    if sel_ok is not None:
        s_sel = jnp.where(sel_ok[..., None], s_sel, NEG)
    logits = jnp.concatenate([s_sel.reshape(B, H, C, n * K), s_own], axis=-1)
    p = jax.nn.softmax(logits, axis=-1).astype(v_own.dtype)
    p_sel = p[..., :n * K].reshape(B, H, C, n, K)
    return (jnp.einsum('bhcnk,bhcnkd->bhcd', p_sel, v_sel)
            + jnp.einsum('bhck,bhkd->bhcd', p[..., n * K:], v_own))


def moba_prompt(q, k, v, rel_bias):
    B, S = q.shape[:2]
    nb = -(-S // MOBA_BLOCK)
    pad = nb * MOBA_BLOCK - S

    def blocks(t):
        t = jnp.pad(t, ((0, 0), (0, pad), (0, 0), (0, 0)))
        return t.reshape(B, nb, MOBA_BLOCK, N_HEADS, HEAD_DIM).transpose(0, 3, 1, 2, 4)

    kb, vb = blocks(k), blocks(v)
    kmean = jnp.mean(kb.astype(jnp.float32), axis=3)
    n_sel = min(MOBA_TOPK, nb - 1)
    qh = q.transpose(0, 2, 1, 3)
    b_ix = jnp.arange(B)[:, None, None, None]
    h_ix = jnp.arange(N_HEADS)[None, :, None, None]
    blk_off = jnp.arange(MOBA_BLOCK)

    def chunk(ci):
        start = ci * QCHUNK
        qc = lax.dynamic_slice_in_dim(qh, start, QCHUNK, axis=2)
        qpos = start + jnp.arange(QCHUNK)
        own = start // MOBA_BLOCK
        k_own = lax.dynamic_index_in_dim(kb, own, axis=2, keepdims=False)
        v_own = lax.dynamic_index_in_dim(vb, own, axis=2, keepdims=False)
        kpos_own = own * MOBA_BLOCK + blk_off
        if n_sel == 0:
            return moba_attend(qc, qpos, rel_bias, k_own, v_own, kpos_own, None)
        gate = jnp.einsum('bhcd,bhnd->bhcn', qc.astype(jnp.float32), kmean)
        gate = jnp.where(jnp.arange(nb) < own, gate, NEG)
        _, sel = lax.top_k(gate, n_sel)
        k_sel = kb[b_ix, h_ix, sel]
        v_sel = vb[b_ix, h_ix, sel]
        kpos_sel = sel[..., None] * MOBA_BLOCK + blk_off
        return moba_attend(qc, qpos, rel_bias, k_own, v_own, kpos_own,
                           (k_sel, v_sel, kpos_sel, sel < own))

    out = lax.map(chunk, jnp.arange(S // QCHUNK))
    return out.transpose(1, 0, 3, 2, 4).reshape(B, S, ATTN_WIDTH)


def moba_sample(q, k, v, cache_k, cache_v, layer, page_table, rel_bias):
    DB, T = q.shape[:2]
    ppb = MOBA_BLOCK // PAGE_SIZE
    own = PAST_LEN // MOBA_BLOCK
    n_sel = min(MOBA_TOPK, own)
    r_pages = PAST_LEN // PAGE_SIZE - own * ppb
    qh = q.transpose(0, 2, 1, 3)
    qpos = PAST_LEN + jnp.arange(T)
    own_pages = page_table[:, own * ppb: own * ppb + r_pages]

    def own_rows(cache, new):
        past = cache[layer, own_pages].reshape(DB, r_pages * PAGE_SIZE, N_HEADS, HEAD_DIM)
        return jnp.concatenate([past.astype(new.dtype), new], axis=1).transpose(0, 2, 1, 3)

    k_own, v_own = own_rows(cache_k, k), own_rows(cache_v, v)
    kpos_own = own * MOBA_BLOCK + jnp.arange(r_pages * PAGE_SIZE + T)
    sel_part = None
    if n_sel > 0:
        past = cache_k[layer, page_table[:, :own * ppb]].astype(jnp.float32)
        kmean = jnp.mean(past.reshape(DB, own, MOBA_BLOCK, N_HEADS, HEAD_DIM), axis=2).transpose(0, 2, 1, 3)
        gate = jnp.einsum('bhtd,bhnd->bhtn', qh.astype(jnp.float32), kmean)
        _, sel = lax.top_k(gate, n_sel)
        b_ix = jnp.arange(DB)[:, None, None, None, None]
        phys = page_table[b_ix, sel[..., None] * ppb + jnp.arange(ppb)]
        h_ix = jnp.arange(N_HEADS)[None, :, None, None, None, None]

        def gather(cache):
            g = cache[layer, phys[..., None], jnp.arange(PAGE_SIZE), h_ix]
            return g.reshape(DB, N_HEADS, T, n_sel, MOBA_BLOCK, HEAD_DIM).astype(q.dtype)

        kpos_sel = sel[..., None] * MOBA_BLOCK + jnp.arange(MOBA_BLOCK)
        sel_part = (gather(cache_k), gather(cache_v), kpos_sel, None)
    out = moba_attend(qh, qpos, rel_bias, k_own, v_own, kpos_own, sel_part)
    return out.transpose(0, 2, 1, 3).reshape(DB, T, ATTN_WIDTH)


def multi_pool(u_full, first_pos, w_pool, pool_scale):
    B, L, _ = u_full.shape
    uf = u_full.astype(jnp.float32)
    cs = jnp.pad(jnp.cumsum(uf, axis=1), ((0, 0), (1, 0), (0, 0)))
    pos = first_pos + jnp.arange(L)
    diffs = []
    for g, w in enumerate(POOL_WINDOWS):
        sl = slice(g * POOL_GROUP, (g + 1) * POOL_GROUP)
        cg = cs[..., sl]
        lower = jnp.pad(cg, ((0, 0), (w - 1, 0), (0, 0)))[:, :L]
        cnt = jnp.minimum(pos + 1, w).astype(jnp.float32)[None, :, None]
        diffs.append((cg[:, 1:] - lower) / cnt - uf[..., sl])
    d = jnp.stack(diffs, axis=2).astype(u_full.dtype)
    y = jnp.einsum('blgc,gcd->blgd', d, w_pool).reshape(B, L, POOL_WIDTH)
    return y * pool_scale


def layer_in(x, c, w_ada, b_ada, norm_mix, w_in):
    mods = ada_modulation(c, w_ada, b_ada)
    h = rms_norm(x, norm_mix) * (1 + mods[1]) + mods[0]
    B, L, _ = x.shape
    q, k, v, u = jnp.split(h @ w_in, [ATTN_WIDTH, 2 * ATTN_WIDTH, 3 * ATTN_WIDTH], axis=-1)
    heads = lambda t: t.reshape(B, L, N_HEADS, HEAD_DIM)
    return mods, heads(q), heads(k), heads(v), u


def layer_out(x, attn, pool, mods, w_out, norm_mlp, w_up, w_down):
    _, _, g1, sh2, sc2, g2 = mods
    x = x + g1 * (jnp.concatenate([attn, pool], axis=-1) @ w_out)
    h = rms_norm(x, norm_mlp) * (1 + sc2) + sh2
    return x + g2 * (jnp.square(jax.nn.relu(h @ w_up)) @ w_down)


def setup_inputs(seed: int = 0) -> dict:
    key = jax.random.key(seed)
    ks = jax.random.split(key, 24)
    n_pages = PAST_LEN // PAGE_SIZE
    n_phys = (DEC_BATCH * n_pages * 5) // 4
    f32 = jnp.float32

    def nrm(k, shape, s):
        return jax.random.normal(k, shape, f32) * s

    page_table = jax.random.permutation(ks[5], n_phys)[:DEC_BATCH * n_pages]
    page_table = page_table.reshape(DEC_BATCH, n_pages).astype(jnp.int32)
    return {
        'x_prompt': nrm(ks[0], (BATCH, SEQ, D_MODEL), 1.0),
        'x_sample': nrm(ks[1], (DEC_BATCH, DEC_SEQ, D_MODEL), 1.0),
        'cache_k': nrm(ks[2], (DEPTH, n_phys, PAGE_SIZE, N_HEADS, HEAD_DIM), 1.0),
        'cache_v': nrm(ks[3], (DEPTH, n_phys, PAGE_SIZE, N_HEADS, HEAD_DIM), 1.0),
        'state_pool': nrm(ks[4], (DEPTH, DEC_BATCH, POOL_STATE, POOL_WIDTH), 1.0),
        'page_table': page_table,
        'c_prompt': nrm(ks[6], (BATCH, D_MODEL), 1.0),
        'c_sample': nrm(ks[7], (DEC_BATCH, D_MODEL), 1.0),
        'w_ada': nrm(ks[8], (DEPTH, D_MODEL, N_MOD * D_MODEL), 0.02),
        'b_ada': nrm(ks[9], (DEPTH, N_MOD * D_MODEL), 0.01),
        'norm_mix': 1.0 + nrm(ks[10], (DEPTH, D_MODEL), 0.1),
        'w_in': nrm(ks[11], (DEPTH, D_MODEL, IN_WIDTH), D_MODEL ** -0.5),
        'rel_bias': nrm(ks[12], (N_BUCKETS, N_HEADS), 0.5),
        'w_pool': nrm(ks[13], (DEPTH, N_POOL_GROUPS, POOL_GROUP, POOL_GROUP), POOL_GROUP ** -0.5),
        'pool_scale': 1.0 + nrm(ks[14], (DEPTH, POOL_WIDTH), 0.1),
        'w_out': nrm(ks[15], (DEPTH, MIX_WIDTH, D_MODEL), MIX_WIDTH ** -0.5),
        'norm_mlp': 1.0 + nrm(ks[16], (DEPTH, D_MODEL), 0.1),
        'w_up': nrm(ks[17], (DEPTH, D_MODEL, D_FF), D_MODEL ** -0.5),
        'w_down': nrm(ks[18], (DEPTH, D_FF, D_MODEL), D_FF ** -0.5),
        'norm_final': 1.0 + nrm(ks[19], (D_MODEL,), 0.1),
    }


def reference(x_prompt, x_sample, cache_k, cache_v, state_pool, page_table, c_prompt, c_sample,
              w_ada, b_ada, norm_mix, w_in, rel_bias, w_pool, pool_scale, w_out, norm_mlp,
              w_up, w_down, norm_final):
    xp, xs = x_prompt, x_sample
    kp, vp, pp, ksm, vsm, psm = [], [], [], [], [], []
    for l in range(DEPTH):
        mods, q, k, v, u = layer_in(xp, c_prompt, w_ada[l], b_ada[l], norm_mix[l], w_in[l])
        attn = moba_prompt(q, k, v, rel_bias)
        pool = multi_pool(u, 0, w_pool[l], pool_scale[l])
        xp = layer_out(xp, attn, pool, mods, w_out[l], norm_mlp[l], w_up[l], w_down[l])
        kp.append(k)
        vp.append(v)
        pp.append(u[:, -POOL_STATE:])
        mods, q, k, v, u = layer_in(xs, c_sample, w_ada[l], b_ada[l], norm_mix[l], w_in[l])
        attn = moba_sample(q, k, v, cache_k, cache_v, l, page_table, rel_bias)
        u_full = jnp.concatenate([state_pool[l].astype(u.dtype), u], axis=1)
        pool = multi_pool(u_full, PAST_LEN - POOL_STATE, w_pool[l], pool_scale[l])[:, POOL_STATE:]
        xs = layer_out(xs, attn, pool, mods, w_out[l], norm_mlp[l], w_up[l], w_down[l])
        ksm.append(k)
        vsm.append(v)
        psm.append(u_full[:, -POOL_STATE:])
    y_prompt = rms_norm(xp, norm_final)
    y_sample = rms_norm(xs, norm_final)
    return (y_prompt, y_sample, jnp.stack(kp), jnp.stack(vp), jnp.stack(pp),
            jnp.stack(ksm), jnp.stack(vsm), jnp.stack(psm))
```

```python
import functools
import math

import numpy as np
import jax
import jax.numpy as jnp
from jax import lax
from jax.experimental import pallas as pl
from jax.experimental.pallas import tpu as pltpu

HEAD_DIM = 64
N_HEADS = 8
ATTN_WIDTH = N_HEADS * HEAD_DIM
POOL_WINDOWS = (2, 4, 8, 16)
POOL_GROUP = 128
POOL_WIDTH = POOL_GROUP * len(POOL_WINDOWS)
POOL_STATE = max(POOL_WINDOWS) - 1
POOL_HALO = POOL_STATE + 1
MOBA_BLOCK = 256
MOBA_TOPK = 3
N_BUCKETS = 32
MAX_DISTANCE = 128
PAGE_SIZE = 128
PAGES_PER_BLOCK = MOBA_BLOCK // PAGE_SIZE
N_MOD = 6
EPS = 1e-6
NEG = -1e30
SCALE = HEAD_DIM ** -0.5
LANES = 128
FF_CHUNK = 1024
VMEM_LIMIT = 56 * 1024 * 1024

BF16 = jnp.bfloat16
F32 = jnp.float32


def _nt_dot(a, b):
    return lax.dot_general(a, b, (((1,), (1,)), ((), ())), preferred_element_type=F32)


def _dot(a, b):
    return jnp.dot(a, b, preferred_element_type=F32)


def _row_tile(n, candidates=(512, 256, 128, 64, 32, 16, 8)):
    for c in candidates:
        if n % c == 0:
            return c
    raise ValueError(f"row count {n} is not a multiple of 8")


def _t5_bucket_np(rel):
    n = np.maximum(rel, 0)
    max_exact = N_BUCKETS // 2
    nf = np.maximum(n, max_exact).astype(np.float32)
    large = max_exact + (np.log(nf / np.float32(max_exact)) / np.float32(math.log(MAX_DISTANCE / max_exact))
                         * np.float32(N_BUCKETS - max_exact)).astype(np.int32)
    large = np.minimum(large, N_BUCKETS - 1)
    return np.where(n < max_exact, n, large).astype(np.int32)


def _rms(x, g):
    return x * lax.rsqrt(jnp.mean(x * x, axis=-1, keepdims=True) + EPS) * g


def _ada_kernel(c_ref, w_ref, b_ref, o_ref):
    c = c_ref[...]
    s = c / (1.0 + jnp.exp(-c))
    o_ref[...] = _dot(s.astype(BF16), w_ref[0].astype(BF16)) + b_ref[...]


def _ada(c_all, w_ada, b_ada):
    n, d = c_all.shape
    width = w_ada.shape[-1]
    tn = 1024
    return pl.pallas_call(
        _ada_kernel,
        grid=(width // tn,),
        in_specs=[pl.BlockSpec((n, d), lambda j: (0, 0)),
                  pl.BlockSpec((1, d, tn), lambda j: (0, 0, j)),
                  pl.BlockSpec((1, tn), lambda j: (0, j))],
        out_specs=pl.BlockSpec((n, tn), lambda j: (0, j)),
        out_shape=jax.ShapeDtypeStruct((n, width), F32),
        compiler_params=pltpu.CompilerParams(dimension_semantics=("arbitrary",), vmem_limit_bytes=VMEM_LIMIT),
        name="ada_mod",
    )(c_all, w_ada, b_ada)


def _inproj_kernel(x_ref, mods_ref, g_ref, w_ref, q_ref, k_ref, v_ref, u_ref):
    d = x_ref.shape[-1]
    x = x_ref[0]
    shift = mods_ref[0, :, 0:d]
    scale = mods_ref[0, :, d:2 * d]
    h = _rms(x, g_ref[...]) * (1.0 + scale) + shift
    r = _dot(h.astype(BF16), w_ref[...])
    a = ATTN_WIDTH
    q_ref[0] = (r[:, 0:a] * SCALE).astype(BF16)
    k_ref[0] = r[:, a:2 * a]
    v_ref[0] = r[:, 2 * a:3 * a]
    u_ref[0] = r[:, 3 * a:]


def _inproj(x, mods, g, w_in_b):
    nb, s, d = x.shape
    r = mods.shape[1]
    ts = _row_tile(s)
    width = w_in_b.shape[1]
    row_spec = lambda w: pl.BlockSpec((1, ts, w), lambda b, i: (b, i, 0))
    mods_spec = (pl.BlockSpec((1, 1, N_MOD * d), lambda b, i: (b, 0, 0)) if r == 1
                 else pl.BlockSpec((1, ts, N_MOD * d), lambda b, i: (b, i, 0)))
    return pl.pallas_call(
        _inproj_kernel,
        grid=(nb, s // ts),
        in_specs=[row_spec(d), mods_spec,
                  pl.BlockSpec((1, d), lambda b, i: (0, 0)),
                  pl.BlockSpec((d, width), lambda b, i: (0, 0))],
        out_specs=[row_spec(ATTN_WIDTH), row_spec(ATTN_WIDTH), row_spec(ATTN_WIDTH), row_spec(POOL_WIDTH)],
        out_shape=[jax.ShapeDtypeStruct((nb, s, ATTN_WIDTH), BF16),
                   jax.ShapeDtypeStruct((nb, s, ATTN_WIDTH), F32),
                   jax.ShapeDtypeStruct((nb, s, ATTN_WIDTH), F32),
                   jax.ShapeDtypeStruct((nb, s, POOL_WIDTH), F32)],
        compiler_params=pltpu.CompilerParams(dimension_semantics=("arbitrary", "arbitrary"),
                                             vmem_limit_bytes=VMEM_LIMIT),
        name="in_proj",
    )(x, mods, g, w_in_b)


def _bias_kernel(tab_ref, idx_own_ref, idx_adj_ref, own_ref, adj_ref):
    h = pl.program_id(0)
    io = idx_own_ref[...]
    ia = idx_adj_ref[...]
    bo = jnp.zeros(io.shape, F32)
    ba = jnp.zeros(ia.shape, F32)
    for b in range(N_BUCKETS):
        t = tab_ref[h, b]
        bo = jnp.where(io == b, t, bo)
        ba = jnp.where(ia == b, t, ba)
    own_ref[0] = jnp.where(io < 0, NEG, bo)
    adj_ref[0] = ba


def _bias_tiles(tab_t):
    blk = MOBA_BLOCK
    r = np.arange(blk)[:, None]
    c = np.arange(blk)[None, :]
    idx_own = np.where(r >= c, _t5_bucket_np(r - c), -1).astype(np.int32)
    idx_adj = _t5_bucket_np(blk + r - c)
    tile = pl.BlockSpec((blk, blk), lambda h: (0, 0))
    out = pl.BlockSpec((1, blk, blk), lambda h: (h, 0, 0))
    return pl.pallas_call(
        _bias_kernel,
        grid=(N_HEADS,),
        in_specs=[pl.BlockSpec(memory_space=pltpu.SMEM), tile, tile],
        out_specs=[out, out],
        out_shape=[jax.ShapeDtypeStruct((N_HEADS, blk, blk), F32)] * 2,
        compiler_params=pltpu.CompilerParams(dimension_semantics=("arbitrary",)),
        name="rel_bias_tiles",
    )(tab_t, jnp.asarray(idx_own), jnp.asarray(idx_adj))


def _softmax_update(carry, s, v_blk):
    m, l, acc = carry
    m_new = jnp.maximum(m, jnp.max(s, axis=-1, keepdims=True))
    alpha = jnp.exp(m - m_new)
    p = jnp.exp(s - m_new)
    l = alpha * l + jnp.sum(p, axis=-1, keepdims=True)
    acc = alpha * acc + _dot(p.astype(BF16), v_blk)
    return m_new, l, acc


def _attn_prompt_kernel(tab_ref, q_ref, k_ref, v_ref, bown_ref, badj_ref, o_ref,
                        kb_ref, vb_ref, km_ref, madd_ref, *, nb):
    i = pl.program_id(1)
    blk = MOBA_BLOCK

    @pl.when(i == 0)
    def _():
        kb_ref[...] = k_ref[0].astype(BF16)
        vb_ref[...] = v_ref[0].astype(BF16)
        km_ref[...] = jnp.zeros(km_ref.shape, F32)
        means = [jnp.mean(k_ref[0, j * blk:(j + 1) * blk, :], axis=0, keepdims=True) for j in range(nb)]
        km_ref[0:nb, :] = jnp.concatenate(means, axis=0)

    lane = lax.broadcasted_iota(jnp.int32, (blk, LANES), 1)
    valid = lane < i
    jm1 = jnp.maximum(i - 1, 0)
    own0 = pl.multiple_of(i * blk, blk)
    adj0 = pl.multiple_of(jm1 * blk, blk)
    for p in range(ATTN_WIDTH // LANES):
        ps = slice(p * LANES, (p + 1) * LANES)
        q2 = q_ref[0, :, ps]
        kmp = km_ref[:, ps].astype(BF16)
        outs = []
        for hh in range(LANES // HEAD_DIM):
            h = p * (LANES // HEAD_DIM) + hh
            hmask = (lane >= hh * HEAD_DIM) & (lane < (hh + 1) * HEAD_DIM)
            qm = jnp.where(hmask, q2, jnp.zeros_like(q2))
            gate = _nt_dot(qm, kmp)
            for j in range(nb):
                gj = gate[:, j:j + 1]
                beats = ((gate > gj) | ((gate == gj) & (lane < j))) & valid
                cnt = jnp.sum(beats.astype(F32), axis=-1, keepdims=True)
                cnt = cnt + jnp.where(j < i, 0.0, float(nb))
                madd_ref[j] = jnp.broadcast_to(jnp.where(cnt < MOBA_TOPK, 0.0, NEG), (blk, LANES))
            c_far = tab_ref[h, N_BUCKETS - 1]

            s = _nt_dot(qm, kb_ref[pl.ds(own0, blk), ps]) + bown_ref[h]
            m = jnp.max(s, axis=-1, keepdims=True)
            pexp = jnp.exp(s - m)
            l = jnp.sum(pexp, axis=-1, keepdims=True)
            acc = _dot(pexp.astype(BF16), vb_ref[pl.ds(own0, blk), ps])

            md = madd_ref[jm1]
            s = _nt_dot(qm, kb_ref[pl.ds(adj0, blk), ps]) + badj_ref[h] + jnp.concatenate([md, md], axis=1)
            carry = _softmax_update((m, l, acc), s, vb_ref[pl.ds(adj0, blk), ps])

            def far(j, carry):
                j0 = pl.multiple_of(j * blk, blk)
                mdj = madd_ref[j] + c_far
                sj = _nt_dot(qm, kb_ref[pl.ds(j0, blk), ps]) + jnp.concatenate([mdj, mdj], axis=1)
                return _softmax_update(carry, sj, vb_ref[pl.ds(j0, blk), ps])

            m, l, acc = lax.fori_loop(0, jm1, far, carry)
            outs.append(acc / l)
        o_ref[0, :, ps] = jnp.where(lane < HEAD_DIM, outs[0], outs[1]).astype(BF16)


def _attn_prompt(tab_t, q, k, v, bias_own, bias_adj):
    b, s, a = q.shape
    blk = MOBA_BLOCK
    nb = s // blk
    assert s % blk == 0 and nb <= LANES
    full = pl.BlockSpec((1, s, a), lambda bi, i: (bi, 0, 0))
    tile = pl.BlockSpec((1, blk, a), lambda bi, i: (bi, i, 0))
    bias = pl.BlockSpec((N_HEADS, blk, blk), lambda bi, i: (0, 0, 0))
    return pl.pallas_call(
        functools.partial(_attn_prompt_kernel, nb=nb),
        grid=(b, nb),
        in_specs=[pl.BlockSpec(memory_space=pltpu.SMEM), tile, full, full, bias, bias],
        out_specs=tile,
        out_shape=jax.ShapeDtypeStruct((b, s, a), BF16),
        scratch_shapes=[pltpu.VMEM((s, a), BF16), pltpu.VMEM((s, a), BF16),
                        pltpu.VMEM((LANES, a), F32), pltpu.VMEM((nb, blk, LANES), F32)],
        compiler_params=pltpu.CompilerParams(dimension_semantics=("arbitrary", "arbitrary"),
                                             vmem_limit_bytes=VMEM_LIMIT),
        name="moba_prompt",
    )(tab_t, q, k, v, bias_own, bias_adj)


def _pool_kernel(u_ref, hist_ref, w_ref, scale_ref, o_ref, *, first_pos):
    i = pl.program_id(1)
    tp = u_ref.shape[1]
    u = u_ref[0]
    hist = jnp.where(i > 0, hist_ref[0], 0.0)
    ext = jnp.concatenate([hist, u], axis=0)
    pos = first_pos + i * tp + lax.broadcasted_iota(jnp.int32, (tp, 1), 0)
    for g, w in enumerate(POOL_WINDOWS):
        gs = slice(g * POOL_GROUP, (g + 1) * POOL_GROUP)
        s = ext[:, gs]
        shift = 1
        while shift < w:
            s = s + pltpu.roll(s, shift, 0)
            shift *= 2
        cnt = jnp.minimum(pos + 1, w).astype(F32)
        d = s[POOL_HALO:, :] / cnt - u[:, gs]
        y = _dot(d.astype(BF16), w_ref[g]) * scale_ref[:, gs]
        o_ref[0, :, gs] = y.astype(BF16)


def _pool(u, w_pool_b, pool_scale, first_pos):
    nb, s, c = u.shape
    tp = _row_tile(s, (512, 256, 128, 64, 32, 16))
    hb = tp // POOL_HALO
    return pl.pallas_call(
        functools.partial(_pool_kernel, first_pos=first_pos),
        grid=(nb, s // tp),
        in_specs=[pl.BlockSpec((1, tp, c), lambda b, i: (b, i, 0)),
                  pl.BlockSpec((1, POOL_HALO, c), lambda b, i: (b, jnp.maximum(i * hb - 1, 0), 0)),
                  pl.BlockSpec(w_pool_b.shape, lambda b, i: (0, 0, 0)),
                  pl.BlockSpec((1, c), lambda b, i: (0, 0))],
        out_specs=pl.BlockSpec((1, tp, c), lambda b, i: (b, i, 0)),
        out_shape=jax.ShapeDtypeStruct((nb, s, c), BF16),
        compiler_params=pltpu.CompilerParams(dimension_semantics=("arbitrary", "arbitrary"),
                                             vmem_limit_bytes=VMEM_LIMIT),
        name="multi_pool",
    )(u, u, w_pool_b, pool_scale)


def _out_kernel(x_ref, attn_ref, pool_ref, mods_ref, gm_ref, gf_ref, wo_ref, wu_ref, wd_ref, y_ref):
    d = x_ref.shape[-1]
    a = ATTN_WIDTH
    x = x_ref[0]
    g1 = mods_ref[0, :, 2 * d:3 * d]
    sh2 = mods_ref[0, :, 3 * d:4 * d]
    sc2 = mods_ref[0, :, 4 * d:5 * d]
    g2 = mods_ref[0, :, 5 * d:6 * d]
    mix = _dot(attn_ref[0], wo_ref[0:a, :]) + _dot(pool_ref[0], wo_ref[a:, :])
    x1 = x + g1 * mix
    hb = (_rms(x1, gm_ref[...]) * (1.0 + sc2) + sh2).astype(BF16)
    acc = jnp.zeros(x.shape, F32)
    for c in range(wu_ref.shape[1] // FF_CHUNK):
        cs = slice(c * FF_CHUNK, (c + 1) * FF_CHUNK)
        t = jnp.maximum(_dot(hb, wu_ref[:, cs]), 0.0)
        acc = acc + _dot((t * t).astype(BF16), wd_ref[cs, :])
    x2 = x1 + g2 * acc
    y_ref[0] = _rms(x2, gf_ref[...])


def _out(x, attn, pool, mods, g_mlp, g_final, w_out_b, w_up_b, w_down_b):
    nb, s, d = x.shape
    r = mods.shape[1]
    tm = _row_tile(s)
    row_spec = lambda w: pl.BlockSpec((1, tm, w), lambda b, i: (b, i, 0))
    mods_spec = (pl.BlockSpec((1, 1, N_MOD * d), lambda b, i: (b, 0, 0)) if r == 1
                 else pl.BlockSpec((1, tm, N_MOD * d), lambda b, i: (b, i, 0)))
    resident = lambda w: pl.BlockSpec(w.shape, lambda b, i: (0, 0), pipeline_mode=pl.Buffered(1))
    vec = pl.BlockSpec((1, d), lambda b, i: (0, 0))
    return pl.pallas_call(
        _out_kernel,
        grid=(nb, s // tm),
        in_specs=[row_spec(d), row_spec(ATTN_WIDTH), row_spec(POOL_WIDTH), mods_spec, vec, vec,
                  resident(w_out_b), resident(w_up_b), resident(w_down_b)],
        out_specs=row_spec(d),
        out_shape=jax.ShapeDtypeStruct((nb, s, d), F32),
        compiler_params=pltpu.CompilerParams(dimension_semantics=("arbitrary", "arbitrary"),
                                             vmem_limit_bytes=VMEM_LIMIT),
        name="out_mlp",
    )(x, attn, pool, mods, g_mlp, g_final, w_out_b, w_up_b, w_down_b)


KMEAN_PAGES_PER_STEP = 16


def _kmean_kernel(pt_ref, *refs):
    page_refs, o_ref = refs[:-1], refs[-1]
    for blk in range(len(page_refs) // PAGES_PER_BLOCK):
        s = page_refs[blk * PAGES_PER_BLOCK][0, 0]
        for pg in range(1, PAGES_PER_BLOCK):
            s = s + page_refs[blk * PAGES_PER_BLOCK + pg][0, 0]
        o_ref[0, blk] = jnp.sum(s, axis=-1) * (1.0 / MOBA_BLOCK)


def _kmean(page_table, cache_kt):
    db, n_pages = page_table.shape
    nblk = n_pages // PAGES_PER_BLOCK
    pps = math.gcd(KMEAN_PAGES_PER_STEP, n_pages)
    assert pps % PAGES_PER_BLOCK == 0
    page_shape = (1, 1) + cache_kt.shape[2:]

    def page_spec(n):
        return pl.BlockSpec(page_shape, lambda b, s, pt: (0, pt[b, s * pps + n], 0, 0, 0))

    grid_spec = pltpu.PrefetchScalarGridSpec(
        num_scalar_prefetch=1,
        grid=(db, n_pages // pps),
        in_specs=[page_spec(n) for n in range(pps)],
        out_specs=pl.BlockSpec((1, pps // PAGES_PER_BLOCK, N_HEADS, HEAD_DIM), lambda b, s, pt: (b, s, 0, 0)),
    )
    return pl.pallas_call(
        _kmean_kernel,
        grid_spec=grid_spec,
        out_shape=jax.ShapeDtypeStruct((db, nblk, N_HEADS, HEAD_DIM), F32),
        compiler_params=pltpu.CompilerParams(dimension_semantics=("arbitrary", "arbitrary"),
                                             vmem_limit_bytes=VMEM_LIMIT),
        name="paged_kmean",
    )(page_table, *([cache_kt] * pps))


def _select_kernel(q_ref, km_ref, sel_ref):
    g = lax.dot_general(q_ref[0].astype(F32), km_ref[0], (((1,), (1,)), ((), ())),
                        precision=lax.Precision.HIGHEST, preferred_element_type=F32)
    rows, nblk = g.shape
    lane = lax.broadcasted_iota(jnp.int32, g.shape, 1)
    out_lane = lax.broadcasted_iota(jnp.int32, (rows, LANES), 1)
    out = jnp.zeros((rows, LANES), jnp.int32)
    for n in range(MOBA_TOPK):
        mx = jnp.max(g, axis=-1, keepdims=True)
        idx = jnp.min(jnp.where(g == mx, lane, nblk), axis=-1, keepdims=True)
        out = jnp.where(out_lane == n, idx, out)
        g = jnp.where(lane == idx, -jnp.inf, g)
    sel_ref[0] = out


def _select(q_bd, kmean2d):
    db, rows, a = q_bd.shape
    nblk = kmean2d.shape[1]
    return pl.pallas_call(
        _select_kernel,
        grid=(db,),
        in_specs=[pl.BlockSpec((1, rows, a), lambda b: (b, 0, 0)),
                  pl.BlockSpec((1, nblk, a), lambda b: (b, 0, 0))],
        out_specs=pl.BlockSpec((1, rows, LANES), lambda b: (b, 0, 0)),
        out_shape=jax.ShapeDtypeStruct((db, rows, LANES), jnp.int32),
        compiler_params=pltpu.CompilerParams(dimension_semantics=("arbitrary",)),
        name="moba_select",
    )(q_bd, kmean2d)


def _attn_sample_kernel(pt_ref, sel_ref, tab_ref, q_ref, kn_ref, vn_ref, badj_ref, bown_ref, ck_ref, cv_ref,
                        o_ref, kbuf, vbuf, sem, *, nblk, t_new):
    b = pl.program_id(0)
    h = pl.program_id(1)
    n_sel = MOBA_TOPK

    def block_of(t, n):
        return sel_ref[((b * N_HEADS + h) * t_new + t) * n_sel + n]

    def copies(t, n):
        j = block_of(t, n)
        slot = t * n_sel + n
        out = []
        for pg in range(PAGES_PER_BLOCK):
            page = pt_ref[b, j * PAGES_PER_BLOCK + pg]
            keys = pl.ds(pg * PAGE_SIZE, PAGE_SIZE)
            out.append(pltpu.make_async_copy(ck_ref.at[0, page, h], kbuf.at[slot, :, keys], sem.at[0]))
            out.append(pltpu.make_async_copy(cv_ref.at[0, page, h], vbuf.at[slot, :, keys], sem.at[1]))
        return out

    for t in range(t_new):
        for n in range(n_sel):
            for c in copies(t, n):
                c.start()
    for t in range(t_new):
        for n in range(n_sel):
            for c in copies(t, n):
                c.wait()

    c_far = tab_ref[h, N_BUCKETS - 1]
    kn = kn_ref[0, 0]
    vn = vn_ref[0, 0]
    out_lane = lax.broadcasted_iota(jnp.int32, (HEAD_DIM, LANES), 1)
    out = jnp.zeros((HEAD_DIM, LANES), F32)
    for t in range(t_new):
        qt = q_ref[0, 0, :, t:t + 1]
        s_own = jnp.sum(kn * qt, axis=0, keepdims=True) + bown_ref[0, t:t + 1, 0:t_new]
        scores = []
        for n in range(n_sel):
            j = block_of(t, n)
            s = jnp.sum(kbuf[t * n_sel + n] * qt, axis=0, keepdims=True)
            scores.append(s + jnp.where(j == nblk - 1, badj_ref[0, t:t + 1, :], c_far))
        m = jnp.max(s_own, axis=-1, keepdims=True)
        for s in scores:
            m = jnp.maximum(m, jnp.max(s, axis=-1, keepdims=True))
        p_own = jnp.exp(s_own - m)
        l = jnp.sum(p_own, axis=-1, keepdims=True)
        acc = jnp.sum(vn * p_own, axis=-1, keepdims=True)
        pv = None
        for n, s in enumerate(scores):
            p = jnp.exp(s - m)
            l = l + jnp.sum(p, axis=-1, keepdims=True)
            term = vbuf[t * n_sel + n] * p
            pv = term if pv is None else pv + term
        acc = acc + jnp.sum(pv, axis=-1, keepdims=True)
        out = jnp.where(out_lane == t, acc / l, out)
    o_ref[0, 0] = out


def _attn_sample(page_table, sel_flat, tab_t, q_t, kn_t, vn_t, badj, bown, cache_kt, cache_vt, nblk):
    db, nh, dh, t_new = q_t.shape
    head_spec = pl.BlockSpec((1, 1, dh, t_new), lambda b, h, pt, sel: (b, h, 0, 0))
    grid_spec = pltpu.PrefetchScalarGridSpec(
        num_scalar_prefetch=2,
        grid=(db, nh),
        in_specs=[pl.BlockSpec(memory_space=pltpu.SMEM), head_spec, head_spec, head_spec,
                  pl.BlockSpec((1,) + badj.shape[1:], lambda b, h, pt, sel: (h, 0, 0)),
                  pl.BlockSpec((1,) + bown.shape[1:], lambda b, h, pt, sel: (h, 0, 0)),
                  pl.BlockSpec(memory_space=pl.ANY), pl.BlockSpec(memory_space=pl.ANY)],
        out_specs=pl.BlockSpec((1, 1, dh, LANES), lambda b, h, pt, sel: (b, h, 0, 0)),
        scratch_shapes=[pltpu.VMEM((t_new * MOBA_TOPK, dh, MOBA_BLOCK), F32),
                        pltpu.VMEM((t_new * MOBA_TOPK, dh, MOBA_BLOCK), F32),
                        pltpu.SemaphoreType.DMA((2,))],
    )
    return pl.pallas_call(
        functools.partial(_attn_sample_kernel, nblk=nblk, t_new=t_new),
        grid_spec=grid_spec,
        out_shape=jax.ShapeDtypeStruct((db, nh, dh, LANES), F32),
        compiler_params=pltpu.CompilerParams(dimension_semantics=("arbitrary", "arbitrary"),
                                             vmem_limit_bytes=VMEM_LIMIT),
        name="moba_sample",
    )(page_table, sel_flat, tab_t, q_t, kn_t, vn_t, badj, bown, cache_kt, cache_vt)


def kernel(x_prompt, x_sample, cache_k, cache_v, state_pool, page_table, c_prompt, c_sample, w_ada, b_ada, norm_mix, w_in, rel_bias, w_pool, pool_scale, w_out, norm_mlp, w_up, w_down, norm_final):
    assert w_ada.shape[0] == 1, "single-layer decoder"
    b, s, d = x_prompt.shape
    db, t_new, _ = x_sample.shape
    n_pages = page_table.shape[1]
    past = n_pages * PAGE_SIZE
    assert past % MOBA_BLOCK == 0 and past // MOBA_BLOCK >= MOBA_TOPK
    nblk = past // MOBA_BLOCK

    w_in_b = w_in[0].astype(BF16)
    w_out_b = w_out[0].astype(BF16)
    w_up_b = w_up[0].astype(BF16)
    w_down_b = w_down[0].astype(BF16)
    w_pool_b = w_pool[0].astype(BF16)
    g_final = norm_final.reshape(1, d)
    tab_t = rel_bias.T

    mods = _ada(jnp.concatenate([c_prompt, c_sample], axis=0), w_ada, b_ada[0:1])
    mods_p = mods[:b].reshape(b, 1, N_MOD * d)
    mods_s = jnp.repeat(mods[b:], t_new, axis=0).reshape(1, db * t_new, N_MOD * d)
    bias_own, bias_adj = _bias_tiles(tab_t)

    q_p, k_p, v_p, u_p = _inproj(x_prompt, mods_p, norm_mix, w_in_b)
    attn_p = _attn_prompt(tab_t, q_p, k_p, v_p, bias_own, bias_adj)
    pool_p = _pool(u_p, w_pool_b, pool_scale, first_pos=0)
    y_p = _out(x_prompt, attn_p, pool_p, mods_p, norm_mlp, g_final, w_out_b, w_up_b, w_down_b)

    xs = x_sample.reshape(1, db * t_new, d)
    q_s, k_s, v_s, u_s = _inproj(xs, mods_s, norm_mix, w_in_b)
    cache_kt = cache_k.transpose(0, 1, 3, 4, 2)
    cache_vt = cache_v.transpose(0, 1, 3, 4, 2)
    kmean = _kmean(page_table, cache_kt)
    head_of_lane = jnp.arange(ATTN_WIDTH) // HEAD_DIM
    q_bt = q_s.reshape(db, 1, t_new, ATTN_WIDTH)
    q_bd = jnp.where(head_of_lane[None, None, None, :] == jnp.arange(N_HEADS)[None, :, None, None],
                     q_bt, jnp.zeros_like(q_bt)).reshape(db, N_HEADS * t_new, ATTN_WIDTH)
    sel = _select(q_bd, kmean.reshape(db, nblk, ATTN_WIDTH))[:, :, :MOBA_TOPK]
    per_head = lambda z: z.reshape(db, t_new, N_HEADS, HEAD_DIM).transpose(0, 2, 3, 1).astype(F32)
    assert t_new <= 8
    attn_s = _attn_sample(page_table, sel.reshape(-1), tab_t, per_head(q_s), per_head(k_s), per_head(v_s),
                          bias_adj[:, :8, :], bias_own[:, :8, :LANES], cache_kt, cache_vt, nblk)
    attn_s = attn_s[..., :t_new].transpose(0, 3, 1, 2).reshape(1, db * t_new, ATTN_WIDTH).astype(BF16)

    u_full = jnp.concatenate([state_pool[0], u_s.reshape(db, t_new, POOL_WIDTH)], axis=1)
    ext_rows = POOL_HALO + 8
    u_ext = jnp.pad(u_full, ((0, 0), (1, ext_rows - 1 - u_full.shape[1]), (0, 0)))
    pool_ext = _pool(u_ext.reshape(1, db * ext_rows, POOL_WIDTH), w_pool_b, pool_scale, first_pos=past)
    pool_s = pool_ext.reshape(db, ext_rows, POOL_WIDTH)[:, POOL_HALO:POOL_HALO + t_new]
    pool_s = pool_s.reshape(1, db * t_new, POOL_WIDTH)
    y_s = _out(xs, attn_s, pool_s, mods_s, norm_mlp, g_final, w_out_b, w_up_b, w_down_b)

    heads = lambda z, n, l: z.reshape(1, n, l, N_HEADS, HEAD_DIM)
    return (y_p, y_s.reshape(db, t_new, d),
            heads(k_p, b, s), heads(v_p, b, s), u_p[:, -POOL_STATE:][None],
            heads(k_s, db, t_new), heads(v_s, db, t_new), u_full[:, -POOL_STATE:][None])
```

```python
import functools
import math

import numpy as np
import jax
import jax.numpy as jnp
from jax import lax
from jax.experimental import pallas as pl
from jax.experimental.pallas import tpu as pltpu

HEAD_DIM = 64
N_HEADS = 8
ATTN_WIDTH = N_HEADS * HEAD_DIM
POOL_WINDOWS = (2, 4, 8, 16)
POOL_GROUP = 128
POOL_WIDTH = POOL_GROUP * len(POOL_WINDOWS)
POOL_STATE = max(POOL_WINDOWS) - 1
POOL_HALO = POOL_STATE + 1
MOBA_BLOCK = 256
MOBA_TOPK = 3
N_BUCKETS = 32
MAX_DISTANCE = 128
PAGE_SIZE = 128
PAGES_PER_BLOCK = MOBA_BLOCK // PAGE_SIZE
N_MOD = 6
EPS = 1e-6
NEG = -1e30
SCALE = HEAD_DIM ** -0.5
LOG2E = math.log2(math.e)
Q_SCALE = SCALE * LOG2E
LANES = 128
FF_CHUNK = 1024
VMEM_LIMIT = 56 * 1024 * 1024

BF16 = jnp.bfloat16
F32 = jnp.float32


def _nt_dot(a, b):
    return lax.dot_general(a, b, (((1,), (1,)), ((), ())), preferred_element_type=F32)


def _dot(a, b):
    return jnp.dot(a, b, preferred_element_type=F32)


def _row_tile(n, candidates=(512, 256, 128, 64, 32, 16, 8)):
    for c in candidates:
        if n % c == 0:
            return c
    raise ValueError(f"row count {n} is not a multiple of 8")


def _t5_bucket_np(rel):
    n = np.maximum(rel, 0)
    max_exact = N_BUCKETS // 2
    nf = np.maximum(n, max_exact).astype(np.float32)
    large = max_exact + (np.log(nf / np.float32(max_exact)) / np.float32(math.log(MAX_DISTANCE / max_exact))
                         * np.float32(N_BUCKETS - max_exact)).astype(np.int32)
    large = np.minimum(large, N_BUCKETS - 1)
    return np.where(n < max_exact, n, large).astype(np.int32)


def _rms(x, g):
    return x * lax.rsqrt(jnp.mean(x * x, axis=-1, keepdims=True) + EPS) * g


def _ada_kernel(c_ref, w_ref, b_ref, o_ref):
    c = c_ref[...]
    s = c / (1.0 + jnp.exp(-c))
    o_ref[...] = _dot(s.astype(BF16), w_ref[0].astype(BF16)) + b_ref[...]


def _ada(c_all, w_ada, b_ada):
    n, d = c_all.shape
    width = w_ada.shape[-1]
    tn = 1024
    return pl.pallas_call(
        _ada_kernel,
        grid=(width // tn,),
        in_specs=[pl.BlockSpec((n, d), lambda j: (0, 0)),
                  pl.BlockSpec((1, d, tn), lambda j: (0, 0, j)),
                  pl.BlockSpec((1, tn), lambda j: (0, j))],
        out_specs=pl.BlockSpec((n, tn), lambda j: (0, j)),
        out_shape=jax.ShapeDtypeStruct((n, width), F32),
        compiler_params=pltpu.CompilerParams(dimension_semantics=("arbitrary",), vmem_limit_bytes=VMEM_LIMIT),
        name="ada_mod",
    )(c_all, w_ada, b_ada)


def _inproj_kernel(x_ref, mods_ref, g_ref, w_ref, q_ref, k_ref, v_ref, u_ref):
    d = x_ref.shape[-1]
    x = x_ref[0]
    shift = mods_ref[0, :, 0:d]
    scale = mods_ref[0, :, d:2 * d]
    h = _rms(x, g_ref[...]) * (1.0 + scale) + shift
    r = _dot(h.astype(BF16), w_ref[...])
    a = ATTN_WIDTH
    q_ref[0] = (r[:, 0:a] * Q_SCALE).astype(BF16)
    k_ref[0] = r[:, a:2 * a]
    v_ref[0] = r[:, 2 * a:3 * a]
    u_ref[0] = r[:, 3 * a:]


def _inproj(x, mods, g, w_in_b):
    nb, s, d = x.shape
    r = mods.shape[1]
    ts = _row_tile(s)
    width = w_in_b.shape[1]
    row_spec = lambda w: pl.BlockSpec((1, ts, w), lambda b, i: (b, i, 0))
    mods_spec = (pl.BlockSpec((1, 1, N_MOD * d), lambda b, i: (b, 0, 0)) if r == 1
                 else pl.BlockSpec((1, ts, N_MOD * d), lambda b, i: (b, i, 0)))
    return pl.pallas_call(
        _inproj_kernel,
        grid=(nb, s // ts),
        in_specs=[row_spec(d), mods_spec,
                  pl.BlockSpec((1, d), lambda b, i: (0, 0)),
                  pl.BlockSpec((d, width), lambda b, i: (0, 0))],
        out_specs=[row_spec(ATTN_WIDTH), row_spec(ATTN_WIDTH), row_spec(ATTN_WIDTH), row_spec(POOL_WIDTH)],
        out_shape=[jax.ShapeDtypeStruct((nb, s, ATTN_WIDTH), BF16),
                   jax.ShapeDtypeStruct((nb, s, ATTN_WIDTH), F32),
                   jax.ShapeDtypeStruct((nb, s, ATTN_WIDTH), F32),
                   jax.ShapeDtypeStruct((nb, s, POOL_WIDTH), F32)],
        compiler_params=pltpu.CompilerParams(dimension_semantics=("arbitrary", "arbitrary"),
                                             vmem_limit_bytes=VMEM_LIMIT),
        name="in_proj",
    )(x, mods, g, w_in_b)


def _bias_kernel(tab_ref, idx_own_ref, idx_adj_ref, own_ref, adj_ref):
    h = pl.program_id(0)
    io = idx_own_ref[...]
    ia = idx_adj_ref[...]
    bo = jnp.zeros(io.shape, F32)
    ba = jnp.zeros(ia.shape, F32)
    for b in range(N_BUCKETS):
        t = tab_ref[h, b]
        bo = jnp.where(io == b, t, bo)
        ba = jnp.where(ia == b, t, ba)
    own_ref[0] = jnp.where(io < 0, NEG, bo * LOG2E)
    adj_ref[0] = ba * LOG2E


def _bias_tiles(tab_t):
    blk = MOBA_BLOCK
    r = np.arange(blk)[None, :]
    c = np.arange(blk)[:, None]
    idx_own = np.where(r >= c, _t5_bucket_np(r - c), -1).astype(np.int32)
    idx_adj = _t5_bucket_np(blk + r - c)
    tile = pl.BlockSpec((blk, blk), lambda h: (0, 0))
    out = pl.BlockSpec((1, blk, blk), lambda h: (h, 0, 0))
    return pl.pallas_call(
        _bias_kernel,
        grid=(N_HEADS,),
        in_specs=[pl.BlockSpec(memory_space=pltpu.SMEM), tile, tile],
        out_specs=[out, out],
        out_shape=[jax.ShapeDtypeStruct((N_HEADS, blk, blk), F32)] * 2,
        compiler_params=pltpu.CompilerParams(dimension_semantics=("arbitrary",)),
        name="rel_bias_tiles",
    )(tab_t, jnp.asarray(idx_own), jnp.asarray(idx_adj))


def _attn_prompt_kernel(tab_ref, q_ref, k_ref, v_ref, bown_ref, badj_ref, o_ref,
                        kb_ref, vt_ref, km_ref, qm_ref, madd_ref, m_ref, l_ref, acc_ref, s_ref, *, nb):
    i = pl.program_id(1)
    blk = MOBA_BLOCK
    nbp = km_ref.shape[0]

    @pl.when(i == 0)
    def _():
        kb_ref[...] = k_ref[0].astype(BF16)
        for j in range(nb):
            vt_ref[j] = v_ref[0, j * blk:(j + 1) * blk, :].T.astype(BF16)
        means = [jnp.mean(k_ref[0, j * blk:(j + 1) * blk, :], axis=0, keepdims=True) for j in range(nb)]
        means += [jnp.zeros_like(means[0])] * (nbp - nb)
        km_ref[...] = jnp.concatenate(means, axis=0)

    lane = lax.broadcasted_iota(jnp.int32, (blk, LANES), 1)
    blk_row = lax.broadcasted_iota(jnp.int32, (nbp, blk), 0)
    valid = blk_row < i
    jm1 = jnp.maximum(i - 1, 0)
    own0 = pl.multiple_of(i * blk, blk)
    adj0 = pl.multiple_of(jm1 * blk, blk)
    heads_per_vreg = LANES // HEAD_DIM

    def lanes_of(h):
        p = h // heads_per_vreg
        return slice(p * LANES, (p + 1) * LANES)

    for h in range(N_HEADS):
        hh = h % heads_per_vreg
        q2 = q_ref[0, :, lanes_of(h)]
        hmask = (lane >= hh * HEAD_DIM) & (lane < (hh + 1) * HEAD_DIM)
        qm = jnp.where(hmask, q2, jnp.zeros_like(q2))
        qm_ref[h] = qm
        gate = _nt_dot(km_ref[:, lanes_of(h)].astype(BF16), qm)
        for j in range(nb):
            gj = gate[j:j + 1, :]
            beats = ((gate > gj) | ((gate == gj) & (blk_row < j))) & valid
            cnt = jnp.sum(beats.astype(F32), axis=0, keepdims=True)
            cnt = cnt + jnp.where(j < i, 0.0, float(nb))
            madd_ref[h, j:j + 1, :] = jnp.where(cnt < MOBA_TOPK, 0.0, NEG)

    def key_block(key_rows, blk_idx, bias_of, first=False):
        for h in range(N_HEADS):
            s_ref[h] = _nt_dot(kb_ref[key_rows, lanes_of(h)], qm_ref[h]) + bias_of(h)
        for h in range(N_HEADS):
            s = s_ref[h]
            vt = vt_ref[blk_idx, h * HEAD_DIM:(h + 1) * HEAD_DIM, :]
            m_blk = jnp.max(s, axis=0, keepdims=True)
            if first:
                m_new = m_blk
                p = jnp.exp2(s - m_new)
                l_ref[h] = jnp.sum(p, axis=0, keepdims=True)
                acc_ref[h] = _dot(vt, p.astype(BF16))
            else:
                m_old = m_ref[h]
                m_new = jnp.maximum(m_old, m_blk)
                alpha = jnp.exp2(m_old - m_new)
                p = jnp.exp2(s - m_new)
                l_ref[h] = alpha * l_ref[h] + jnp.sum(p, axis=0, keepdims=True)
                acc_ref[h] = alpha * acc_ref[h] + _dot(vt, p.astype(BF16))
            m_ref[h] = m_new

    key_block(pl.ds(own0, blk), i, lambda h: bown_ref[h], first=True)
    key_block(pl.ds(adj0, blk), jm1, lambda h: badj_ref[h] + madd_ref[h, pl.ds(jm1, 1), :])

    def far(j, carry):
        j0 = pl.multiple_of(j * blk, blk)
        key_block(pl.ds(j0, blk), j, lambda h: madd_ref[h, pl.ds(j, 1), :] + tab_ref[h, N_BUCKETS - 1] * LOG2E)
        return carry

    lax.fori_loop(0, jm1, far, 0)

    for p in range(ATTN_WIDTH // LANES):
        heads = range(p * heads_per_vreg, (p + 1) * heads_per_vreg)
        out_t = jnp.concatenate([acc_ref[h] / l_ref[h] for h in heads], axis=0)
        o_ref[0, :, p * LANES:(p + 1) * LANES] = out_t.T.astype(BF16)


def _attn_prompt(tab_t, q, k, v, bias_own_t, bias_adj_t):
    b, s, a = q.shape
    blk = MOBA_BLOCK
    nb = s // blk
    nbp = 8
    assert s % blk == 0 and nb <= nbp
    full = pl.BlockSpec((1, s, a), lambda bi, i: (bi, 0, 0))
    tile = pl.BlockSpec((1, blk, a), lambda bi, i: (bi, i, 0))
    bias = pl.BlockSpec((N_HEADS, blk, blk), lambda bi, i: (0, 0, 0))
    return pl.pallas_call(
        functools.partial(_attn_prompt_kernel, nb=nb),
        grid=(b, nb),
        in_specs=[pl.BlockSpec(memory_space=pltpu.SMEM), tile, full, full, bias, bias],
        out_specs=tile,
        out_shape=jax.ShapeDtypeStruct((b, s, a), BF16),
        scratch_shapes=[pltpu.VMEM((s, a), BF16), pltpu.VMEM((nb, a, blk), BF16),
                        pltpu.VMEM((nbp, a), F32), pltpu.VMEM((N_HEADS, blk, LANES), BF16),
                        pltpu.VMEM((N_HEADS, nbp, blk), F32), pltpu.VMEM((N_HEADS, 1, blk), F32),
                        pltpu.VMEM((N_HEADS, 1, blk), F32), pltpu.VMEM((N_HEADS, HEAD_DIM, blk), F32),
                        pltpu.VMEM((N_HEADS, blk, blk), F32)],
        compiler_params=pltpu.CompilerParams(dimension_semantics=("arbitrary", "arbitrary"),
                                             vmem_limit_bytes=VMEM_LIMIT),
        name="moba_prompt",
    )(tab_t, q, k, v, bias_own_t, bias_adj_t)


def _pool_kernel(u_ref, hist_ref, w_ref, scale_ref, o_ref, *, first_pos):
    i = pl.program_id(1)
    tp = u_ref.shape[1]
    u = u_ref[0]
    hist = jnp.where(i > 0, hist_ref[0], 0.0)
    ext = jnp.concatenate([hist, u], axis=0)
    pos = first_pos + i * tp + lax.broadcasted_iota(jnp.int32, (tp, 1), 0)
    for g, w in enumerate(POOL_WINDOWS):
        gs = slice(g * POOL_GROUP, (g + 1) * POOL_GROUP)
        s = ext[:, gs]
        shift = 1
        while shift < w:
            s = s + pltpu.roll(s, shift, 0)
            shift *= 2
        cnt = jnp.minimum(pos + 1, w).astype(F32)
        d = s[POOL_HALO:, :] / cnt - u[:, gs]
        y = _dot(d.astype(BF16), w_ref[g]) * scale_ref[:, gs]
        o_ref[0, :, gs] = y.astype(BF16)


def _pool(u, w_pool_b, pool_scale, first_pos):
    nb, s, c = u.shape
    tp = _row_tile(s, (512, 256, 128, 64, 32, 16))
    hb = tp // POOL_HALO
    return pl.pallas_call(
        functools.partial(_pool_kernel, first_pos=first_pos),
        grid=(nb, s // tp),
        in_specs=[pl.BlockSpec((1, tp, c), lambda b, i: (b, i, 0)),
                  pl.BlockSpec((1, POOL_HALO, c), lambda b, i: (b, jnp.maximum(i * hb - 1, 0), 0)),
                  pl.BlockSpec(w_pool_b.shape, lambda b, i: (0, 0, 0)),
                  pl.BlockSpec((1, c), lambda b, i: (0, 0))],
        out_specs=pl.BlockSpec((1, tp, c), lambda b, i: (b, i, 0)),
        out_shape=jax.ShapeDtypeStruct((nb, s, c), BF16),
        compiler_params=pltpu.CompilerParams(dimension_semantics=("arbitrary", "arbitrary"),
                                             vmem_limit_bytes=VMEM_LIMIT),
        name="multi_pool",
    )(u, u, w_pool_b, pool_scale)


def _out_kernel(x_ref, attn_ref, pool_ref, mods_ref, gm_ref, gf_ref, wo_ref, wu_ref, wd_ref, y_ref):
    d = x_ref.shape[-1]
    a = ATTN_WIDTH
    x = x_ref[0]
    g1 = mods_ref[0, :, 2 * d:3 * d]
    sh2 = mods_ref[0, :, 3 * d:4 * d]
    sc2 = mods_ref[0, :, 4 * d:5 * d]
    g2 = mods_ref[0, :, 5 * d:6 * d]
    mix = _dot(attn_ref[0], wo_ref[0:a, :]) + _dot(pool_ref[0], wo_ref[a:, :])
    x1 = x + g1 * mix
    hb = (_rms(x1, gm_ref[...]) * (1.0 + sc2) + sh2).astype(BF16)
    acc = jnp.zeros(x.shape, F32)
    for c in range(wu_ref.shape[1] // FF_CHUNK):
        cs = slice(c * FF_CHUNK, (c + 1) * FF_CHUNK)
        t = jnp.maximum(_dot(hb, wu_ref[:, cs]), 0.0)
        acc = acc + _dot((t * t).astype(BF16), wd_ref[cs, :])
    x2 = x1 + g2 * acc
    y_ref[0] = _rms(x2, gf_ref[...])


def _out(x, attn, pool, mods, g_mlp, g_final, w_out_b, w_up_b, w_down_b):
    nb, s, d = x.shape
    r = mods.shape[1]
    tm = _row_tile(s)
    row_spec = lambda w: pl.BlockSpec((1, tm, w), lambda b, i: (b, i, 0))
    mods_spec = (pl.BlockSpec((1, 1, N_MOD * d), lambda b, i: (b, 0, 0)) if r == 1
                 else pl.BlockSpec((1, tm, N_MOD * d), lambda b, i: (b, i, 0)))
    resident = lambda w: pl.BlockSpec(w.shape, lambda b, i: (0, 0), pipeline_mode=pl.Buffered(1))
    vec = pl.BlockSpec((1, d), lambda b, i: (0, 0))
    return pl.pallas_call(
        _out_kernel,
        grid=(nb, s // tm),
        in_specs=[row_spec(d), row_spec(ATTN_WIDTH), row_spec(POOL_WIDTH), mods_spec, vec, vec,
                  resident(w_out_b), resident(w_up_b), resident(w_down_b)],
        out_specs=row_spec(d),
        out_shape=jax.ShapeDtypeStruct((nb, s, d), F32),
        compiler_params=pltpu.CompilerParams(dimension_semantics=("arbitrary", "arbitrary"),
                                             vmem_limit_bytes=VMEM_LIMIT),
        name="out_mlp",
    )(x, attn, pool, mods, g_mlp, g_final, w_out_b, w_up_b, w_down_b)


KMEAN_PAGES_PER_STEP = 16


def _kmean_kernel(pt_ref, *refs):
    page_refs, o_ref = refs[:-1], refs[-1]
    for blk in range(len(page_refs) // PAGES_PER_BLOCK):
        s = page_refs[blk * PAGES_PER_BLOCK][0, 0]
        for pg in range(1, PAGES_PER_BLOCK):
            s = s + page_refs[blk * PAGES_PER_BLOCK + pg][0, 0]
        o_ref[0, blk] = jnp.sum(s, axis=-1) * (1.0 / MOBA_BLOCK)


def _kmean(page_table, cache_kt):
    db, n_pages = page_table.shape
    nblk = n_pages // PAGES_PER_BLOCK
    pps = math.gcd(KMEAN_PAGES_PER_STEP, n_pages)
    assert pps % PAGES_PER_BLOCK == 0
    page_shape = (1, 1) + cache_kt.shape[2:]

    def page_spec(n):
        return pl.BlockSpec(page_shape, lambda b, s, pt: (0, pt[b, s * pps + n], 0, 0, 0))

    grid_spec = pltpu.PrefetchScalarGridSpec(
        num_scalar_prefetch=1,
        grid=(db, n_pages // pps),
        in_specs=[page_spec(n) for n in range(pps)],
        out_specs=pl.BlockSpec((1, pps // PAGES_PER_BLOCK, N_HEADS, HEAD_DIM), lambda b, s, pt: (b, s, 0, 0)),
    )
    return pl.pallas_call(
        _kmean_kernel,
        grid_spec=grid_spec,
        out_shape=jax.ShapeDtypeStruct((db, nblk, N_HEADS, HEAD_DIM), F32),
        compiler_params=pltpu.CompilerParams(dimension_semantics=("arbitrary", "arbitrary"),
                                             vmem_limit_bytes=VMEM_LIMIT),
        name="paged_kmean",
    )(page_table, *([cache_kt] * pps))


def _select_kernel(q_ref, km_ref, sel_ref):
    g = lax.dot_general(q_ref[0].astype(F32), km_ref[0], (((1,), (1,)), ((), ())),
                        precision=lax.Precision.HIGHEST, preferred_element_type=F32)
    rows, nblk = g.shape
    lane = lax.broadcasted_iota(jnp.int32, g.shape, 1)
    out_lane = lax.broadcasted_iota(jnp.int32, (rows, LANES), 1)
    out = jnp.zeros((rows, LANES), jnp.int32)
    for n in range(MOBA_TOPK):
        mx = jnp.max(g, axis=-1, keepdims=True)
        idx = jnp.min(jnp.where(g == mx, lane, nblk), axis=-1, keepdims=True)
        out = jnp.where(out_lane == n, idx, out)
        g = jnp.where(lane == idx, -jnp.inf, g)
    sel_ref[0] = out


def _select(q_bd, kmean2d):
    db, rows, a = q_bd.shape
    nblk = kmean2d.shape[1]
    return pl.pallas_call(
        _select_kernel,
        grid=(db,),
        in_specs=[pl.BlockSpec((1, rows, a), lambda b: (b, 0, 0)),
                  pl.BlockSpec((1, nblk, a), lambda b: (b, 0, 0))],
        out_specs=pl.BlockSpec((1, rows, LANES), lambda b: (b, 0, 0)),
        out_shape=jax.ShapeDtypeStruct((db, rows, LANES), jnp.int32),
        compiler_params=pltpu.CompilerParams(dimension_semantics=("arbitrary",)),
        name="moba_select",
    )(q_bd, kmean2d)


def _attn_sample_kernel(pt_ref, sel_ref, tab_ref, q_ref, kn_ref, vn_ref, badj_ref, bown_ref, ck_ref, cv_ref,
                        o_ref, kbuf, vbuf, sem, *, nblk, t_new):
    b = pl.program_id(0)
    h = pl.program_id(1)
    n_sel = MOBA_TOPK
    step = b * N_HEADS + h
    n_steps = pl.num_programs(0) * N_HEADS
    buf = step % 2

    def block_of(t, n):
        return sel_ref[(step * t_new + t) * n_sel + n]

    def copies(st, bf):
        sb = st // N_HEADS
        sh = st % N_HEADS
        out = []
        for t in range(t_new):
            for n in range(n_sel):
                j = sel_ref[(st * t_new + t) * n_sel + n]
                slot = bf * (t_new * n_sel) + t * n_sel + n
                for pg in range(PAGES_PER_BLOCK):
                    page = pt_ref[sb, j * PAGES_PER_BLOCK + pg]
                    keys = pl.ds(pg * PAGE_SIZE, PAGE_SIZE)
                    out.append(pltpu.make_async_copy(ck_ref.at[0, page, sh], kbuf.at[slot, :, keys], sem.at[0, bf]))
                    out.append(pltpu.make_async_copy(cv_ref.at[0, page, sh], vbuf.at[slot, :, keys], sem.at[1, bf]))
        return out

    @pl.when(step == 0)
    def _():
        for c in copies(step, buf):
            c.start()

    @pl.when(step + 1 < n_steps)
    def _():
        for c in copies(step + 1, 1 - buf):
            c.start()

    for c in copies(step, buf):
        c.wait()

    slot0 = buf * (t_new * n_sel)
    c_far = tab_ref[h, N_BUCKETS - 1] * LOG2E
    kn = kn_ref[0, 0]
    vn = vn_ref[0, 0]
    out_lane = lax.broadcasted_iota(jnp.int32, (HEAD_DIM, LANES), 1)
    out = jnp.zeros((HEAD_DIM, LANES), F32)
    for t in range(t_new):
        qt = q_ref[0, 0, :, t:t + 1]
        s_own = jnp.sum(kn * qt, axis=0, keepdims=True) + bown_ref[0, t:t + 1, 0:t_new]
        scores = []
        for n in range(n_sel):
            j = block_of(t, n)
            s = jnp.sum(kbuf[slot0 + t * n_sel + n] * qt, axis=0, keepdims=True)
            scores.append(s + jnp.where(j == nblk - 1, badj_ref[0, t:t + 1, :], c_far))
        m = jnp.max(s_own, axis=-1, keepdims=True)
        for s in scores:
            m = jnp.maximum(m, jnp.max(s, axis=-1, keepdims=True))
        p_own = jnp.exp2(s_own - m)
        l = jnp.sum(p_own, axis=-1, keepdims=True)
        acc = jnp.sum(vn * p_own, axis=-1, keepdims=True)
        pv = None
        for n, s in enumerate(scores):
            p = jnp.exp2(s - m)
            l = l + jnp.sum(p, axis=-1, keepdims=True)
            term = vbuf[slot0 + t * n_sel + n] * p
            pv = term if pv is None else pv + term
        acc = acc + jnp.sum(pv, axis=-1, keepdims=True)
        out = jnp.where(out_lane == t, acc / l, out)
    o_ref[0, 0] = out


def _attn_sample(page_table, sel_flat, tab_t, q_t, kn_t, vn_t, badj, bown, cache_kt, cache_vt, nblk):
    db, nh, dh, t_new = q_t.shape
    head_spec = pl.BlockSpec((1, 1, dh, t_new), lambda b, h, pt, sel: (b, h, 0, 0))
    grid_spec = pltpu.PrefetchScalarGridSpec(
        num_scalar_prefetch=2,
        grid=(db, nh),
        in_specs=[pl.BlockSpec(memory_space=pltpu.SMEM), head_spec, head_spec, head_spec,
                  pl.BlockSpec((1,) + badj.shape[1:], lambda b, h, pt, sel: (h, 0, 0)),
                  pl.BlockSpec((1,) + bown.shape[1:], lambda b, h, pt, sel: (h, 0, 0)),
                  pl.BlockSpec(memory_space=pl.ANY), pl.BlockSpec(memory_space=pl.ANY)],
        out_specs=pl.BlockSpec((1, 1, dh, LANES), lambda b, h, pt, sel: (b, h, 0, 0)),
        scratch_shapes=[pltpu.VMEM((2 * t_new * MOBA_TOPK, dh, MOBA_BLOCK), F32),
                        pltpu.VMEM((2 * t_new * MOBA_TOPK, dh, MOBA_BLOCK), F32),
                        pltpu.SemaphoreType.DMA((2, 2))],
    )
    return pl.pallas_call(
        functools.partial(_attn_sample_kernel, nblk=nblk, t_new=t_new),
        grid_spec=grid_spec,
        out_shape=jax.ShapeDtypeStruct((db, nh, dh, LANES), F32),
        compiler_params=pltpu.CompilerParams(dimension_semantics=("arbitrary", "arbitrary"),
                                             vmem_limit_bytes=VMEM_LIMIT),
        name="moba_sample",
    )(page_table, sel_flat, tab_t, q_t, kn_t, vn_t, badj, bown, cache_kt, cache_vt)


def kernel(x_prompt, x_sample, cache_k, cache_v, state_pool, page_table, c_prompt, c_sample, w_ada, b_ada, norm_mix, w_in, rel_bias, w_pool, pool_scale, w_out, norm_mlp, w_up, w_down, norm_final):
    assert w_ada.shape[0] == 1, "single-layer decoder"
    b, s, d = x_prompt.shape
    db, t_new, _ = x_sample.shape
    n_pages = page_table.shape[1]
    past = n_pages * PAGE_SIZE
    assert past % MOBA_BLOCK == 0 and past // MOBA_BLOCK >= MOBA_TOPK
    nblk = past // MOBA_BLOCK

    w_in_b = w_in[0].astype(BF16)
    w_out_b = w_out[0].astype(BF16)
    w_up_b = w_up[0].astype(BF16)
    w_down_b = w_down[0].astype(BF16)
    w_pool_b = w_pool[0].astype(BF16)
    g_final = norm_final.reshape(1, d)
    tab_t = rel_bias.T

    mods = _ada(jnp.concatenate([c_prompt, c_sample], axis=0), w_ada, b_ada[0:1])
    mods_p = mods[:b].reshape(b, 1, N_MOD * d)
    mods_s = jnp.repeat(mods[b:], t_new, axis=0).reshape(1, db * t_new, N_MOD * d)
    bias_own, bias_adj = _bias_tiles(tab_t)

    q_p, k_p, v_p, u_p = _inproj(x_prompt, mods_p, norm_mix, w_in_b)
    attn_p = _attn_prompt(tab_t, q_p, k_p, v_p, bias_own, bias_adj)
    pool_p = _pool(u_p, w_pool_b, pool_scale, first_pos=0)
    y_p = _out(x_prompt, attn_p, pool_p, mods_p, norm_mlp, g_final, w_out_b, w_up_b, w_down_b)

    xs = x_sample.reshape(1, db * t_new, d)
    q_s, k_s, v_s, u_s = _inproj(xs, mods_s, norm_mix, w_in_b)
    cache_kt = cache_k.transpose(0, 1, 3, 4, 2)
    cache_vt = cache_v.transpose(0, 1, 3, 4, 2)
    kmean = _kmean(page_table, cache_kt)
    head_of_lane = jnp.arange(ATTN_WIDTH) // HEAD_DIM
    q_bt = q_s.reshape(db, 1, t_new, ATTN_WIDTH)
    q_bd = jnp.where(head_of_lane[None, None, None, :] == jnp.arange(N_HEADS)[None, :, None, None],
                     q_bt, jnp.zeros_like(q_bt)).reshape(db, N_HEADS * t_new, ATTN_WIDTH)
    sel = _select(q_bd, kmean.reshape(db, nblk, ATTN_WIDTH))[:, :, :MOBA_TOPK]
    per_head = lambda z: z.reshape(db, t_new, N_HEADS, HEAD_DIM).transpose(0, 2, 3, 1).astype(F32)
    assert t_new <= 8
    attn_s = _attn_sample(page_table, sel.reshape(-1), tab_t, per_head(q_s), per_head(k_s), per_head(v_s),
                          bias_adj[:, :, :8].transpose(0, 2, 1), bias_own[:, :LANES, :8].transpose(0, 2, 1),
                          cache_kt, cache_vt, nblk)
    attn_s = attn_s[..., :t_new].transpose(0, 3, 1, 2).reshape(1, db * t_new, ATTN_WIDTH).astype(BF16)

    u_full = jnp.concatenate([state_pool[0], u_s.reshape(db, t_new, POOL_WIDTH)], axis=1)
    ext_rows = POOL_HALO + 8
    u_ext = jnp.pad(u_full, ((0, 0), (1, ext_rows - 1 - u_full.shape[1]), (0, 0)))
    pool_ext = _pool(u_ext.reshape(1, db * ext_rows, POOL_WIDTH), w_pool_b, pool_scale, first_pos=past)
    pool_s = pool_ext.reshape(db, ext_rows, POOL_WIDTH)[:, POOL_HALO:POOL_HALO + t_new]
    pool_s = pool_s.reshape(1, db * t_new, POOL_WIDTH)
    y_s = _out(xs, attn_s, pool_s, mods_s, norm_mlp, g_final, w_out_b, w_up_b, w_down_b)

    heads = lambda z, n, l: z.reshape(1, n, l, N_HEADS, HEAD_DIM)
    return (y_p, y_s.reshape(db, t_new, d),
            heads(k_p, b, s), heads(v_p, b, s), u_p[:, -POOL_STATE:][None],
            heads(k_s, db, t_new), heads(v_s, db, t_new), u_full[:, -POOL_STATE:][None])
```

```python
import functools
import math

import numpy as np
import jax
import jax.numpy as jnp
from jax import lax
from jax.experimental import pallas as pl
from jax.experimental.pallas import tpu as pltpu

HEAD_DIM = 64
N_HEADS = 8
ATTN_WIDTH = N_HEADS * HEAD_DIM
POOL_WINDOWS = (2, 4, 8, 16)
POOL_GROUP = 128
POOL_WIDTH = POOL_GROUP * len(POOL_WINDOWS)
POOL_STATE = max(POOL_WINDOWS) - 1
POOL_HALO = POOL_STATE + 1
MOBA_BLOCK = 256
MOBA_TOPK = 3
N_BUCKETS = 32
MAX_DISTANCE = 128
PAGE_SIZE = 128
PAGES_PER_BLOCK = MOBA_BLOCK // PAGE_SIZE
N_MOD = 6
EPS = 1e-6
NEG = -1e30
SCALE = HEAD_DIM ** -0.5
LOG2E = math.log2(math.e)
Q_SCALE = SCALE * LOG2E
LANES = 128
FF_CHUNK = 1024
OUT_TILE_WITH_PAGES = 256
VMEM_LIMIT = 56 * 1024 * 1024

BF16 = jnp.bfloat16
F32 = jnp.float32


def _nt_dot(a, b):
    return lax.dot_general(a, b, (((1,), (1,)), ((), ())), preferred_element_type=F32)


def _dot(a, b):
    return jnp.dot(a, b, preferred_element_type=F32)


def _row_tile(n, candidates=(512, 256, 128, 64, 32, 16, 8)):
    for c in candidates:
        if n % c == 0:
            return c
    raise ValueError(f"row count {n} is not a multiple of 8")


def _t5_bucket_np(rel):
    n = np.maximum(rel, 0)
    max_exact = N_BUCKETS // 2
    nf = np.maximum(n, max_exact).astype(np.float32)
    large = max_exact + (np.log(nf / np.float32(max_exact)) / np.float32(math.log(MAX_DISTANCE / max_exact))
                         * np.float32(N_BUCKETS - max_exact)).astype(np.int32)
    large = np.minimum(large, N_BUCKETS - 1)
    return np.where(n < max_exact, n, large).astype(np.int32)


def _rms(x, g):
    return x * lax.rsqrt(jnp.mean(x * x, axis=-1, keepdims=True) + EPS) * g


def _ada_kernel(c_ref, w_ref, b_ref, o_ref):
    c = c_ref[...]
    s = c / (1.0 + jnp.exp(-c))
    o_ref[...] = _dot(s.astype(BF16), w_ref[0].astype(BF16)) + b_ref[...]


def _ada(c_all, w_ada, b_ada):
    n, d = c_all.shape
    width = w_ada.shape[-1]
    tn = 1024
    return pl.pallas_call(
        _ada_kernel,
        grid=(width // tn,),
        in_specs=[pl.BlockSpec((n, d), lambda j: (0, 0)),
                  pl.BlockSpec((1, d, tn), lambda j: (0, 0, j)),
                  pl.BlockSpec((1, tn), lambda j: (0, j))],
        out_specs=pl.BlockSpec((n, tn), lambda j: (0, j)),
        out_shape=jax.ShapeDtypeStruct((n, width), F32),
        compiler_params=pltpu.CompilerParams(dimension_semantics=("arbitrary",), vmem_limit_bytes=VMEM_LIMIT),
        name="ada_mod",
    )(c_all, w_ada, b_ada)


def _inproj_kernel(x_ref, mods_ref, g_ref, w_ref, q_ref, k_ref, v_ref, u_ref):
    d = x_ref.shape[-1]
    x = x_ref[0]
    shift = mods_ref[0, :, 0:d]
    scale = mods_ref[0, :, d:2 * d]
    h = _rms(x, g_ref[...]) * (1.0 + scale) + shift
    r = _dot(h.astype(BF16), w_ref[...])
    a = ATTN_WIDTH
    q_ref[0] = (r[:, 0:a] * Q_SCALE).astype(BF16)
    k_ref[0] = r[:, a:2 * a]
    v_ref[0] = r[:, 2 * a:3 * a]
    u_ref[0] = r[:, 3 * a:]


def _inproj(x, mods, g, w_in_b):
    nb, s, d = x.shape
    r = mods.shape[1]
    ts = _row_tile(s)
    width = w_in_b.shape[1]
    row_spec = lambda w: pl.BlockSpec((1, ts, w), lambda b, i: (b, i, 0))
    mods_spec = (pl.BlockSpec((1, 1, N_MOD * d), lambda b, i: (b, 0, 0)) if r == 1
                 else pl.BlockSpec((1, ts, N_MOD * d), lambda b, i: (b, i, 0)))
    return pl.pallas_call(
        _inproj_kernel,
        grid=(nb, s // ts),
        in_specs=[row_spec(d), mods_spec,
                  pl.BlockSpec((1, d), lambda b, i: (0, 0)),
                  pl.BlockSpec((d, width), lambda b, i: (0, 0))],
        out_specs=[row_spec(ATTN_WIDTH), row_spec(ATTN_WIDTH), row_spec(ATTN_WIDTH), row_spec(POOL_WIDTH)],
        out_shape=[jax.ShapeDtypeStruct((nb, s, ATTN_WIDTH), BF16),
                   jax.ShapeDtypeStruct((nb, s, ATTN_WIDTH), F32),
                   jax.ShapeDtypeStruct((nb, s, ATTN_WIDTH), F32),
                   jax.ShapeDtypeStruct((nb, s, POOL_WIDTH), F32)],
        compiler_params=pltpu.CompilerParams(dimension_semantics=("arbitrary", "arbitrary"),
                                             vmem_limit_bytes=VMEM_LIMIT),
        name="in_proj",
    )(x, mods, g, w_in_b)


def _bias_kernel(tab_ref, idx_own_ref, idx_adj_ref, own_ref, adj_ref):
    h = pl.program_id(0)
    io = idx_own_ref[...]
    ia = idx_adj_ref[...]
    bo = jnp.zeros(io.shape, F32)
    ba = jnp.zeros(ia.shape, F32)
    for b in range(N_BUCKETS):
        t = tab_ref[h, b]
        bo = jnp.where(io == b, t, bo)
        ba = jnp.where(ia == b, t, ba)
    own_ref[0] = jnp.where(io < 0, NEG, bo * LOG2E)
    adj_ref[0] = ba * LOG2E


def _bias_tiles(tab_t):
    blk = MOBA_BLOCK
    r = np.arange(blk)[None, :]
    c = np.arange(blk)[:, None]
    idx_own = np.where(r >= c, _t5_bucket_np(r - c), -1).astype(np.int32)
    idx_adj = _t5_bucket_np(blk + r - c)
    tile = pl.BlockSpec((blk, blk), lambda h: (0, 0))
    out = pl.BlockSpec((1, blk, blk), lambda h: (h, 0, 0))
    return pl.pallas_call(
        _bias_kernel,
        grid=(N_HEADS,),
        in_specs=[pl.BlockSpec(memory_space=pltpu.SMEM), tile, tile],
        out_specs=[out, out],
        out_shape=[jax.ShapeDtypeStruct((N_HEADS, blk, blk), F32)] * 2,
        compiler_params=pltpu.CompilerParams(dimension_semantics=("arbitrary",)),
        name="rel_bias_tiles",
    )(tab_t, jnp.asarray(idx_own), jnp.asarray(idx_adj))


def _attn_prompt_kernel(tab_ref, q_ref, k_ref, v_ref, bown_ref, badj_ref, o_ref,
                        kb_ref, vt_ref, km_ref, qm_ref, madd_ref, m_ref, l_ref, acc_ref, s_ref, *, nb):
    i = pl.program_id(1)
    blk = MOBA_BLOCK
    nbp = km_ref.shape[0]

    @pl.when(i == 0)
    def _():
        kb_ref[...] = k_ref[0].astype(BF16)
        for j in range(nb):
            vt_ref[j] = v_ref[0, j * blk:(j + 1) * blk, :].T.astype(BF16)
        means = [jnp.mean(k_ref[0, j * blk:(j + 1) * blk, :], axis=0, keepdims=True) for j in range(nb)]
        means += [jnp.zeros_like(means[0])] * (nbp - nb)
        km_ref[...] = jnp.concatenate(means, axis=0)

    lane = lax.broadcasted_iota(jnp.int32, (blk, LANES), 1)
    blk_row = lax.broadcasted_iota(jnp.int32, (nbp, blk), 0)
    valid = blk_row < i
    jm1 = jnp.maximum(i - 1, 0)
    own0 = pl.multiple_of(i * blk, blk)
    adj0 = pl.multiple_of(jm1 * blk, blk)
    heads_per_vreg = LANES // HEAD_DIM

    def lanes_of(h):
        p = h // heads_per_vreg
        return slice(p * LANES, (p + 1) * LANES)

    for h in range(N_HEADS):
        hh = h % heads_per_vreg
        q2 = q_ref[0, :, lanes_of(h)]
        hmask = (lane >= hh * HEAD_DIM) & (lane < (hh + 1) * HEAD_DIM)
        qm = jnp.where(hmask, q2, jnp.zeros_like(q2))
        qm_ref[h] = qm
        gate = _nt_dot(km_ref[:, lanes_of(h)].astype(BF16), qm)
        for j in range(nb):
            gj = gate[j:j + 1, :]
            beats = ((gate > gj) | ((gate == gj) & (blk_row < j))) & valid
            cnt = jnp.sum(beats.astype(F32), axis=0, keepdims=True)
            cnt = cnt + jnp.where(j < i, 0.0, float(nb))
            madd_ref[h, j:j + 1, :] = jnp.where(cnt < MOBA_TOPK, 0.0, NEG)

    def key_block(key_rows, blk_idx, bias_of, first=False):
        for h in range(N_HEADS):
            s_ref[h] = _nt_dot(kb_ref[key_rows, lanes_of(h)], qm_ref[h]) + bias_of(h)
        for h in range(N_HEADS):
            s = s_ref[h]
            vt = vt_ref[blk_idx, h * HEAD_DIM:(h + 1) * HEAD_DIM, :]
            m_blk = jnp.max(s, axis=0, keepdims=True)
            if first:
                m_new = m_blk
                p = jnp.exp2(s - m_new)
                l_ref[h] = jnp.sum(p, axis=0, keepdims=True)
                acc_ref[h] = _dot(vt, p.astype(BF16))
            else:
                m_old = m_ref[h]
                m_new = jnp.maximum(m_old, m_blk)
                alpha = jnp.exp2(m_old - m_new)
                p = jnp.exp2(s - m_new)
                l_ref[h] = alpha * l_ref[h] + jnp.sum(p, axis=0, keepdims=True)
                acc_ref[h] = alpha * acc_ref[h] + _dot(vt, p.astype(BF16))
            m_ref[h] = m_new

    key_block(pl.ds(own0, blk), i, lambda h: bown_ref[h], first=True)
    key_block(pl.ds(adj0, blk), jm1, lambda h: badj_ref[h] + madd_ref[h, pl.ds(jm1, 1), :])

    def far(j, carry):
        j0 = pl.multiple_of(j * blk, blk)
        key_block(pl.ds(j0, blk), j, lambda h: madd_ref[h, pl.ds(j, 1), :] + tab_ref[h, N_BUCKETS - 1] * LOG2E)
        return carry

    lax.fori_loop(0, jm1, far, 0)

    for p in range(ATTN_WIDTH // LANES):
        heads = range(p * heads_per_vreg, (p + 1) * heads_per_vreg)
        out_t = jnp.concatenate([acc_ref[h] / l_ref[h] for h in heads], axis=0)
        o_ref[0, :, p * LANES:(p + 1) * LANES] = out_t.T.astype(BF16)


def _attn_prompt(tab_t, q, k, v, bias_own_t, bias_adj_t):
    b, s, a = q.shape
    blk = MOBA_BLOCK
    nb = s // blk
    nbp = 8
    assert s % blk == 0 and nb <= nbp
    full = pl.BlockSpec((1, s, a), lambda bi, i: (bi, 0, 0))
    tile = pl.BlockSpec((1, blk, a), lambda bi, i: (bi, i, 0))
    bias = pl.BlockSpec((N_HEADS, blk, blk), lambda bi, i: (0, 0, 0))
    return pl.pallas_call(
        functools.partial(_attn_prompt_kernel, nb=nb),
        grid=(b, nb),
        in_specs=[pl.BlockSpec(memory_space=pltpu.SMEM), tile, full, full, bias, bias],
        out_specs=tile,
        out_shape=jax.ShapeDtypeStruct((b, s, a), BF16),
        scratch_shapes=[pltpu.VMEM((s, a), BF16), pltpu.VMEM((nb, a, blk), BF16),
                        pltpu.VMEM((nbp, a), F32), pltpu.VMEM((N_HEADS, blk, LANES), BF16),
                        pltpu.VMEM((N_HEADS, nbp, blk), F32), pltpu.VMEM((N_HEADS, 1, blk), F32),
                        pltpu.VMEM((N_HEADS, 1, blk), F32), pltpu.VMEM((N_HEADS, HEAD_DIM, blk), F32),
                        pltpu.VMEM((N_HEADS, blk, blk), F32)],
        compiler_params=pltpu.CompilerParams(dimension_semantics=("arbitrary", "arbitrary"),
                                             vmem_limit_bytes=VMEM_LIMIT),
        name="moba_prompt",
    )(tab_t, q, k, v, bias_own_t, bias_adj_t)


def _pool_kernel(u_ref, hist_ref, w_ref, scale_ref, o_ref, *, first_pos):
    i = pl.program_id(1)
    tp = u_ref.shape[1]
    u = u_ref[0]
    hist = jnp.where(i > 0, hist_ref[0], 0.0)
    ext = jnp.concatenate([hist, u], axis=0)
    pos = first_pos + i * tp + lax.broadcasted_iota(jnp.int32, (tp, 1), 0)
    for g, w in enumerate(POOL_WINDOWS):
        gs = slice(g * POOL_GROUP, (g + 1) * POOL_GROUP)
        s = ext[:, gs]
        shift = 1
        while shift < w:
            s = s + pltpu.roll(s, shift, 0)
            shift *= 2
        cnt = jnp.minimum(pos + 1, w).astype(F32)
        d = s[POOL_HALO:, :] / cnt - u[:, gs]
        y = _dot(d.astype(BF16), w_ref[g]) * scale_ref[:, gs]
        o_ref[0, :, gs] = y.astype(BF16)


def _pool(u, w_pool_b, pool_scale, first_pos):
    nb, s, c = u.shape
    tp = _row_tile(s, (512, 256, 128, 64, 32, 16))
    hb = tp // POOL_HALO
    return pl.pallas_call(
        functools.partial(_pool_kernel, first_pos=first_pos),
        grid=(nb, s // tp),
        in_specs=[pl.BlockSpec((1, tp, c), lambda b, i: (b, i, 0)),
                  pl.BlockSpec((1, POOL_HALO, c), lambda b, i: (b, jnp.maximum(i * hb - 1, 0), 0)),
                  pl.BlockSpec(w_pool_b.shape, lambda b, i: (0, 0, 0)),
                  pl.BlockSpec((1, c), lambda b, i: (0, 0))],
        out_specs=pl.BlockSpec((1, tp, c), lambda b, i: (b, i, 0)),
        out_shape=jax.ShapeDtypeStruct((nb, s, c), BF16),
        compiler_params=pltpu.CompilerParams(dimension_semantics=("arbitrary", "arbitrary"),
                                             vmem_limit_bytes=VMEM_LIMIT),
        name="multi_pool",
    )(u, u, w_pool_b, pool_scale)


def _page_block_mean(page_refs, blk):
    s = page_refs[blk * PAGES_PER_BLOCK][0, 0]
    for pg in range(1, PAGES_PER_BLOCK):
        s = s + page_refs[blk * PAGES_PER_BLOCK + pg][0, 0]
    return jnp.sum(s, axis=-1) * (1.0 / MOBA_BLOCK)


def _out_kernel(pt_ref, x_ref, attn_ref, pool_ref, mods_ref, gm_ref, gf_ref, wo_ref, wu_ref, wd_ref, *refs,
                n_pages):
    page_refs, y_ref = refs[:n_pages], refs[n_pages]
    n_blocks = n_pages // PAGES_PER_BLOCK
    d = x_ref.shape[-1]
    a = ATTN_WIDTH
    x = x_ref[0]
    g1 = mods_ref[0, :, 2 * d:3 * d]
    sh2 = mods_ref[0, :, 3 * d:4 * d]
    sc2 = mods_ref[0, :, 4 * d:5 * d]
    g2 = mods_ref[0, :, 5 * d:6 * d]
    mix = _dot(attn_ref[0], wo_ref[0:a, :]) + _dot(pool_ref[0], wo_ref[a:, :])
    x1 = x + g1 * mix
    hb = (_rms(x1, gm_ref[...]) * (1.0 + sc2) + sh2).astype(BF16)
    acc = jnp.zeros(x.shape, F32)
    n_chunks = wu_ref.shape[1] // FF_CHUNK
    for c in range(n_chunks):
        cs = slice(c * FF_CHUNK, (c + 1) * FF_CHUNK)
        t = jnp.maximum(_dot(hb, wu_ref[:, cs]), 0.0)
        acc = acc + _dot((t * t).astype(BF16), wd_ref[cs, :])
        for blk in range(c * n_blocks // n_chunks, (c + 1) * n_blocks // n_chunks):
            refs[n_pages + 1][0, blk] = _page_block_mean(page_refs, blk)
    x2 = x1 + g2 * acc
    y_ref[0] = _rms(x2, gf_ref[...])


def _out(x, attn, pool, mods, g_mlp, g_final, w_out_b, w_up_b, w_down_b, page_table, cache_kt=None, tm=None):
    nb, s, d = x.shape
    r = mods.shape[1]
    tm = tm or _row_tile(s)
    n_tiles = s // tm
    row_spec = lambda w: pl.BlockSpec((1, tm, w), lambda b, i, pt: (b, i, 0))
    mods_spec = (pl.BlockSpec((1, 1, N_MOD * d), lambda b, i, pt: (b, 0, 0)) if r == 1
                 else pl.BlockSpec((1, tm, N_MOD * d), lambda b, i, pt: (b, i, 0)))
    resident = lambda w: pl.BlockSpec(w.shape, lambda b, i, pt: (0, 0), pipeline_mode=pl.Buffered(1))
    vec = pl.BlockSpec((1, d), lambda b, i, pt: (0, 0))
    page_specs, out_specs, out_shape = [], [row_spec(d)], [jax.ShapeDtypeStruct((nb, s, d), F32)]
    if cache_kt is not None:
        db, n_pages = page_table.shape
        nblk = n_pages // PAGES_PER_BLOCK
        need = -(-db * nblk // (nb * n_tiles))
        group = min(g for g in range(1, nblk + 1) if nblk % g == 0 and g >= need)
        groups_per_seq = nblk // group
        pps = group * PAGES_PER_BLOCK

        def group_of(b, i):
            g = jnp.minimum(b * n_tiles + i, db * groups_per_seq - 1)
            return g // groups_per_seq, g % groups_per_seq

        def page_spec(n):
            def index(b, i, pt):
                seq, part = group_of(b, i)
                return (0, pt[seq, part * pps + n], 0, 0, 0)
            return pl.BlockSpec((1, 1) + cache_kt.shape[2:], index)

        page_specs = [page_spec(n) for n in range(pps)]
        out_specs.append(pl.BlockSpec((1, group, N_HEADS, HEAD_DIM), lambda b, i, pt: group_of(b, i) + (0, 0)))
        out_shape.append(jax.ShapeDtypeStruct((db, nblk, N_HEADS, HEAD_DIM), F32))
    grid_spec = pltpu.PrefetchScalarGridSpec(
        num_scalar_prefetch=1,
        grid=(nb, n_tiles),
        in_specs=[row_spec(d), row_spec(ATTN_WIDTH), row_spec(POOL_WIDTH), mods_spec, vec, vec,
                  resident(w_out_b), resident(w_up_b), resident(w_down_b)] + page_specs,
        out_specs=out_specs,
    )
    out = pl.pallas_call(
        functools.partial(_out_kernel, n_pages=len(page_specs)),
        grid_spec=grid_spec,
        out_shape=out_shape,
        compiler_params=pltpu.CompilerParams(dimension_semantics=("arbitrary", "arbitrary"),
                                             vmem_limit_bytes=VMEM_LIMIT),
        name="out_mlp",
    )(page_table, x, attn, pool, mods, g_mlp, g_final, w_out_b, w_up_b, w_down_b, *([cache_kt] * len(page_specs)))
    return out if cache_kt is not None else out[0]


def _select_kernel(q_ref, km_ref, sel_ref):
    g = lax.dot_general(q_ref[0].astype(F32), km_ref[0], (((1,), (1,)), ((), ())),
                        precision=lax.Precision.HIGHEST, preferred_element_type=F32)
    rows, nblk = g.shape
    lane = lax.broadcasted_iota(jnp.int32, g.shape, 1)
    out_lane = lax.broadcasted_iota(jnp.int32, (rows, LANES), 1)
    out = jnp.zeros((rows, LANES), jnp.int32)
    for n in range(MOBA_TOPK):
        mx = jnp.max(g, axis=-1, keepdims=True)
        idx = jnp.min(jnp.where(g == mx, lane, nblk), axis=-1, keepdims=True)
        out = jnp.where(out_lane == n, idx, out)
        g = jnp.where(lane == idx, -jnp.inf, g)
    sel_ref[0] = out


def _select(q_bd, kmean2d):
    db, rows, a = q_bd.shape
    nblk = kmean2d.shape[1]
    return pl.pallas_call(
        _select_kernel,
        grid=(db,),
        in_specs=[pl.BlockSpec((1, rows, a), lambda b: (b, 0, 0)),
                  pl.BlockSpec((1, nblk, a), lambda b: (b, 0, 0))],
        out_specs=pl.BlockSpec((1, rows, LANES), lambda b: (b, 0, 0)),
        out_shape=jax.ShapeDtypeStruct((db, rows, LANES), jnp.int32),
        compiler_params=pltpu.CompilerParams(dimension_semantics=("arbitrary",)),
        name="moba_select",
    )(q_bd, kmean2d)


def _attn_sample_kernel(pt_ref, sel_ref, tab_ref, q_ref, kn_ref, vn_ref, badj_ref, bown_ref, ck_ref, cv_ref,
                        o_ref, kbuf, vbuf, sem, *, nblk, t_new):
    b = pl.program_id(0)
    h = pl.program_id(1)
    n_sel = MOBA_TOPK
    step = b * N_HEADS + h
    n_steps = pl.num_programs(0) * N_HEADS
    buf = step % 2

    def block_of(t, n):
        return sel_ref[(step * t_new + t) * n_sel + n]

    def copies(st, bf):
        sb = st // N_HEADS
        sh = st % N_HEADS
        out = []
        for t in range(t_new):
            for n in range(n_sel):
                j = sel_ref[(st * t_new + t) * n_sel + n]
                slot = bf * (t_new * n_sel) + t * n_sel + n
                for pg in range(PAGES_PER_BLOCK):
                    page = pt_ref[sb, j * PAGES_PER_BLOCK + pg]
                    keys = pl.ds(pg * PAGE_SIZE, PAGE_SIZE)
                    out.append(pltpu.make_async_copy(ck_ref.at[0, page, sh], kbuf.at[slot, :, keys], sem.at[0, bf]))
                    out.append(pltpu.make_async_copy(cv_ref.at[0, page, sh], vbuf.at[slot, :, keys], sem.at[1, bf]))
        return out

    @pl.when(step == 0)
    def _():
        for c in copies(step, buf):
            c.start()

    @pl.when(step + 1 < n_steps)
    def _():
        for c in copies(step + 1, 1 - buf):
            c.start()

    for c in copies(step, buf):
        c.wait()

    slot0 = buf * (t_new * n_sel)
    c_far = tab_ref[h, N_BUCKETS - 1] * LOG2E
    kn = kn_ref[0, 0]
    vn = vn_ref[0, 0]
    out_lane = lax.broadcasted_iota(jnp.int32, (HEAD_DIM, LANES), 1)
    out = jnp.zeros((HEAD_DIM, LANES), F32)
    for t in range(t_new):
        qt = q_ref[0, 0, :, t:t + 1]
        s_own = jnp.sum(kn * qt, axis=0, keepdims=True) + bown_ref[0, t:t + 1, 0:t_new]
        scores = []
        for n in range(n_sel):
            j = block_of(t, n)
            s = jnp.sum(kbuf[slot0 + t * n_sel + n] * qt, axis=0, keepdims=True)
            scores.append(s + jnp.where(j == nblk - 1, badj_ref[0, t:t + 1, :], c_far))
        m = jnp.max(s_own, axis=-1, keepdims=True)
        for s in scores:
            m = jnp.maximum(m, jnp.max(s, axis=-1, keepdims=True))
        p_own = jnp.exp2(s_own - m)
        l = jnp.sum(p_own, axis=-1, keepdims=True)
        acc = jnp.sum(vn * p_own, axis=-1, keepdims=True)
        pv = None
        for n, s in enumerate(scores):
            p = jnp.exp2(s - m)
            l = l + jnp.sum(p, axis=-1, keepdims=True)
            term = vbuf[slot0 + t * n_sel + n] * p
            pv = term if pv is None else pv + term
        acc = acc + jnp.sum(pv, axis=-1, keepdims=True)
        out = jnp.where(out_lane == t, acc / l, out)
    o_ref[0, 0] = out


def _attn_sample(page_table, sel_flat, tab_t, q_t, kn_t, vn_t, badj, bown, cache_kt, cache_vt, nblk):
    db, nh, dh, t_new = q_t.shape
    head_spec = pl.BlockSpec((1, 1, dh, t_new), lambda b, h, pt, sel: (b, h, 0, 0))
    grid_spec = pltpu.PrefetchScalarGridSpec(
        num_scalar_prefetch=2,
        grid=(db, nh),
        in_specs=[pl.BlockSpec(memory_space=pltpu.SMEM), head_spec, head_spec, head_spec,
                  pl.BlockSpec((1,) + badj.shape[1:], lambda b, h, pt, sel: (h, 0, 0)),
                  pl.BlockSpec((1,) + bown.shape[1:], lambda b, h, pt, sel: (h, 0, 0)),
                  pl.BlockSpec(memory_space=pl.ANY), pl.BlockSpec(memory_space=pl.ANY)],
        out_specs=pl.BlockSpec((1, 1, dh, LANES), lambda b, h, pt, sel: (b, h, 0, 0)),
        scratch_shapes=[pltpu.VMEM((2 * t_new * MOBA_TOPK, dh, MOBA_BLOCK), F32),
                        pltpu.VMEM((2 * t_new * MOBA_TOPK, dh, MOBA_BLOCK), F32),
                        pltpu.SemaphoreType.DMA((2, 2))],
    )
    return pl.pallas_call(
        functools.partial(_attn_sample_kernel, nblk=nblk, t_new=t_new),
        grid_spec=grid_spec,
        out_shape=jax.ShapeDtypeStruct((db, nh, dh, LANES), F32),
        compiler_params=pltpu.CompilerParams(dimension_semantics=("arbitrary", "arbitrary"),
                                             vmem_limit_bytes=VMEM_LIMIT),
        name="moba_sample",
    )(page_table, sel_flat, tab_t, q_t, kn_t, vn_t, badj, bown, cache_kt, cache_vt)


def kernel(x_prompt, x_sample, cache_k, cache_v, state_pool, page_table, c_prompt, c_sample, w_ada, b_ada, norm_mix, w_in, rel_bias, w_pool, pool_scale, w_out, norm_mlp, w_up, w_down, norm_final):
    assert w_ada.shape[0] == 1, "single-layer decoder"
    b, s, d = x_prompt.shape
    db, t_new, _ = x_sample.shape
    n_pages = page_table.shape[1]
    past = n_pages * PAGE_SIZE
    assert past % MOBA_BLOCK == 0 and past // MOBA_BLOCK >= MOBA_TOPK
    nblk = past // MOBA_BLOCK

    w_in_b = w_in[0].astype(BF16)
    w_out_b = w_out[0].astype(BF16)
    w_up_b = w_up[0].astype(BF16)
    w_down_b = w_down[0].astype(BF16)
    w_pool_b = w_pool[0].astype(BF16)
    g_final = norm_final.reshape(1, d)
    tab_t = rel_bias.T

    mods = _ada(jnp.concatenate([c_prompt, c_sample], axis=0), w_ada, b_ada[0:1])
    mods_p = mods[:b].reshape(b, 1, N_MOD * d)
    mods_s = jnp.repeat(mods[b:], t_new, axis=0).reshape(1, db * t_new, N_MOD * d)
    bias_own, bias_adj = _bias_tiles(tab_t)

    cache_kt = cache_k.transpose(0, 1, 3, 4, 2)
    cache_vt = cache_v.transpose(0, 1, 3, 4, 2)

    q_p, k_p, v_p, u_p = _inproj(x_prompt, mods_p, norm_mix, w_in_b)
    attn_p = _attn_prompt(tab_t, q_p, k_p, v_p, bias_own, bias_adj)
    pool_p = _pool(u_p, w_pool_b, pool_scale, first_pos=0)
    y_p, kmean = _out(x_prompt, attn_p, pool_p, mods_p, norm_mlp, g_final, w_out_b, w_up_b, w_down_b,
                      page_table, cache_kt, tm=_row_tile(s, (OUT_TILE_WITH_PAGES, 128, 64, 32, 16, 8)))

    xs = x_sample.reshape(1, db * t_new, d)
    q_s, k_s, v_s, u_s = _inproj(xs, mods_s, norm_mix, w_in_b)
    head_of_lane = jnp.arange(ATTN_WIDTH) // HEAD_DIM
    q_bt = q_s.reshape(db, 1, t_new, ATTN_WIDTH)
    q_bd = jnp.where(head_of_lane[None, None, None, :] == jnp.arange(N_HEADS)[None, :, None, None],
                     q_bt, jnp.zeros_like(q_bt)).reshape(db, N_HEADS * t_new, ATTN_WIDTH)
    sel = _select(q_bd, kmean.reshape(db, nblk, ATTN_WIDTH))[:, :, :MOBA_TOPK]
    per_head = lambda z: z.reshape(db, t_new, N_HEADS, HEAD_DIM).transpose(0, 2, 3, 1).astype(F32)
    assert t_new <= 8
    attn_s = _attn_sample(page_table, sel.reshape(-1), tab_t, per_head(q_s), per_head(k_s), per_head(v_s),
                          bias_adj[:, :, :8].transpose(0, 2, 1), bias_own[:, :LANES, :8].transpose(0, 2, 1),
                          cache_kt, cache_vt, nblk)
    attn_s = attn_s[..., :t_new].transpose(0, 3, 1, 2).reshape(1, db * t_new, ATTN_WIDTH).astype(BF16)

    u_full = jnp.concatenate([state_pool[0], u_s.reshape(db, t_new, POOL_WIDTH)], axis=1)
    ext_rows = POOL_HALO + 8
    u_ext = jnp.pad(u_full, ((0, 0), (1, ext_rows - 1 - u_full.shape[1]), (0, 0)))
    pool_ext = _pool(u_ext.reshape(1, db * ext_rows, POOL_WIDTH), w_pool_b, pool_scale, first_pos=past)
    pool_s = pool_ext.reshape(db, ext_rows, POOL_WIDTH)[:, POOL_HALO:POOL_HALO + t_new]
    pool_s = pool_s.reshape(1, db * t_new, POOL_WIDTH)
    y_s = _out(xs, attn_s, pool_s, mods_s, norm_mlp, g_final, w_out_b, w_up_b, w_down_b, page_table)

    heads = lambda z, n, l: z.reshape(1, n, l, N_HEADS, HEAD_DIM)
    return (y_p, y_s.reshape(db, t_new, d),
            heads(k_p, b, s), heads(v_p, b, s), u_p[:, -POOL_STATE:][None],
            heads(k_s, db, t_new), heads(v_s, db, t_new), u_full[:, -POOL_STATE:][None])
```

```python
import functools
import math

import numpy as np
import jax
import jax.numpy as jnp
from jax import lax
from jax.experimental import pallas as pl
from jax.experimental.pallas import tpu as pltpu

HEAD_DIM = 64
N_HEADS = 8
ATTN_WIDTH = N_HEADS * HEAD_DIM
POOL_WINDOWS = (2, 4, 8, 16)
POOL_GROUP = 128
POOL_WIDTH = POOL_GROUP * len(POOL_WINDOWS)
POOL_STATE = max(POOL_WINDOWS) - 1
POOL_HALO = POOL_STATE + 1
MOBA_BLOCK = 256
MOBA_TOPK = 3
N_BUCKETS = 32
MAX_DISTANCE = 128
PAGE_SIZE = 128
PAGES_PER_BLOCK = MOBA_BLOCK // PAGE_SIZE
N_MOD = 6
EPS = 1e-6
NEG = -1e30
SCALE = HEAD_DIM ** -0.5
LOG2E = math.log2(math.e)
Q_SCALE = SCALE * LOG2E
LANES = 128
BF16_SUBLANES = 16
VT_ROWS = HEAD_DIM + BF16_SUBLANES
FF_CHUNK = 1024
FAR_GROUP = 2
OUT_TILE_WITH_PAGES = 256
VMEM_LIMIT = 56 * 1024 * 1024

BF16 = jnp.bfloat16
F32 = jnp.float32


def _nt_dot(a, b):
    return lax.dot_general(a, b, (((1,), (1,)), ((), ())), preferred_element_type=F32)


def _dot(a, b):
    return jnp.dot(a, b, preferred_element_type=F32)


def _row_tile(n, candidates=(512, 256, 128, 64, 32, 16, 8)):
    for c in candidates:
        if n % c == 0:
            return c
    raise ValueError(f"row count {n} is not a multiple of 8")


def _t5_bucket_np(rel):
    n = np.maximum(rel, 0)
    max_exact = N_BUCKETS // 2
    nf = np.maximum(n, max_exact).astype(np.float32)
    large = max_exact + (np.log(nf / np.float32(max_exact)) / np.float32(math.log(MAX_DISTANCE / max_exact))
                         * np.float32(N_BUCKETS - max_exact)).astype(np.int32)
    large = np.minimum(large, N_BUCKETS - 1)
    return np.where(n < max_exact, n, large).astype(np.int32)


def _rms(x, g):
    return x * lax.rsqrt(jnp.mean(x * x, axis=-1, keepdims=True) + EPS) * g


def _ada_kernel(c_ref, w_ref, b_ref, o_ref):
    c = c_ref[...]
    s = c / (1.0 + jnp.exp(-c))
    o_ref[...] = _dot(s.astype(BF16), w_ref[0].astype(BF16)) + b_ref[...]


def _ada(c_all, w_ada, b_ada):
    n, d = c_all.shape
    width = w_ada.shape[-1]
    tn = 1024
    return pl.pallas_call(
        _ada_kernel,
        grid=(width // tn,),
        in_specs=[pl.BlockSpec((n, d), lambda j: (0, 0)),
                  pl.BlockSpec((1, d, tn), lambda j: (0, 0, j)),
                  pl.BlockSpec((1, tn), lambda j: (0, j))],
        out_specs=pl.BlockSpec((n, tn), lambda j: (0, j)),
        out_shape=jax.ShapeDtypeStruct((n, width), F32),
        compiler_params=pltpu.CompilerParams(dimension_semantics=("arbitrary",), vmem_limit_bytes=VMEM_LIMIT),
        name="ada_mod",
    )(c_all, w_ada, b_ada)


def _inproj_kernel(x_ref, mods_ref, g_ref, w_ref, q_ref, k_ref, v_ref, u_ref):
    d = x_ref.shape[-1]
    x = x_ref[0]
    shift = mods_ref[0, :, 0:d]
    scale = mods_ref[0, :, d:2 * d]
    h = _rms(x, g_ref[...]) * (1.0 + scale) + shift
    r = _dot(h.astype(BF16), w_ref[...])
    a = ATTN_WIDTH
    q_ref[0] = (r[:, 0:a] * Q_SCALE).astype(BF16)
    k_ref[0] = r[:, a:2 * a]
    v_ref[0] = r[:, 2 * a:3 * a]
    u_ref[0] = r[:, 3 * a:]


def _inproj(x, mods, g, w_in_b):
    nb, s, d = x.shape
    r = mods.shape[1]
    ts = _row_tile(s)
    width = w_in_b.shape[1]
    row_spec = lambda w: pl.BlockSpec((1, ts, w), lambda b, i: (b, i, 0))
    mods_spec = (pl.BlockSpec((1, 1, N_MOD * d), lambda b, i: (b, 0, 0)) if r == 1
                 else pl.BlockSpec((1, ts, N_MOD * d), lambda b, i: (b, i, 0)))
    return pl.pallas_call(
        _inproj_kernel,
        grid=(nb, s // ts),
        in_specs=[row_spec(d), mods_spec,
                  pl.BlockSpec((1, d), lambda b, i: (0, 0)),
                  pl.BlockSpec((d, width), lambda b, i: (0, 0))],
        out_specs=[row_spec(ATTN_WIDTH), row_spec(ATTN_WIDTH), row_spec(ATTN_WIDTH), row_spec(POOL_WIDTH)],
        out_shape=[jax.ShapeDtypeStruct((nb, s, ATTN_WIDTH), BF16),
                   jax.ShapeDtypeStruct((nb, s, ATTN_WIDTH), F32),
                   jax.ShapeDtypeStruct((nb, s, ATTN_WIDTH), F32),
                   jax.ShapeDtypeStruct((nb, s, POOL_WIDTH), F32)],
        compiler_params=pltpu.CompilerParams(dimension_semantics=("arbitrary", "arbitrary"),
                                             vmem_limit_bytes=VMEM_LIMIT),
        name="in_proj",
    )(x, mods, g, w_in_b)


def _bias_kernel(tab_ref, idx_own_ref, idx_adj_ref, own_ref, adj_ref):
    h = pl.program_id(0)
    io = idx_own_ref[...]
    ia = idx_adj_ref[...]
    bo = jnp.zeros(io.shape, F32)
    ba = jnp.zeros(ia.shape, F32)
    for b in range(N_BUCKETS):
        t = tab_ref[h, b]
        bo = jnp.where(io == b, t, bo)
        ba = jnp.where(ia == b, t, ba)
    own_ref[0] = jnp.where(io < 0, NEG, bo * LOG2E)
    adj_ref[0] = ba * LOG2E


def _bias_tiles(tab_t):
    blk = MOBA_BLOCK
    r = np.arange(blk)[None, :]
    c = np.arange(blk)[:, None]
    idx_own = np.where(r >= c, _t5_bucket_np(r - c), -1).astype(np.int32)
    idx_adj = _t5_bucket_np(blk + r - c)
    tile = pl.BlockSpec((blk, blk), lambda h: (0, 0))
    out = pl.BlockSpec((1, blk, blk), lambda h: (h, 0, 0))
    return pl.pallas_call(
        _bias_kernel,
        grid=(N_HEADS,),
        in_specs=[pl.BlockSpec(memory_space=pltpu.SMEM), tile, tile],
        out_specs=[out, out],
        out_shape=[jax.ShapeDtypeStruct((N_HEADS, blk, blk), F32)] * 2,
        compiler_params=pltpu.CompilerParams(dimension_semantics=("arbitrary",)),
        name="rel_bias_tiles",
    )(tab_t, jnp.asarray(idx_own), jnp.asarray(idx_adj))


def _attn_prompt_kernel(tab_ref, q_ref, k_ref, v_ref, bown_ref, badj_ref, o_ref,
                        kb_ref, vt_ref, km_ref, qm_ref, madd_ref, m_ref, acc_ref, s_ref, *, nb):
    i = pl.program_id(1)
    blk = MOBA_BLOCK
    nbp = km_ref.shape[0]

    @pl.when(i == 0)
    def _():
        kb_ref[...] = k_ref[0].astype(BF16)
        ones = jnp.ones((VT_ROWS - HEAD_DIM, blk), BF16)
        for j in range(nb):
            vt = v_ref[0, j * blk:(j + 1) * blk, :].T.astype(BF16)
            for h in range(N_HEADS):
                vt_ref[j, h, 0:HEAD_DIM, :] = vt[h * HEAD_DIM:(h + 1) * HEAD_DIM, :]
                vt_ref[j, h, HEAD_DIM:, :] = ones
        means = [jnp.mean(k_ref[0, j * blk:(j + 1) * blk, :], axis=0, keepdims=True) for j in range(nb)]
        means += [jnp.zeros_like(means[0])] * (nbp - nb)
        km_ref[...] = jnp.concatenate(means, axis=0)

    lane = lax.broadcasted_iota(jnp.int32, (blk, LANES), 1)
    blk_row = lax.broadcasted_iota(jnp.int32, (nbp, blk), 0)
    valid = blk_row < i
    jm1 = jnp.maximum(i - 1, 0)
    own0 = pl.multiple_of(i * blk, blk)
    adj0 = pl.multiple_of(jm1 * blk, blk)
    heads_per_vreg = LANES // HEAD_DIM

    def lanes_of(h):
        p = h // heads_per_vreg
        return slice(p * LANES, (p + 1) * LANES)

    for h in range(N_HEADS):
        hh = h % heads_per_vreg
        q2 = q_ref[0, :, lanes_of(h)]
        hmask = (lane >= hh * HEAD_DIM) & (lane < (hh + 1) * HEAD_DIM)
        qm = jnp.where(hmask, q2, jnp.zeros_like(q2))
        qm_ref[h] = qm
        gate = _nt_dot(km_ref[:, lanes_of(h)].astype(BF16), qm)
        for j in range(nb):
            gj = gate[j:j + 1, :]
            beats = ((gate > gj) | ((gate == gj) & (blk_row < j))) & valid
            cnt = jnp.sum(beats.astype(F32), axis=0, keepdims=True)
            cnt = cnt + jnp.where(j < i, 0.0, float(nb))
            madd_ref[h, j:j + 1, :] = jnp.where(cnt < MOBA_TOPK, 0.0, NEG)

    def key_blocks(blocks):
        for n, (key0, _, tile_bias, _, _) in enumerate(blocks):
            for h in range(N_HEADS):
                s = _nt_dot(kb_ref[pl.ds(key0, blk), lanes_of(h)], qm_ref[h])
                s_ref[n, h] = s if tile_bias is None else s + tile_bias(h)
        for n, (_, blk_idx, _, query_bias, first) in enumerate(blocks):
            for h in range(N_HEADS):
                s = s_ref[n, h]
                m_blk = jnp.max(s, axis=0, keepdims=True)
                if query_bias is not None:
                    qb = query_bias(h)
                    m_blk = m_blk + qb
                if first:
                    m_new = m_blk
                else:
                    m_old = m_ref[h]
                    m_new = jnp.maximum(m_old, m_blk)
                ref_row = m_new if query_bias is None else m_new - qb
                p = jnp.exp2(s - ref_row).astype(BF16)
                pv = _dot(vt_ref[blk_idx, h], p)
                acc_ref[h] = pv if first else jnp.exp2(m_old - m_new) * acc_ref[h] + pv
                m_ref[h] = m_new

    def far_block(j):
        return (pl.multiple_of(j * blk, blk), j, None,
                lambda h: madd_ref[h, pl.ds(j, 1), :] + tab_ref[h, N_BUCKETS - 1] * LOG2E, False)

    key_blocks([(own0, i, lambda h: bown_ref[h], None, True),
                (adj0, jm1, lambda h: badj_ref[h], lambda h: madd_ref[h, pl.ds(jm1, 1), :], False)])

    def far_pair(jj, carry):
        key_blocks([far_block(FAR_GROUP * jj + n) for n in range(FAR_GROUP)])
        return carry

    lax.fori_loop(0, jm1 // FAR_GROUP, far_pair, 0)
    for n in range(1, FAR_GROUP):
        @pl.when(jm1 % FAR_GROUP >= n)
        def _():
            key_blocks([far_block(jm1 - n)])

    for p in range(ATTN_WIDTH // LANES):
        halves = []
        for h in range(p * heads_per_vreg, (p + 1) * heads_per_vreg):
            acc = acc_ref[h]
            halves.append(acc[0:HEAD_DIM, :] / acc[HEAD_DIM:HEAD_DIM + 1, :])
        o_ref[0, :, p * LANES:(p + 1) * LANES] = jnp.concatenate(halves, axis=0).T.astype(BF16)


def _attn_prompt(tab_t, q, k, v, bias_own_t, bias_adj_t):
    b, s, a = q.shape
    blk = MOBA_BLOCK
    nb = s // blk
    nbp = 8
    assert s % blk == 0 and nb <= nbp
    full = pl.BlockSpec((1, s, a), lambda bi, i: (bi, 0, 0))
    tile = pl.BlockSpec((1, blk, a), lambda bi, i: (bi, i, 0))
    bias = pl.BlockSpec((N_HEADS, blk, blk), lambda bi, i: (0, 0, 0))
    return pl.pallas_call(
        functools.partial(_attn_prompt_kernel, nb=nb),
        grid=(b, nb),
        in_specs=[pl.BlockSpec(memory_space=pltpu.SMEM), tile, full, full, bias, bias],
        out_specs=tile,
        out_shape=jax.ShapeDtypeStruct((b, s, a), BF16),
        scratch_shapes=[pltpu.VMEM((s, a), BF16), pltpu.VMEM((nb, N_HEADS, VT_ROWS, blk), BF16),
                        pltpu.VMEM((nbp, a), F32), pltpu.VMEM((N_HEADS, blk, LANES), BF16),
                        pltpu.VMEM((N_HEADS, nbp, blk), F32), pltpu.VMEM((N_HEADS, 1, blk), F32),
                        pltpu.VMEM((N_HEADS, VT_ROWS, blk), F32),
                        pltpu.VMEM((FAR_GROUP, N_HEADS, blk, blk), F32)],
        compiler_params=pltpu.CompilerParams(dimension_semantics=("arbitrary", "arbitrary"),
                                             vmem_limit_bytes=VMEM_LIMIT),
        name="moba_prompt",
    )(tab_t, q, k, v, bias_own_t, bias_adj_t)


def _pool_kernel(u_ref, hist_ref, w_ref, scale_ref, o_ref, *, first_pos):
    i = pl.program_id(1)
    tp = u_ref.shape[1]
    u = u_ref[0]
    hist = jnp.where(i > 0, hist_ref[0], 0.0)
    ext = jnp.concatenate([hist, u], axis=0)
    pos = first_pos + i * tp + lax.broadcasted_iota(jnp.int32, (tp, 1), 0)
    for g, w in enumerate(POOL_WINDOWS):
        gs = slice(g * POOL_GROUP, (g + 1) * POOL_GROUP)
        s = ext[:, gs]
        shift = 1
        while shift < w:
            s = s + pltpu.roll(s, shift, 0)
            shift *= 2
        cnt = jnp.minimum(pos + 1, w).astype(F32)
        d = s[POOL_HALO:, :] / cnt - u[:, gs]
        y = _dot(d.astype(BF16), w_ref[g]) * scale_ref[:, gs]
        o_ref[0, :, gs] = y.astype(BF16)


def _pool(u, w_pool_b, pool_scale, first_pos):
    nb, s, c = u.shape
    tp = _row_tile(s, (512, 256, 128, 64, 32, 16))
    hb = tp // POOL_HALO
    return pl.pallas_call(
        functools.partial(_pool_kernel, first_pos=first_pos),
        grid=(nb, s // tp),
        in_specs=[pl.BlockSpec((1, tp, c), lambda b, i: (b, i, 0)),
                  pl.BlockSpec((1, POOL_HALO, c), lambda b, i: (b, jnp.maximum(i * hb - 1, 0), 0)),
                  pl.BlockSpec(w_pool_b.shape, lambda b, i: (0, 0, 0)),
                  pl.BlockSpec((1, c), lambda b, i: (0, 0))],
        out_specs=pl.BlockSpec((1, tp, c), lambda b, i: (b, i, 0)),
        out_shape=jax.ShapeDtypeStruct((nb, s, c), BF16),
        compiler_params=pltpu.CompilerParams(dimension_semantics=("arbitrary", "arbitrary"),
                                             vmem_limit_bytes=VMEM_LIMIT),
        name="multi_pool",
    )(u, u, w_pool_b, pool_scale)


def _page_block_mean(page_refs, blk):
    s = page_refs[blk * PAGES_PER_BLOCK][0, 0]
    for pg in range(1, PAGES_PER_BLOCK):
        s = s + page_refs[blk * PAGES_PER_BLOCK + pg][0, 0]
    return jnp.sum(s, axis=-1) * (1.0 / MOBA_BLOCK)


def _out_kernel(pt_ref, x_ref, attn_ref, pool_ref, mods_ref, gm_ref, gf_ref, wo_ref, wu_ref, wd_ref, *refs,
                n_pages):
    page_refs, y_ref = refs[:n_pages], refs[n_pages]
    n_blocks = n_pages // PAGES_PER_BLOCK
    d = x_ref.shape[-1]
    a = ATTN_WIDTH
    x = x_ref[0]
    g1 = mods_ref[0, :, 2 * d:3 * d]
    sh2 = mods_ref[0, :, 3 * d:4 * d]
    sc2 = mods_ref[0, :, 4 * d:5 * d]
    g2 = mods_ref[0, :, 5 * d:6 * d]
    mix = _dot(attn_ref[0], wo_ref[0:a, :]) + _dot(pool_ref[0], wo_ref[a:, :])
    x1 = x + g1 * mix
    hb = (_rms(x1, gm_ref[...]) * (1.0 + sc2) + sh2).astype(BF16)
    acc = jnp.zeros(x.shape, F32)
    n_chunks = wu_ref.shape[1] // FF_CHUNK
    for c in range(n_chunks):
        cs = slice(c * FF_CHUNK, (c + 1) * FF_CHUNK)
        t = jnp.maximum(_dot(hb, wu_ref[:, cs]), 0.0)
        acc = acc + _dot((t * t).astype(BF16), wd_ref[cs, :])
        for blk in range(c * n_blocks // n_chunks, (c + 1) * n_blocks // n_chunks):
            refs[n_pages + 1][0, blk] = _page_block_mean(page_refs, blk)
    x2 = x1 + g2 * acc
    y_ref[0] = _rms(x2, gf_ref[...])


def _out(x, attn, pool, mods, g_mlp, g_final, w_out_b, w_up_b, w_down_b, page_table, cache_kt=None, tm=None):
    nb, s, d = x.shape
    r = mods.shape[1]
    tm = tm or _row_tile(s)
    n_tiles = s // tm
    row_spec = lambda w: pl.BlockSpec((1, tm, w), lambda b, i, pt: (b, i, 0))
    mods_spec = (pl.BlockSpec((1, 1, N_MOD * d), lambda b, i, pt: (b, 0, 0)) if r == 1
                 else pl.BlockSpec((1, tm, N_MOD * d), lambda b, i, pt: (b, i, 0)))
    resident = lambda w: pl.BlockSpec(w.shape, lambda b, i, pt: (0, 0), pipeline_mode=pl.Buffered(1))
    vec = pl.BlockSpec((1, d), lambda b, i, pt: (0, 0))
    page_specs, out_specs, out_shape = [], [row_spec(d)], [jax.ShapeDtypeStruct((nb, s, d), F32)]
    if cache_kt is not None:
        db, n_pages = page_table.shape
        nblk = n_pages // PAGES_PER_BLOCK
        need = -(-db * nblk // (nb * n_tiles))
        group = min(g for g in range(1, nblk + 1) if nblk % g == 0 and g >= need)
        groups_per_seq = nblk // group
        pps = group * PAGES_PER_BLOCK

        def group_of(b, i):
            g = jnp.minimum(b * n_tiles + i, db * groups_per_seq - 1)
            return g // groups_per_seq, g % groups_per_seq

        def page_spec(n):
            def index(b, i, pt):
                seq, part = group_of(b, i)
                return (0, pt[seq, part * pps + n], 0, 0, 0)
            return pl.BlockSpec((1, 1) + cache_kt.shape[2:], index)

        page_specs = [page_spec(n) for n in range(pps)]
        out_specs.append(pl.BlockSpec((1, group, N_HEADS, HEAD_DIM), lambda b, i, pt: group_of(b, i) + (0, 0)))
        out_shape.append(jax.ShapeDtypeStruct((db, nblk, N_HEADS, HEAD_DIM), F32))
    grid_spec = pltpu.PrefetchScalarGridSpec(
        num_scalar_prefetch=1,
        grid=(nb, n_tiles),
        in_specs=[row_spec(d), row_spec(ATTN_WIDTH), row_spec(POOL_WIDTH), mods_spec, vec, vec,
                  resident(w_out_b), resident(w_up_b), resident(w_down_b)] + page_specs,
        out_specs=out_specs,
    )
    out = pl.pallas_call(
        functools.partial(_out_kernel, n_pages=len(page_specs)),
        grid_spec=grid_spec,
        out_shape=out_shape,
        compiler_params=pltpu.CompilerParams(dimension_semantics=("arbitrary", "arbitrary"),
                                             vmem_limit_bytes=VMEM_LIMIT),
        name="out_mlp",
    )(page_table, x, attn, pool, mods, g_mlp, g_final, w_out_b, w_up_b, w_down_b, *([cache_kt] * len(page_specs)))
    return out if cache_kt is not None else out[0]


def _select_kernel(q_ref, km_ref, sel_ref):
    g = lax.dot_general(q_ref[0].astype(F32), km_ref[0], (((1,), (1,)), ((), ())),
                        precision=lax.Precision.HIGHEST, preferred_element_type=F32)
    rows, nblk = g.shape
    lane = lax.broadcasted_iota(jnp.int32, g.shape, 1)
    out_lane = lax.broadcasted_iota(jnp.int32, (rows, LANES), 1)
    out = jnp.zeros((rows, LANES), jnp.int32)
    for n in range(MOBA_TOPK):
        mx = jnp.max(g, axis=-1, keepdims=True)
        idx = jnp.min(jnp.where(g == mx, lane, nblk), axis=-1, keepdims=True)
        out = jnp.where(out_lane == n, idx, out)
        g = jnp.where(lane == idx, -jnp.inf, g)
    sel_ref[0] = out


def _select(q_bd, kmean2d):
    db, rows, a = q_bd.shape
    nblk = kmean2d.shape[1]
    return pl.pallas_call(
        _select_kernel,
        grid=(db,),
        in_specs=[pl.BlockSpec((1, rows, a), lambda b: (b, 0, 0)),
                  pl.BlockSpec((1, nblk, a), lambda b: (b, 0, 0))],
        out_specs=pl.BlockSpec((1, rows, LANES), lambda b: (b, 0, 0)),
        out_shape=jax.ShapeDtypeStruct((db, rows, LANES), jnp.int32),
        compiler_params=pltpu.CompilerParams(dimension_semantics=("arbitrary",)),
        name="moba_select",
    )(q_bd, kmean2d)


def _attn_sample_kernel(pt_ref, sel_ref, tab_ref, q_ref, kn_ref, vn_ref, badj_ref, bown_ref, ck_ref, cv_ref,
                        o_ref, kbuf, vbuf, sem, *, nblk, t_new):
    b = pl.program_id(0)
    h = pl.program_id(1)
    n_sel = MOBA_TOPK
    step = b * N_HEADS + h
    n_steps = pl.num_programs(0) * N_HEADS
    buf = step % 2

    def block_of(t, n):
        return sel_ref[(step * t_new + t) * n_sel + n]

    def copies(st, bf):
        sb = st // N_HEADS
        sh = st % N_HEADS
        out = []
        for t in range(t_new):
            for n in range(n_sel):
                j = sel_ref[(st * t_new + t) * n_sel + n]
                slot = bf * (t_new * n_sel) + t * n_sel + n
                for pg in range(PAGES_PER_BLOCK):
                    page = pt_ref[sb, j * PAGES_PER_BLOCK + pg]
                    keys = pl.ds(pg * PAGE_SIZE, PAGE_SIZE)
                    out.append(pltpu.make_async_copy(ck_ref.at[0, page, sh], kbuf.at[slot, :, keys], sem.at[0, bf]))
                    out.append(pltpu.make_async_copy(cv_ref.at[0, page, sh], vbuf.at[slot, :, keys], sem.at[1, bf]))
        return out

    @pl.when(step == 0)
    def _():
        for c in copies(step, buf):
            c.start()

    @pl.when(step + 1 < n_steps)
    def _():
        for c in copies(step + 1, 1 - buf):
            c.start()

    for c in copies(step, buf):
        c.wait()

    slot0 = buf * (t_new * n_sel)
    c_far = tab_ref[h, N_BUCKETS - 1] * LOG2E
    kn = kn_ref[0, 0]
    vn = vn_ref[0, 0]
    out_lane = lax.broadcasted_iota(jnp.int32, (HEAD_DIM, LANES), 1)
    out = jnp.zeros((HEAD_DIM, LANES), F32)
    for t in range(t_new):
        qt = q_ref[0, 0, :, t:t + 1]
        s_own = jnp.sum(kn * qt, axis=0, keepdims=True) + bown_ref[0, t:t + 1, 0:t_new]
        scores = []
        for n in range(n_sel):
            j = block_of(t, n)
            s = jnp.sum(kbuf[slot0 + t * n_sel + n] * qt, axis=0, keepdims=True)
            scores.append(s + jnp.where(j == nblk - 1, badj_ref[0, t:t + 1, :], c_far))
        m = jnp.max(s_own, axis=-1, keepdims=True)
        for s in scores:
            m = jnp.maximum(m, jnp.max(s, axis=-1, keepdims=True))
        p_own = jnp.exp2(s_own - m)
        l = jnp.sum(p_own, axis=-1, keepdims=True)
        acc = jnp.sum(vn * p_own, axis=-1, keepdims=True)
        pv = None
        for n, s in enumerate(scores):
            p = jnp.exp2(s - m)
            l = l + jnp.sum(p, axis=-1, keepdims=True)
            term = vbuf[slot0 + t * n_sel + n] * p
            pv = term if pv is None else pv + term
        acc = acc + jnp.sum(pv, axis=-1, keepdims=True)
        out = jnp.where(out_lane == t, acc / l, out)
    o_ref[0, 0] = out


def _attn_sample(page_table, sel_flat, tab_t, q_t, kn_t, vn_t, badj, bown, cache_kt, cache_vt, nblk):
    db, nh, dh, t_new = q_t.shape
    head_spec = pl.BlockSpec((1, 1, dh, t_new), lambda b, h, pt, sel: (b, h, 0, 0))
    grid_spec = pltpu.PrefetchScalarGridSpec(
        num_scalar_prefetch=2,
        grid=(db, nh),
        in_specs=[pl.BlockSpec(memory_space=pltpu.SMEM), head_spec, head_spec, head_spec,
                  pl.BlockSpec((1,) + badj.shape[1:], lambda b, h, pt, sel: (h, 0, 0)),
                  pl.BlockSpec((1,) + bown.shape[1:], lambda b, h, pt, sel: (h, 0, 0)),
                  pl.BlockSpec(memory_space=pl.ANY), pl.BlockSpec(memory_space=pl.ANY)],
        out_specs=pl.BlockSpec((1, 1, dh, LANES), lambda b, h, pt, sel: (b, h, 0, 0)),
        scratch_shapes=[pltpu.VMEM((2 * t_new * MOBA_TOPK, dh, MOBA_BLOCK), F32),
                        pltpu.VMEM((2 * t_new * MOBA_TOPK, dh, MOBA_BLOCK), F32),
                        pltpu.SemaphoreType.DMA((2, 2))],
    )
    return pl.pallas_call(
        functools.partial(_attn_sample_kernel, nblk=nblk, t_new=t_new),
        grid_spec=grid_spec,
        out_shape=jax.ShapeDtypeStruct((db, nh, dh, LANES), F32),
        compiler_params=pltpu.CompilerParams(dimension_semantics=("arbitrary", "arbitrary"),
                                             vmem_limit_bytes=VMEM_LIMIT),
        name="moba_sample",
    )(page_table, sel_flat, tab_t, q_t, kn_t, vn_t, badj, bown, cache_kt, cache_vt)


def kernel(x_prompt, x_sample, cache_k, cache_v, state_pool, page_table, c_prompt, c_sample, w_ada, b_ada, norm_mix, w_in, rel_bias, w_pool, pool_scale, w_out, norm_mlp, w_up, w_down, norm_final):
    assert w_ada.shape[0] == 1, "single-layer decoder"
    b, s, d = x_prompt.shape
    db, t_new, _ = x_sample.shape
    n_pages = page_table.shape[1]
    past = n_pages * PAGE_SIZE
    assert past % MOBA_BLOCK == 0 and past // MOBA_BLOCK >= MOBA_TOPK
    nblk = past // MOBA_BLOCK

    w_in_b = w_in[0].astype(BF16)
    w_out_b = w_out[0].astype(BF16)
    w_up_b = w_up[0].astype(BF16)
    w_down_b = w_down[0].astype(BF16)
    w_pool_b = w_pool[0].astype(BF16)
    g_final = norm_final.reshape(1, d)
    tab_t = rel_bias.T

    mods = _ada(jnp.concatenate([c_prompt, c_sample], axis=0), w_ada, b_ada[0:1])
    mods_p = mods[:b].reshape(b, 1, N_MOD * d)
    mods_s = jnp.repeat(mods[b:], t_new, axis=0).reshape(1, db * t_new, N_MOD * d)
    bias_own, bias_adj = _bias_tiles(tab_t)

    cache_kt = cache_k.transpose(0, 1, 3, 4, 2)
    cache_vt = cache_v.transpose(0, 1, 3, 4, 2)

    q_p, k_p, v_p, u_p = _inproj(x_prompt, mods_p, norm_mix, w_in_b)
    attn_p = _attn_prompt(tab_t, q_p, k_p, v_p, bias_own, bias_adj)
    pool_p = _pool(u_p, w_pool_b, pool_scale, first_pos=0)
    y_p, kmean = _out(x_prompt, attn_p, pool_p, mods_p, norm_mlp, g_final, w_out_b, w_up_b, w_down_b,
                      page_table, cache_kt, tm=_row_tile(s, (OUT_TILE_WITH_PAGES, 128, 64, 32, 16, 8)))

    xs = x_sample.reshape(1, db * t_new, d)
    q_s, k_s, v_s, u_s = _inproj(xs, mods_s, norm_mix, w_in_b)
    head_of_lane = jnp.arange(ATTN_WIDTH) // HEAD_DIM
    q_bt = q_s.reshape(db, 1, t_new, ATTN_WIDTH)
    q_bd = jnp.where(head_of_lane[None, None, None, :] == jnp.arange(N_HEADS)[None, :, None, None],
                     q_bt, jnp.zeros_like(q_bt)).reshape(db, N_HEADS * t_new, ATTN_WIDTH)
    sel = _select(q_bd, kmean.reshape(db, nblk, ATTN_WIDTH))[:, :, :MOBA_TOPK]
    per_head = lambda z: z.reshape(db, t_new, N_HEADS, HEAD_DIM).transpose(0, 2, 3, 1).astype(F32)
    assert t_new <= 8
    attn_s = _attn_sample(page_table, sel.reshape(-1), tab_t, per_head(q_s), per_head(k_s), per_head(v_s),
                          bias_adj[:, :, :8].transpose(0, 2, 1), bias_own[:, :LANES, :8].transpose(0, 2, 1),
                          cache_kt, cache_vt, nblk)
    attn_s = attn_s[..., :t_new].transpose(0, 3, 1, 2).reshape(1, db * t_new, ATTN_WIDTH).astype(BF16)

    u_full = jnp.concatenate([state_pool[0], u_s.reshape(db, t_new, POOL_WIDTH)], axis=1)
    ext_rows = POOL_HALO + 8
    u_ext = jnp.pad(u_full, ((0, 0), (1, ext_rows - 1 - u_full.shape[1]), (0, 0)))
    pool_ext = _pool(u_ext.reshape(1, db * ext_rows, POOL_WIDTH), w_pool_b, pool_scale, first_pos=past)
    pool_s = pool_ext.reshape(db, ext_rows, POOL_WIDTH)[:, POOL_HALO:POOL_HALO + t_new]
    pool_s = pool_s.reshape(1, db * t_new, POOL_WIDTH)
    y_s = _out(xs, attn_s, pool_s, mods_s, norm_mlp, g_final, w_out_b, w_up_b, w_down_b, page_table)

    heads = lambda z, n, l: z.reshape(1, n, l, N_HEADS, HEAD_DIM)
    return (y_p, y_s.reshape(db, t_new, d),
            heads(k_p, b, s), heads(v_p, b, s), u_p[:, -POOL_STATE:][None],
            heads(k_s, db, t_new), heads(v_s, db, t_new), u_full[:, -POOL_STATE:][None])
```

```python
import functools
import math

import numpy as np
import jax
import jax.numpy as jnp
from jax import lax
from jax.experimental import pallas as pl
from jax.experimental.pallas import tpu as pltpu

HEAD_DIM = 64
N_HEADS = 8
ATTN_WIDTH = N_HEADS * HEAD_DIM
POOL_WINDOWS = (2, 4, 8, 16)
POOL_GROUP = 128
POOL_WIDTH = POOL_GROUP * len(POOL_WINDOWS)
POOL_STATE = max(POOL_WINDOWS) - 1
POOL_HALO = POOL_STATE + 1
MOBA_BLOCK = 256
MOBA_TOPK = 3
N_BUCKETS = 32
MAX_DISTANCE = 128
PAGE_SIZE = 128
PAGES_PER_BLOCK = MOBA_BLOCK // PAGE_SIZE
N_MOD = 6
EPS = 1e-6
NEG = -1e30
SCALE = HEAD_DIM ** -0.5
LOG2E = math.log2(math.e)
Q_SCALE = SCALE * LOG2E
LANES = 128
BF16_SUBLANES = 16
VT_ROWS = HEAD_DIM + BF16_SUBLANES
FF_CHUNK = 1024
SELECT_SEQS_PER_STEP = 8
FAR_GROUP = 2
OUT_TILE_WITH_PAGES = 256
VMEM_LIMIT = 56 * 1024 * 1024

BF16 = jnp.bfloat16
F32 = jnp.float32


def _nt_dot(a, b):
    return lax.dot_general(a, b, (((1,), (1,)), ((), ())), preferred_element_type=F32)


def _dot(a, b):
    return jnp.dot(a, b, preferred_element_type=F32)


def _row_tile(n, candidates=(512, 256, 128, 64, 32, 16, 8)):
    for c in candidates:
        if n % c == 0:
            return c
    raise ValueError(f"row count {n} is not a multiple of 8")


def _t5_bucket_np(rel):
    n = np.maximum(rel, 0)
    max_exact = N_BUCKETS // 2
    nf = np.maximum(n, max_exact).astype(np.float32)
    large = max_exact + (np.log(nf / np.float32(max_exact)) / np.float32(math.log(MAX_DISTANCE / max_exact))
                         * np.float32(N_BUCKETS - max_exact)).astype(np.int32)
    large = np.minimum(large, N_BUCKETS - 1)
    return np.where(n < max_exact, n, large).astype(np.int32)


def _rms(x, g):
    return x * lax.rsqrt(jnp.mean(x * x, axis=-1, keepdims=True) + EPS) * g


def _ada_kernel(c_ref, w_ref, b_ref, o_ref):
    c = c_ref[...]
    s = c / (1.0 + jnp.exp(-c))
    o_ref[...] = _dot(s.astype(BF16), w_ref[0].astype(BF16)) + b_ref[...]


def _ada(c_all, w_ada, b_ada):
    n, d = c_all.shape
    width = w_ada.shape[-1]
    tn = 1024
    return pl.pallas_call(
        _ada_kernel,
        grid=(width // tn,),
        in_specs=[pl.BlockSpec((n, d), lambda j: (0, 0)),
                  pl.BlockSpec((1, d, tn), lambda j: (0, 0, j)),
                  pl.BlockSpec((1, tn), lambda j: (0, j))],
        out_specs=pl.BlockSpec((n, tn), lambda j: (0, j)),
        out_shape=jax.ShapeDtypeStruct((n, width), F32),
        compiler_params=pltpu.CompilerParams(dimension_semantics=("arbitrary",), vmem_limit_bytes=VMEM_LIMIT),
        name="ada_mod",
    )(c_all, w_ada, b_ada)


def _inproj_kernel(x_ref, mods_ref, g_ref, w_ref, q_ref, k_ref, v_ref, u_ref):
    d = x_ref.shape[-1]
    x = x_ref[0]
    shift = mods_ref[0, :, 0:d]
    scale = mods_ref[0, :, d:2 * d]
    h = _rms(x, g_ref[...]) * (1.0 + scale) + shift
    r = _dot(h.astype(BF16), w_ref[...])
    a = ATTN_WIDTH
    q_ref[0] = (r[:, 0:a] * Q_SCALE).astype(BF16)
    k_ref[0] = r[:, a:2 * a]
    v_ref[0] = r[:, 2 * a:3 * a]
    u_ref[0] = r[:, 3 * a:]


def _inproj(x, mods, g, w_in_b):
    nb, s, d = x.shape
    r = mods.shape[1]
    ts = _row_tile(s)
    width = w_in_b.shape[1]
    row_spec = lambda w: pl.BlockSpec((1, ts, w), lambda b, i: (b, i, 0))
    mods_spec = (pl.BlockSpec((1, 1, N_MOD * d), lambda b, i: (b, 0, 0)) if r == 1
                 else pl.BlockSpec((1, ts, N_MOD * d), lambda b, i: (b, i, 0)))
    return pl.pallas_call(
        _inproj_kernel,
        grid=(nb, s // ts),
        in_specs=[row_spec(d), mods_spec,
                  pl.BlockSpec((1, d), lambda b, i: (0, 0)),
                  pl.BlockSpec((d, width), lambda b, i: (0, 0))],
        out_specs=[row_spec(ATTN_WIDTH), row_spec(ATTN_WIDTH), row_spec(ATTN_WIDTH), row_spec(POOL_WIDTH)],
        out_shape=[jax.ShapeDtypeStruct((nb, s, ATTN_WIDTH), BF16),
                   jax.ShapeDtypeStruct((nb, s, ATTN_WIDTH), F32),
                   jax.ShapeDtypeStruct((nb, s, ATTN_WIDTH), F32),
                   jax.ShapeDtypeStruct((nb, s, POOL_WIDTH), F32)],
        compiler_params=pltpu.CompilerParams(dimension_semantics=("arbitrary", "arbitrary"),
                                             vmem_limit_bytes=VMEM_LIMIT),
        name="in_proj",
    )(x, mods, g, w_in_b)


def _bias_kernel(tab_ref, idx_own_ref, idx_adj_ref, own_ref, adj_ref):
    h = pl.program_id(0)
    io = idx_own_ref[...]
    ia = idx_adj_ref[...]
    bo = jnp.zeros(io.shape, F32)
    ba = jnp.zeros(ia.shape, F32)
    for b in range(N_BUCKETS):
        t = tab_ref[h, b]
        bo = jnp.where(io == b, t, bo)
        ba = jnp.where(ia == b, t, ba)
    own_ref[0] = jnp.where(io < 0, NEG, bo * LOG2E)
    adj_ref[0] = ba * LOG2E


def _bias_tiles(tab_t):
    blk = MOBA_BLOCK
    r = np.arange(blk)[None, :]
    c = np.arange(blk)[:, None]
    idx_own = np.where(r >= c, _t5_bucket_np(r - c), -1).astype(np.int32)
    idx_adj = _t5_bucket_np(blk + r - c)
    tile = pl.BlockSpec((blk, blk), lambda h: (0, 0))
    out = pl.BlockSpec((1, blk, blk), lambda h: (h, 0, 0))
    return pl.pallas_call(
        _bias_kernel,
        grid=(N_HEADS,),
        in_specs=[pl.BlockSpec(memory_space=pltpu.SMEM), tile, tile],
        out_specs=[out, out],
        out_shape=[jax.ShapeDtypeStruct((N_HEADS, blk, blk), F32)] * 2,
        compiler_params=pltpu.CompilerParams(dimension_semantics=("arbitrary",)),
        name="rel_bias_tiles",
    )(tab_t, jnp.asarray(idx_own), jnp.asarray(idx_adj))


def _attn_prompt_kernel(tab_ref, q_ref, k_ref, v_ref, bown_ref, badj_ref, o_ref,
                        kb_ref, vt_ref, km_ref, qm_ref, madd_ref, m_ref, acc_ref, s_ref, *, nb):
    i = pl.program_id(1)
    blk = MOBA_BLOCK
    nbp = km_ref.shape[0]

    @pl.when(i == 0)
    def _():
        kb_ref[...] = k_ref[0].astype(BF16)
        ones = jnp.ones((VT_ROWS - HEAD_DIM, blk), BF16)
        for j in range(nb):
            vt = v_ref[0, j * blk:(j + 1) * blk, :].T.astype(BF16)
            for h in range(N_HEADS):
                vt_ref[j, h, 0:HEAD_DIM, :] = vt[h * HEAD_DIM:(h + 1) * HEAD_DIM, :]
                vt_ref[j, h, HEAD_DIM:, :] = ones
        means = [jnp.mean(k_ref[0, j * blk:(j + 1) * blk, :], axis=0, keepdims=True) for j in range(nb)]
        means += [jnp.zeros_like(means[0])] * (nbp - nb)
        km_ref[...] = jnp.concatenate(means, axis=0)

    lane = lax.broadcasted_iota(jnp.int32, (blk, LANES), 1)
    blk_row = lax.broadcasted_iota(jnp.int32, (nbp, blk), 0)
    valid = blk_row < i
    jm1 = jnp.maximum(i - 1, 0)
    own0 = pl.multiple_of(i * blk, blk)
    adj0 = pl.multiple_of(jm1 * blk, blk)
    heads_per_vreg = LANES // HEAD_DIM

    def lanes_of(h):
        p = h // heads_per_vreg
        return slice(p * LANES, (p + 1) * LANES)

    for h in range(N_HEADS):
        hh = h % heads_per_vreg
        q2 = q_ref[0, :, lanes_of(h)]
        hmask = (lane >= hh * HEAD_DIM) & (lane < (hh + 1) * HEAD_DIM)
        qm = jnp.where(hmask, q2, jnp.zeros_like(q2))
        qm_ref[h] = qm
        gate = _nt_dot(km_ref[:, lanes_of(h)].astype(BF16), qm)
        for j in range(nb):
            gj = gate[j:j + 1, :]
            beats = ((gate > gj) | ((gate == gj) & (blk_row < j))) & valid
            cnt = jnp.sum(beats.astype(F32), axis=0, keepdims=True)
            cnt = cnt + jnp.where(j < i, 0.0, float(nb))
            madd_ref[h, j:j + 1, :] = jnp.where(cnt < MOBA_TOPK, 0.0, NEG)

    def key_blocks(blocks):
        for n, (key0, _, tile_bias, _, _) in enumerate(blocks):
            for h in range(N_HEADS):
                s = _nt_dot(kb_ref[pl.ds(key0, blk), lanes_of(h)], qm_ref[h])
                s_ref[n, h] = s if tile_bias is None else s + tile_bias(h)
        for n, (_, blk_idx, _, query_bias, first) in enumerate(blocks):
            for h in range(N_HEADS):
                s = s_ref[n, h]
                m_blk = jnp.max(s, axis=0, keepdims=True)
                if query_bias is not None:
                    qb = query_bias(h)
                    m_blk = m_blk + qb
                if first:
                    m_new = m_blk
                else:
                    m_old = m_ref[h]
                    m_new = jnp.maximum(m_old, m_blk)
                ref_row = m_new if query_bias is None else m_new - qb
                p = jnp.exp2(s - ref_row).astype(BF16)
                pv = _dot(vt_ref[blk_idx, h], p)
                acc_ref[h] = pv if first else jnp.exp2(m_old - m_new) * acc_ref[h] + pv
                m_ref[h] = m_new

    def far_block(j):
        return (pl.multiple_of(j * blk, blk), j, None,
                lambda h: madd_ref[h, pl.ds(j, 1), :] + tab_ref[h, N_BUCKETS - 1] * LOG2E, False)

    key_blocks([(own0, i, lambda h: bown_ref[h], None, True),
                (adj0, jm1, lambda h: badj_ref[h], lambda h: madd_ref[h, pl.ds(jm1, 1), :], False)])

    def far_pair(jj, carry):
        key_blocks([far_block(FAR_GROUP * jj + n) for n in range(FAR_GROUP)])
        return carry

    lax.fori_loop(0, jm1 // FAR_GROUP, far_pair, 0)
    for n in range(1, FAR_GROUP):
        @pl.when(jm1 % FAR_GROUP >= n)
        def _():
            key_blocks([far_block(jm1 - n)])

    for p in range(ATTN_WIDTH // LANES):
        halves = []
        for h in range(p * heads_per_vreg, (p + 1) * heads_per_vreg):
            acc = acc_ref[h]
            halves.append(acc[0:HEAD_DIM, :] / acc[HEAD_DIM:HEAD_DIM + 1, :])
        o_ref[0, :, p * LANES:(p + 1) * LANES] = jnp.concatenate(halves, axis=0).T.astype(BF16)


def _attn_prompt(tab_t, q, k, v, bias_own_t, bias_adj_t):
    b, s, a = q.shape
    blk = MOBA_BLOCK
    nb = s // blk
    nbp = 8
    assert s % blk == 0 and nb <= nbp
    full = pl.BlockSpec((1, s, a), lambda bi, i: (bi, 0, 0))
    tile = pl.BlockSpec((1, blk, a), lambda bi, i: (bi, i, 0))
    bias = pl.BlockSpec((N_HEADS, blk, blk), lambda bi, i: (0, 0, 0))
    return pl.pallas_call(
        functools.partial(_attn_prompt_kernel, nb=nb),
        grid=(b, nb),
        in_specs=[pl.BlockSpec(memory_space=pltpu.SMEM), tile, full, full, bias, bias],
        out_specs=tile,
        out_shape=jax.ShapeDtypeStruct((b, s, a), BF16),
        scratch_shapes=[pltpu.VMEM((s, a), BF16), pltpu.VMEM((nb, N_HEADS, VT_ROWS, blk), BF16),
                        pltpu.VMEM((nbp, a), F32), pltpu.VMEM((N_HEADS, blk, LANES), BF16),
                        pltpu.VMEM((N_HEADS, nbp, blk), F32), pltpu.VMEM((N_HEADS, 1, blk), F32),
                        pltpu.VMEM((N_HEADS, VT_ROWS, blk), F32),
                        pltpu.VMEM((FAR_GROUP, N_HEADS, blk, blk), F32)],
        compiler_params=pltpu.CompilerParams(dimension_semantics=("arbitrary", "arbitrary"),
                                             vmem_limit_bytes=VMEM_LIMIT),
        name="moba_prompt",
    )(tab_t, q, k, v, bias_own_t, bias_adj_t)


def _pool_rows(u, hist, w_ref, scale_ref, first_row_pos):
    tp = u.shape[0]
    ext = jnp.concatenate([hist, u], axis=0)
    pos = first_row_pos + lax.broadcasted_iota(jnp.int32, (tp, 1), 0)
    outs = []
    for g, w in enumerate(POOL_WINDOWS):
        gs = slice(g * POOL_GROUP, (g + 1) * POOL_GROUP)
        s = ext[:, gs]
        shift = 1
        while shift < w:
            s = s + pltpu.roll(s, shift, 0)
            shift *= 2
        cnt = jnp.minimum(pos + 1, w).astype(F32)
        d = s[POOL_HALO:, :] / cnt - u[:, gs]
        outs.append((_dot(d.astype(BF16), w_ref[g]) * scale_ref[:, gs]).astype(BF16))
    return jnp.concatenate(outs, axis=1)


def _pool_kernel(u_ref, hist_ref, w_ref, scale_ref, o_ref, *, first_pos):
    i = pl.program_id(1)
    hist = jnp.where(i > 0, hist_ref[0], 0.0)
    o_ref[0] = _pool_rows(u_ref[0], hist, w_ref, scale_ref, first_pos + i * u_ref.shape[1])


def _halo_spec(tile_rows, width):
    per_tile = tile_rows // POOL_HALO
    return pl.BlockSpec((1, POOL_HALO, width), lambda b, i, *_: (b, jnp.maximum(i * per_tile - 1, 0), 0))


def _pool(u, w_pool_b, pool_scale, first_pos):
    nb, s, c = u.shape
    tp = _row_tile(s, (512, 256, 128, 64, 32, 16))
    return pl.pallas_call(
        functools.partial(_pool_kernel, first_pos=first_pos),
        grid=(nb, s // tp),
        in_specs=[pl.BlockSpec((1, tp, c), lambda b, i: (b, i, 0)),
                  _halo_spec(tp, c),
                  pl.BlockSpec(w_pool_b.shape, lambda b, i: (0, 0, 0)),
                  pl.BlockSpec((1, c), lambda b, i: (0, 0))],
        out_specs=pl.BlockSpec((1, tp, c), lambda b, i: (b, i, 0)),
        out_shape=jax.ShapeDtypeStruct((nb, s, c), BF16),
        compiler_params=pltpu.CompilerParams(dimension_semantics=("arbitrary", "arbitrary"),
                                             vmem_limit_bytes=VMEM_LIMIT),
        name="multi_pool",
    )(u, u, w_pool_b, pool_scale)


def _page_block_mean(page_refs, blk):
    s = page_refs[blk * PAGES_PER_BLOCK][0, 0]
    for pg in range(1, PAGES_PER_BLOCK):
        s = s + page_refs[blk * PAGES_PER_BLOCK + pg][0, 0]
    return jnp.sum(s, axis=-1) * (1.0 / MOBA_BLOCK)


def _out_kernel(pt_ref, x_ref, attn_ref, pool_ref, mods_ref, gm_ref, gf_ref, wo_ref, wu_ref, wd_ref, *refs,
                n_pages, fused_pool):
    if fused_pool:
        hist_ref, wp_ref, ps_ref = refs[:3]
        refs = refs[3:]
        i = pl.program_id(1)
        hist = jnp.where(i > 0, hist_ref[0], 0.0)
        pool = _pool_rows(pool_ref[0], hist, wp_ref, ps_ref, i * x_ref.shape[1])
    else:
        pool = pool_ref[0]
    page_refs, y_ref = refs[:n_pages], refs[n_pages]
    n_blocks = n_pages // PAGES_PER_BLOCK
    d = x_ref.shape[-1]
    a = ATTN_WIDTH
    x = x_ref[0]
    g1 = mods_ref[0, :, 2 * d:3 * d]
    sh2 = mods_ref[0, :, 3 * d:4 * d]
    sc2 = mods_ref[0, :, 4 * d:5 * d]
    g2 = mods_ref[0, :, 5 * d:6 * d]
    mix = _dot(attn_ref[0], wo_ref[0:a, :]) + _dot(pool, wo_ref[a:, :])
    x1 = x + g1 * mix
    hb = (_rms(x1, gm_ref[...]) * (1.0 + sc2) + sh2).astype(BF16)
    acc = jnp.zeros(x.shape, F32)
    n_chunks = wu_ref.shape[1] // FF_CHUNK
    for c in range(n_chunks):
        cs = slice(c * FF_CHUNK, (c + 1) * FF_CHUNK)
        t = jnp.maximum(_dot(hb, wu_ref[:, cs]), 0.0)
        acc = acc + _dot((t * t).astype(BF16), wd_ref[cs, :])
        for blk in range(c * n_blocks // n_chunks, (c + 1) * n_blocks // n_chunks):
            refs[n_pages + 1][0, blk] = _page_block_mean(page_refs, blk)
    x2 = x1 + g2 * acc
    y_ref[0] = _rms(x2, gf_ref[...])


def _out(x, attn, pool, mods, g_mlp, g_final, w_out_b, w_up_b, w_down_b, page_table, pool_params=None,
         cache_kt=None, tm=None):
    nb, s, d = x.shape
    r = mods.shape[1]
    tm = tm or _row_tile(s)
    n_tiles = s // tm
    row_spec = lambda w: pl.BlockSpec((1, tm, w), lambda b, i, pt: (b, i, 0))
    mods_spec = (pl.BlockSpec((1, 1, N_MOD * d), lambda b, i, pt: (b, 0, 0)) if r == 1
                 else pl.BlockSpec((1, tm, N_MOD * d), lambda b, i, pt: (b, i, 0)))
    resident = lambda w: pl.BlockSpec(w.shape, lambda b, i, pt: (0, 0), pipeline_mode=pl.Buffered(1))
    vec = pl.BlockSpec((1, d), lambda b, i, pt: (0, 0))
    page_specs, out_specs, out_shape = [], [row_spec(d)], [jax.ShapeDtypeStruct((nb, s, d), F32)]
    pool_specs, pool_args = [], []
    if pool_params is not None:
        w_pool_b, pool_scale = pool_params
        pool_specs = [_halo_spec(tm, POOL_WIDTH), pl.BlockSpec(w_pool_b.shape, lambda b, i, pt: (0, 0, 0)),
                      pl.BlockSpec((1, POOL_WIDTH), lambda b, i, pt: (0, 0))]
        pool_args = [pool, w_pool_b, pool_scale]
    if cache_kt is not None:
        db, n_pages = page_table.shape
        nblk = n_pages // PAGES_PER_BLOCK
        need = -(-db * nblk // (nb * n_tiles))
        group = min(g for g in range(1, nblk + 1) if nblk % g == 0 and g >= need)
        groups_per_seq = nblk // group
        pps = group * PAGES_PER_BLOCK

        def group_of(b, i):
            g = jnp.minimum(b * n_tiles + i, db * groups_per_seq - 1)
            return g // groups_per_seq, g % groups_per_seq

        def page_spec(n):
            def index(b, i, pt):
                seq, part = group_of(b, i)
                return (0, pt[seq, part * pps + n], 0, 0, 0)
            return pl.BlockSpec((1, 1) + cache_kt.shape[2:], index)

        page_specs = [page_spec(n) for n in range(pps)]
        out_specs.append(pl.BlockSpec((1, group, N_HEADS, HEAD_DIM), lambda b, i, pt: group_of(b, i) + (0, 0)))
        out_shape.append(jax.ShapeDtypeStruct((db, nblk, N_HEADS, HEAD_DIM), F32))
    grid_spec = pltpu.PrefetchScalarGridSpec(
        num_scalar_prefetch=1,
        grid=(nb, n_tiles),
        in_specs=[row_spec(d), row_spec(ATTN_WIDTH), row_spec(POOL_WIDTH), mods_spec, vec, vec,
                  resident(w_out_b), resident(w_up_b), resident(w_down_b)] + pool_specs + page_specs,
        out_specs=out_specs,
    )
    out = pl.pallas_call(
        functools.partial(_out_kernel, n_pages=len(page_specs), fused_pool=pool_params is not None),
        grid_spec=grid_spec,
        out_shape=out_shape,
        compiler_params=pltpu.CompilerParams(dimension_semantics=("arbitrary", "arbitrary"),
                                             vmem_limit_bytes=VMEM_LIMIT),
        name="out_mlp",
    )(page_table, x, attn, pool, mods, g_mlp, g_final, w_out_b, w_up_b, w_down_b, *pool_args,
      *([cache_kt] * len(page_specs)))
    return out if cache_kt is not None else out[0]


def _select_kernel(q_ref, km_ref, sel_ref):
    for sq in range(q_ref.shape[0]):
        g = lax.dot_general(q_ref[sq].astype(F32), km_ref[sq], (((1,), (1,)), ((), ())),
                            precision=lax.Precision.HIGHEST, preferred_element_type=F32)
        rows, nblk = g.shape
        lane = lax.broadcasted_iota(jnp.int32, g.shape, 1)
        out_lane = lax.broadcasted_iota(jnp.int32, (rows, LANES), 1)
        out = jnp.zeros((rows, LANES), jnp.int32)
        for n in range(MOBA_TOPK):
            mx = jnp.max(g, axis=-1, keepdims=True)
            idx = jnp.min(jnp.where(g == mx, lane, nblk), axis=-1, keepdims=True)
            out = jnp.where(out_lane == n, idx, out)
            g = jnp.where(lane == idx, -jnp.inf, g)
        sel_ref[sq] = out


def _select(q_bd, kmean2d):
    db, rows, a = q_bd.shape
    nblk = kmean2d.shape[1]
    seqs = math.gcd(db, SELECT_SEQS_PER_STEP)
    return pl.pallas_call(
        _select_kernel,
        grid=(db // seqs,),
        in_specs=[pl.BlockSpec((seqs, rows, a), lambda b: (b, 0, 0)),
                  pl.BlockSpec((seqs, nblk, a), lambda b: (b, 0, 0))],
        out_specs=pl.BlockSpec((seqs, rows, LANES), lambda b: (b, 0, 0)),
        out_shape=jax.ShapeDtypeStruct((db, rows, LANES), jnp.int32),
        compiler_params=pltpu.CompilerParams(dimension_semantics=("arbitrary",)),
        name="moba_select",
    )(q_bd, kmean2d)


def _attn_sample_kernel(pt_ref, sel_ref, tab_ref, qkv_ref, badj_ref, bown_ref, ck_ref, cv_ref,
                        o_ref, kbuf, vbuf, sem, *, nblk, t_new):
    b = pl.program_id(0)
    n_seq = pl.num_programs(0)
    n_sel = MOBA_TOPK
    per_head = t_new * n_sel
    buf = b % 2

    def block_of(sb, h, t, n):
        return sel_ref[((sb * N_HEADS + h) * t_new + t) * n_sel + n]

    def copies(sb, h, bf):
        out = []
        for t in range(t_new):
            for n in range(n_sel):
                j = block_of(sb, h, t, n)
                slot = (bf * N_HEADS + h) * per_head + t * n_sel + n
                for pg in range(PAGES_PER_BLOCK):
                    page = pt_ref[sb, j * PAGES_PER_BLOCK + pg]
                    keys = pl.ds(pg * PAGE_SIZE, PAGE_SIZE)
                    out.append(pltpu.make_async_copy(ck_ref.at[0, page, h], kbuf.at[slot, :, keys], sem.at[0, bf]))
                    out.append(pltpu.make_async_copy(cv_ref.at[0, page, h], vbuf.at[slot, :, keys], sem.at[1, bf]))
        return out

    @pl.when(b == 0)
    def _():
        for h in range(N_HEADS):
            for c in copies(b, h, buf):
                c.start()

    for h in range(N_HEADS):
        for c in copies(b, h, buf):
            c.wait()

    def head_attention(h):
        slot0 = (buf * N_HEADS + h) * per_head
        c_far = tab_ref[h, N_BUCKETS - 1] * LOG2E
        kn = qkv_ref[0, N_HEADS + h]
        vn = qkv_ref[0, 2 * N_HEADS + h]
        out_lane = lax.broadcasted_iota(jnp.int32, (HEAD_DIM, LANES), 1)
        out = jnp.zeros((HEAD_DIM, LANES), F32)
        for t in range(t_new):
            qt = qkv_ref[0, h, :, t:t + 1]
            s_own = jnp.sum(kn * qt, axis=0, keepdims=True) + bown_ref[h, t:t + 1, 0:t_new]
            scores = []
            for n in range(n_sel):
                j = block_of(b, h, t, n)
                s = jnp.sum(kbuf[slot0 + t * n_sel + n] * qt, axis=0, keepdims=True)
                scores.append(s + jnp.where(j == nblk - 1, badj_ref[h, t:t + 1, :], c_far))
            m = jnp.max(s_own, axis=-1, keepdims=True)
            for s in scores:
                m = jnp.maximum(m, jnp.max(s, axis=-1, keepdims=True))
            p_own = jnp.exp2(s_own - m)
            l = jnp.sum(p_own, axis=-1, keepdims=True)
            acc = jnp.sum(vn * p_own, axis=-1, keepdims=True)
            pv = None
            for n, s in enumerate(scores):
                p = jnp.exp2(s - m)
                l = l + jnp.sum(p, axis=-1, keepdims=True)
                term = vbuf[slot0 + t * n_sel + n] * p
                pv = term if pv is None else pv + term
            acc = acc + jnp.sum(pv, axis=-1, keepdims=True)
            out = jnp.where(out_lane == t, acc / l, out)
        o_ref[0, h] = out

    nxt = jnp.minimum(b + 1, n_seq - 1)
    for h in range(N_HEADS):
        for c in copies(nxt, h, 1 - buf):
            c.start()
        head_attention(h)

    @pl.when(b == n_seq - 1)
    def _():
        for h in range(N_HEADS):
            for c in copies(nxt, h, 1 - buf):
                c.wait()


def _attn_sample(page_table, sel_flat, tab_t, qkv_t, badj, bown, cache_kt, cache_vt, nblk):
    db, nh3, dh, t_new = qkv_t.shape
    nh = nh3 // 3
    n_slots = 2 * nh * t_new * MOBA_TOPK
    grid_spec = pltpu.PrefetchScalarGridSpec(
        num_scalar_prefetch=2,
        grid=(db,),
        in_specs=[pl.BlockSpec(memory_space=pltpu.SMEM),
                  pl.BlockSpec((1, nh3, dh, t_new), lambda b, pt, sel: (b, 0, 0, 0)),
                  pl.BlockSpec(badj.shape, lambda b, pt, sel: (0, 0, 0)),
                  pl.BlockSpec(bown.shape, lambda b, pt, sel: (0, 0, 0)),
                  pl.BlockSpec(memory_space=pl.ANY), pl.BlockSpec(memory_space=pl.ANY)],
        out_specs=pl.BlockSpec((1, nh, dh, LANES), lambda b, pt, sel: (b, 0, 0, 0)),
        scratch_shapes=[pltpu.VMEM((n_slots, dh, MOBA_BLOCK), F32),
                        pltpu.VMEM((n_slots, dh, MOBA_BLOCK), F32),
                        pltpu.SemaphoreType.DMA((2, 2))],
    )
    return pl.pallas_call(
        functools.partial(_attn_sample_kernel, nblk=nblk, t_new=t_new),
        grid_spec=grid_spec,
        out_shape=jax.ShapeDtypeStruct((db, nh, dh, LANES), F32),
        compiler_params=pltpu.CompilerParams(dimension_semantics=("arbitrary",),
                                             vmem_limit_bytes=VMEM_LIMIT),
        name="moba_sample",
    )(page_table, sel_flat, tab_t, qkv_t, badj, bown, cache_kt, cache_vt)


def kernel(x_prompt, x_sample, cache_k, cache_v, state_pool, page_table, c_prompt, c_sample, w_ada, b_ada, norm_mix, w_in, rel_bias, w_pool, pool_scale, w_out, norm_mlp, w_up, w_down, norm_final):
    assert w_ada.shape[0] == 1, "single-layer decoder"
    b, s, d = x_prompt.shape
    db, t_new, _ = x_sample.shape
    n_pages = page_table.shape[1]
    past = n_pages * PAGE_SIZE
    assert past % MOBA_BLOCK == 0 and past // MOBA_BLOCK >= MOBA_TOPK
    nblk = past // MOBA_BLOCK

    w_in_b = w_in[0].astype(BF16)
    w_out_b = w_out[0].astype(BF16)
    w_up_b = w_up[0].astype(BF16)
    w_down_b = w_down[0].astype(BF16)
    w_pool_b = w_pool[0].astype(BF16)
    g_final = norm_final.reshape(1, d)
    tab_t = rel_bias.T

    mods = _ada(jnp.concatenate([c_prompt, c_sample], axis=0), w_ada, b_ada[0:1])
    mods_p = mods[:b].reshape(b, 1, N_MOD * d)
    mods_s = jnp.repeat(mods[b:], t_new, axis=0).reshape(1, db * t_new, N_MOD * d)
    bias_own, bias_adj = _bias_tiles(tab_t)

    cache_kt = cache_k.transpose(0, 1, 3, 4, 2)
    cache_vt = cache_v.transpose(0, 1, 3, 4, 2)

    q_p, k_p, v_p, u_p = _inproj(x_prompt, mods_p, norm_mix, w_in_b)
    attn_p = _attn_prompt(tab_t, q_p, k_p, v_p, bias_own, bias_adj)
    y_p, kmean = _out(x_prompt, attn_p, u_p, mods_p, norm_mlp, g_final, w_out_b, w_up_b, w_down_b,
                      page_table, pool_params=(w_pool_b, pool_scale), cache_kt=cache_kt,
                      tm=_row_tile(s, (OUT_TILE_WITH_PAGES, 128, 64, 32, 16)))

    xs = x_sample.reshape(1, db * t_new, d)
    q_s, k_s, v_s, u_s = _inproj(xs, mods_s, norm_mix, w_in_b)
    head_of_lane = jnp.arange(ATTN_WIDTH) // HEAD_DIM
    q_bt = q_s.reshape(db, 1, t_new, ATTN_WIDTH)
    q_bd = jnp.where(head_of_lane[None, None, None, :] == jnp.arange(N_HEADS)[None, :, None, None],
                     q_bt, jnp.zeros_like(q_bt)).reshape(db, N_HEADS * t_new, ATTN_WIDTH)
    sel = _select(q_bd, kmean.reshape(db, nblk, ATTN_WIDTH))[:, :, :MOBA_TOPK]
    assert t_new <= 8
    qkv_t = jnp.concatenate([q_s.astype(F32), k_s, v_s], axis=-1)
    qkv_t = qkv_t.reshape(db, t_new, 3 * N_HEADS, HEAD_DIM).transpose(0, 2, 3, 1)
    attn_s = _attn_sample(page_table, sel.reshape(-1), tab_t, qkv_t,
                          bias_adj[:, :, :8].transpose(0, 2, 1), bias_own[:, :LANES, :8].transpose(0, 2, 1),
                          cache_kt, cache_vt, nblk)
    attn_s = attn_s[..., :t_new].transpose(0, 3, 1, 2).reshape(1, db * t_new, ATTN_WIDTH).astype(BF16)

    u_full = jnp.concatenate([state_pool[0], u_s.reshape(db, t_new, POOL_WIDTH)], axis=1)
    ext_rows = POOL_HALO + 8
    u_ext = jnp.pad(u_full, ((0, 0), (1, ext_rows - 1 - u_full.shape[1]), (0, 0)))
    pool_ext = _pool(u_ext.reshape(1, db * ext_rows, POOL_WIDTH), w_pool_b, pool_scale, first_pos=past)
    pool_s = pool_ext.reshape(db, ext_rows, POOL_WIDTH)[:, POOL_HALO:POOL_HALO + t_new]
    pool_s = pool_s.reshape(1, db * t_new, POOL_WIDTH)
    y_s = _out(xs, attn_s, pool_s, mods_s, norm_mlp, g_final, w_out_b, w_up_b, w_down_b, page_table)

    heads = lambda z, n, l: z.reshape(1, n, l, N_HEADS, HEAD_DIM)
    return (y_p, y_s.reshape(db, t_new, d),
            heads(k_p, b, s), heads(v_p, b, s), u_p[:, -POOL_STATE:][None],
            heads(k_s, db, t_new), heads(v_s, db, t_new), u_full[:, -POOL_STATE:][None])
```

```python
import functools
import math

import numpy as np
import jax
import jax.numpy as jnp
from jax import lax
from jax.experimental import pallas as pl
from jax.experimental.pallas import tpu as pltpu

HEAD_DIM = 64
N_HEADS = 8
ATTN_WIDTH = N_HEADS * HEAD_DIM
POOL_WINDOWS = (2, 4, 8, 16)
POOL_GROUP = 128
POOL_WIDTH = POOL_GROUP * len(POOL_WINDOWS)
POOL_STATE = max(POOL_WINDOWS) - 1
POOL_HALO = POOL_STATE + 1
MOBA_BLOCK = 256
MOBA_TOPK = 3
N_BUCKETS = 32
MAX_DISTANCE = 128
PAGE_SIZE = 128
PAGES_PER_BLOCK = MOBA_BLOCK // PAGE_SIZE
N_MOD = 6
EPS = 1e-6
NEG = -1e30
SCALE = HEAD_DIM ** -0.5
LOG2E = math.log2(math.e)
Q_SCALE = SCALE * LOG2E
LANES = 128
BF16_SUBLANES = 16
VT_ROWS = HEAD_DIM + BF16_SUBLANES
FF_CHUNK = 1024
SELECT_SEQS_PER_STEP = 8
FAR_GROUP = 2
OUT_TILE_WITH_PAGES = 256
VMEM_LIMIT = 56 * 1024 * 1024

BF16 = jnp.bfloat16
F32 = jnp.float32


def _nt_dot(a, b):
    return lax.dot_general(a, b, (((1,), (1,)), ((), ())), preferred_element_type=F32)


def _dot(a, b):
    return jnp.dot(a, b, preferred_element_type=F32)


def _row_tile(n, candidates=(512, 256, 128, 64, 32, 16, 8)):
    for c in candidates:
        if n % c == 0:
            return c
    raise ValueError(f"row count {n} is not a multiple of 8")


def _t5_bucket_np(rel):
    n = np.maximum(rel, 0)
    max_exact = N_BUCKETS // 2
    nf = np.maximum(n, max_exact).astype(np.float32)
    large = max_exact + (np.log(nf / np.float32(max_exact)) / np.float32(math.log(MAX_DISTANCE / max_exact))
                         * np.float32(N_BUCKETS - max_exact)).astype(np.int32)
    large = np.minimum(large, N_BUCKETS - 1)
    return np.where(n < max_exact, n, large).astype(np.int32)


def _rms(x, g):
    return x * lax.rsqrt(jnp.mean(x * x, axis=-1, keepdims=True) + EPS) * g


def _ada_kernel(c_ref, w_ref, b_ref, o_ref):
    c = c_ref[...]
    s = c / (1.0 + jnp.exp(-c))
    o_ref[...] = _dot(s.astype(BF16), w_ref[0].astype(BF16)) + b_ref[...]


def _ada(c_all, w_ada, b_ada):
    n, d = c_all.shape
    width = w_ada.shape[-1]
    tn = 1024
    return pl.pallas_call(
        _ada_kernel,
        grid=(width // tn,),
        in_specs=[pl.BlockSpec((n, d), lambda j: (0, 0)),
                  pl.BlockSpec((1, d, tn), lambda j: (0, 0, j)),
                  pl.BlockSpec((1, tn), lambda j: (0, j))],
        out_specs=pl.BlockSpec((n, tn), lambda j: (0, j)),
        out_shape=jax.ShapeDtypeStruct((n, width), F32),
        compiler_params=pltpu.CompilerParams(dimension_semantics=("arbitrary",), vmem_limit_bytes=VMEM_LIMIT),
        name="ada_mod",
    )(c_all, w_ada, b_ada)


def _inproj_kernel(x_ref, mods_ref, g_ref, w_ref, q_ref, k_ref, v_ref, u_ref):
    d = x_ref.shape[-1]
    x = x_ref[0]
    shift = mods_ref[0, :, 0:d]
    scale = mods_ref[0, :, d:2 * d]
    h = _rms(x, g_ref[...]) * (1.0 + scale) + shift
    r = _dot(h.astype(BF16), w_ref[...])
    a = ATTN_WIDTH
    q_ref[0] = (r[:, 0:a] * Q_SCALE).astype(BF16)
    k_ref[0] = r[:, a:2 * a]
    v_ref[0] = r[:, 2 * a:3 * a]
    u_ref[0] = r[:, 3 * a:]


def _inproj(x, mods, g, w_in_b):
    nb, s, d = x.shape
    r = mods.shape[1]
    ts = _row_tile(s)
    width = w_in_b.shape[1]
    row_spec = lambda w: pl.BlockSpec((1, ts, w), lambda b, i: (b, i, 0))
    mods_spec = (pl.BlockSpec((1, 1, N_MOD * d), lambda b, i: (b, 0, 0)) if r == 1
                 else pl.BlockSpec((1, ts, N_MOD * d), lambda b, i: (b, i, 0)))
    return pl.pallas_call(
        _inproj_kernel,
        grid=(nb, s // ts),
        in_specs=[row_spec(d), mods_spec,
                  pl.BlockSpec((1, d), lambda b, i: (0, 0)),
                  pl.BlockSpec((d, width), lambda b, i: (0, 0))],
        out_specs=[row_spec(ATTN_WIDTH), row_spec(ATTN_WIDTH), row_spec(ATTN_WIDTH), row_spec(POOL_WIDTH)],
        out_shape=[jax.ShapeDtypeStruct((nb, s, ATTN_WIDTH), BF16),
                   jax.ShapeDtypeStruct((nb, s, ATTN_WIDTH), F32),
                   jax.ShapeDtypeStruct((nb, s, ATTN_WIDTH), F32),
                   jax.ShapeDtypeStruct((nb, s, POOL_WIDTH), F32)],
        compiler_params=pltpu.CompilerParams(dimension_semantics=("arbitrary", "arbitrary"),
                                             vmem_limit_bytes=VMEM_LIMIT),
        name="in_proj",
    )(x, mods, g, w_in_b)


def _bias_kernel(tab_ref, idx_own_ref, idx_adj_ref, own_ref, adj_ref):
    h = pl.program_id(0)
    io = idx_own_ref[...]
    ia = idx_adj_ref[...]
    bo = jnp.zeros(io.shape, F32)
    ba = jnp.zeros(ia.shape, F32)
    for b in range(N_BUCKETS):
        t = tab_ref[h, b]
        bo = jnp.where(io == b, t, bo)
        ba = jnp.where(ia == b, t, ba)
    own_ref[0] = jnp.where(io < 0, NEG, bo * LOG2E)
    adj_ref[0] = ba * LOG2E


def _bias_tiles(tab_t):
    blk = MOBA_BLOCK
    r = np.arange(blk)[None, :]
    c = np.arange(blk)[:, None]
    idx_own = np.where(r >= c, _t5_bucket_np(r - c), -1).astype(np.int32)
    idx_adj = _t5_bucket_np(blk + r - c)
    tile = pl.BlockSpec((blk, blk), lambda h: (0, 0))
    out = pl.BlockSpec((1, blk, blk), lambda h: (h, 0, 0))
    return pl.pallas_call(
        _bias_kernel,
        grid=(N_HEADS,),
        in_specs=[pl.BlockSpec(memory_space=pltpu.SMEM), tile, tile],
        out_specs=[out, out],
        out_shape=[jax.ShapeDtypeStruct((N_HEADS, blk, blk), F32)] * 2,
        compiler_params=pltpu.CompilerParams(dimension_semantics=("arbitrary",)),
        name="rel_bias_tiles",
    )(tab_t, jnp.asarray(idx_own), jnp.asarray(idx_adj))


def _attn_prompt_kernel(tab_ref, q_ref, k_ref, v_ref, bown_ref, badj_ref, o_ref,
                        kb_ref, vt_ref, km_ref, qm_ref, madd_ref, m_ref, acc_ref, s_ref, *, nb):
    i = pl.program_id(1)
    blk = MOBA_BLOCK
    nbp = km_ref.shape[0]

    @pl.when(i == 0)
    def _():
        kb_ref[...] = k_ref[0].astype(BF16)
        ones = jnp.ones((VT_ROWS - HEAD_DIM, blk), BF16)
        for j in range(nb):
            vt = v_ref[0, j * blk:(j + 1) * blk, :].T.astype(BF16)
            for h in range(N_HEADS):
                vt_ref[j, h, 0:HEAD_DIM, :] = vt[h * HEAD_DIM:(h + 1) * HEAD_DIM, :]
                vt_ref[j, h, HEAD_DIM:, :] = ones
        means = [jnp.mean(k_ref[0, j * blk:(j + 1) * blk, :], axis=0, keepdims=True) for j in range(nb)]
        means += [jnp.zeros_like(means[0])] * (nbp - nb)
        km_ref[...] = jnp.concatenate(means, axis=0)

    lane = lax.broadcasted_iota(jnp.int32, (blk, LANES), 1)
    blk_row = lax.broadcasted_iota(jnp.int32, (nbp, blk), 0)
    valid = blk_row < i
    jm1 = jnp.maximum(i - 1, 0)
    own0 = pl.multiple_of(i * blk, blk)
    adj0 = pl.multiple_of(jm1 * blk, blk)
    heads_per_vreg = LANES // HEAD_DIM

    def lanes_of(h):
        p = h // heads_per_vreg
        return slice(p * LANES, (p + 1) * LANES)

    for h in range(N_HEADS):
        hh = h % heads_per_vreg
        q2 = q_ref[0, :, lanes_of(h)]
        hmask = (lane >= hh * HEAD_DIM) & (lane < (hh + 1) * HEAD_DIM)
        qm = jnp.where(hmask, q2, jnp.zeros_like(q2))
        qm_ref[h] = qm
        gate = _nt_dot(km_ref[:, lanes_of(h)].astype(BF16), qm)
        for j in range(nb):
            gj = gate[j:j + 1, :]
            beats = ((gate > gj) | ((gate == gj) & (blk_row < j))) & valid
            cnt = jnp.sum(beats.astype(F32), axis=0, keepdims=True)
            cnt = cnt + jnp.where(j < i, 0.0, float(nb))
            madd_ref[h, j:j + 1, :] = jnp.where(cnt < MOBA_TOPK, 0.0, NEG)

    def key_blocks(blocks):
        for n, (key0, _, tile_bias, _, _) in enumerate(blocks):
            for h in range(N_HEADS):
                s = _nt_dot(kb_ref[pl.ds(key0, blk), lanes_of(h)], qm_ref[h])
                s_ref[n, h] = s if tile_bias is None else s + tile_bias(h)
        for n, (_, blk_idx, _, query_bias, first) in enumerate(blocks):
            for h in range(N_HEADS):
                s = s_ref[n, h]
                m_blk = jnp.max(s, axis=0, keepdims=True)
                if query_bias is not None:
                    qb = query_bias(h)
                    m_blk = m_blk + qb
                if first:
                    m_new = m_blk
                else:
                    m_old = m_ref[h]
                    m_new = jnp.maximum(m_old, m_blk)
                ref_row = m_new if query_bias is None else m_new - qb
                p = jnp.exp2(s - ref_row).astype(BF16)
                pv = _dot(vt_ref[blk_idx, h], p)
                acc_ref[h] = pv if first else jnp.exp2(m_old - m_new) * acc_ref[h] + pv
                m_ref[h] = m_new

    def far_block(j):
        return (pl.multiple_of(j * blk, blk), j, None,
                lambda h: madd_ref[h, pl.ds(j, 1), :] + tab_ref[h, N_BUCKETS - 1] * LOG2E, False)

    key_blocks([(own0, i, lambda h: bown_ref[h], None, True),
                (adj0, jm1, lambda h: badj_ref[h], lambda h: madd_ref[h, pl.ds(jm1, 1), :], False)])

    def far_pair(jj, carry):
        key_blocks([far_block(FAR_GROUP * jj + n) for n in range(FAR_GROUP)])
        return carry

    lax.fori_loop(0, jm1 // FAR_GROUP, far_pair, 0)
    for n in range(1, FAR_GROUP):
        @pl.when(jm1 % FAR_GROUP >= n)
        def _():
            key_blocks([far_block(jm1 - n)])

    for p in range(ATTN_WIDTH // LANES):
        halves = []
        for h in range(p * heads_per_vreg, (p + 1) * heads_per_vreg):
            acc = acc_ref[h]
            halves.append(acc[0:HEAD_DIM, :] / acc[HEAD_DIM:HEAD_DIM + 1, :])
        o_ref[0, :, p * LANES:(p + 1) * LANES] = jnp.concatenate(halves, axis=0).T.astype(BF16)


def _attn_prompt(tab_t, q, k, v, bias_own_t, bias_adj_t):
    b, s, a = q.shape
    blk = MOBA_BLOCK
    nb = s // blk
    nbp = 8
    assert s % blk == 0 and nb <= nbp
    full = pl.BlockSpec((1, s, a), lambda bi, i: (bi, 0, 0))
    tile = pl.BlockSpec((1, blk, a), lambda bi, i: (bi, i, 0))
    bias = pl.BlockSpec((N_HEADS, blk, blk), lambda bi, i: (0, 0, 0))
    return pl.pallas_call(
        functools.partial(_attn_prompt_kernel, nb=nb),
        grid=(b, nb),
        in_specs=[pl.BlockSpec(memory_space=pltpu.SMEM), tile, full, full, bias, bias],
        out_specs=tile,
        out_shape=jax.ShapeDtypeStruct((b, s, a), BF16),
        scratch_shapes=[pltpu.VMEM((s, a), BF16), pltpu.VMEM((nb, N_HEADS, VT_ROWS, blk), BF16),
                        pltpu.VMEM((nbp, a), F32), pltpu.VMEM((N_HEADS, blk, LANES), BF16),
                        pltpu.VMEM((N_HEADS, nbp, blk), F32), pltpu.VMEM((N_HEADS, 1, blk), F32),
                        pltpu.VMEM((N_HEADS, VT_ROWS, blk), F32),
                        pltpu.VMEM((FAR_GROUP, N_HEADS, blk, blk), F32)],
        compiler_params=pltpu.CompilerParams(dimension_semantics=("arbitrary", "arbitrary"),
                                             vmem_limit_bytes=VMEM_LIMIT),
        name="moba_prompt",
    )(tab_t, q, k, v, bias_own_t, bias_adj_t)


def _pool_rows(u, hist, w_ref, scale_ref, first_row_pos):
    tp = u.shape[0]
    ext = jnp.concatenate([hist, u], axis=0)
    pos = first_row_pos + lax.broadcasted_iota(jnp.int32, (tp, 1), 0)
    outs = []
    for g, w in enumerate(POOL_WINDOWS):
        gs = slice(g * POOL_GROUP, (g + 1) * POOL_GROUP)
        s = ext[:, gs]
        shift = 1
        while shift < w:
            s = s + pltpu.roll(s, shift, 0)
            shift *= 2
        cnt = jnp.minimum(pos + 1, w).astype(F32)
        d = s[POOL_HALO:, :] / cnt - u[:, gs]
        outs.append((_dot(d.astype(BF16), w_ref[g]) * scale_ref[:, gs]).astype(BF16))
    return jnp.concatenate(outs, axis=1)


def _pool_kernel(u_ref, hist_ref, w_ref, scale_ref, o_ref, *, first_pos):
    i = pl.program_id(1)
    hist = jnp.where(i > 0, hist_ref[0], 0.0)
    o_ref[0] = _pool_rows(u_ref[0], hist, w_ref, scale_ref, first_pos + i * u_ref.shape[1])


def _halo_spec(tile_rows, width):
    per_tile = tile_rows // POOL_HALO
    return pl.BlockSpec((1, POOL_HALO, width), lambda b, i, *_: (b, jnp.maximum(i * per_tile - 1, 0), 0))


def _pool(u, w_pool_b, pool_scale, first_pos):
    nb, s, c = u.shape
    tp = _row_tile(s, (512, 256, 128, 64, 32, 16))
    return pl.pallas_call(
        functools.partial(_pool_kernel, first_pos=first_pos),
        grid=(nb, s // tp),
        in_specs=[pl.BlockSpec((1, tp, c), lambda b, i: (b, i, 0)),
                  _halo_spec(tp, c),
                  pl.BlockSpec(w_pool_b.shape, lambda b, i: (0, 0, 0)),
                  pl.BlockSpec((1, c), lambda b, i: (0, 0))],
        out_specs=pl.BlockSpec((1, tp, c), lambda b, i: (b, i, 0)),
        out_shape=jax.ShapeDtypeStruct((nb, s, c), BF16),
        compiler_params=pltpu.CompilerParams(dimension_semantics=("arbitrary", "arbitrary"),
                                             vmem_limit_bytes=VMEM_LIMIT),
        name="multi_pool",
    )(u, u, w_pool_b, pool_scale)


def _out_kernel(pt_ref, x_ref, attn_ref, pool_ref, mods_ref, gm_ref, gf_ref, wo_ref, wu_ref, wd_ref, *refs,
                fused_pool, rider):
    if fused_pool:
        hist_ref, wp_ref, ps_ref = refs[:3]
        refs = refs[3:]
    n_chunks = wu_ref.shape[1] // FF_CHUNK
    if rider is None:
        (y_ref,) = refs
        n_pages = 0
    else:
        ck_ref, y_ref, kmean_ref, pbuf, psem = refs
        n_pages, groups_per_seq, n_groups = rider
        step = pl.program_id(0) * pl.num_programs(1) + pl.program_id(1)
        last = pl.num_programs(0) * pl.num_programs(1) - 1
        slot = step % 2
        nxt = jnp.minimum(step + 1, last)

        def page_copies(st, sl, first, count):
            g = jnp.minimum(st, n_groups - 1)
            seq, part = g // groups_per_seq, g % groups_per_seq
            return [pltpu.make_async_copy(ck_ref.at[0, pt_ref[seq, part * n_pages + n]], pbuf.at[sl, n],
                                          psem.at[sl]) for n in range(first, first + count)]

        @pl.when(step == 0)
        def _():
            for c in page_copies(step, slot, 0, n_pages):
                c.start()

        for c in page_copies(step, slot, 0, n_pages):
            c.wait()
    n_blocks = n_pages // PAGES_PER_BLOCK
    if fused_pool:
        i = pl.program_id(1)
        hist = jnp.where(i > 0, hist_ref[0], 0.0)
        pool = _pool_rows(pool_ref[0], hist, wp_ref, ps_ref, i * x_ref.shape[1])
    else:
        pool = pool_ref[0]
    d = x_ref.shape[-1]
    a = ATTN_WIDTH
    x = x_ref[0]
    g1 = mods_ref[0, :, 2 * d:3 * d]
    sh2 = mods_ref[0, :, 3 * d:4 * d]
    sc2 = mods_ref[0, :, 4 * d:5 * d]
    g2 = mods_ref[0, :, 5 * d:6 * d]
    mix = _dot(attn_ref[0], wo_ref[0:a, :]) + _dot(pool, wo_ref[a:, :])
    x1 = x + g1 * mix
    hb = (_rms(x1, gm_ref[...]) * (1.0 + sc2) + sh2).astype(BF16)
    acc = jnp.zeros(x.shape, F32)
    for c in range(n_chunks):
        cs = slice(c * FF_CHUNK, (c + 1) * FF_CHUNK)
        t = jnp.maximum(_dot(hb, wu_ref[:, cs]), 0.0)
        acc = acc + _dot((t * t).astype(BF16), wd_ref[cs, :])
        if rider is not None:
            p0, p1 = c * n_pages // n_chunks, (c + 1) * n_pages // n_chunks
            for cp in page_copies(nxt, 1 - slot, p0, p1 - p0):
                cp.start()
            for blk in range(c * n_blocks // n_chunks, (c + 1) * n_blocks // n_chunks):
                ksum = pbuf[slot, blk * PAGES_PER_BLOCK]
                for pg in range(1, PAGES_PER_BLOCK):
                    ksum = ksum + pbuf[slot, blk * PAGES_PER_BLOCK + pg]
                kmean_ref[0, blk] = jnp.sum(ksum, axis=-1) * (1.0 / MOBA_BLOCK)
    x2 = x1 + g2 * acc
    y_ref[0] = _rms(x2, gf_ref[...])
    if rider is not None:
        @pl.when(step == last)
        def _():
            for cp in page_copies(nxt, 1 - slot, 0, n_pages):
                cp.wait()


def _out(x, attn, pool, mods, g_mlp, g_final, w_out_b, w_up_b, w_down_b, page_table, pool_params=None,
         cache_kt=None, tm=None):
    nb, s, d = x.shape
    r = mods.shape[1]
    tm = tm or _row_tile(s)
    n_tiles = s // tm
    row_spec = lambda w: pl.BlockSpec((1, tm, w), lambda b, i, pt: (b, i, 0))
    mods_spec = (pl.BlockSpec((1, 1, N_MOD * d), lambda b, i, pt: (b, 0, 0)) if r == 1
                 else pl.BlockSpec((1, tm, N_MOD * d), lambda b, i, pt: (b, i, 0)))
    resident = lambda w: pl.BlockSpec(w.shape, lambda b, i, pt: (0, 0), pipeline_mode=pl.Buffered(1))
    vec = pl.BlockSpec((1, d), lambda b, i, pt: (0, 0))
    out_specs, out_shape = [row_spec(d)], [jax.ShapeDtypeStruct((nb, s, d), F32)]
    pool_specs, pool_args, cache_specs, cache_args, scratch, rider = [], [], [], [], [], None
    if pool_params is not None:
        w_pool_b, pool_scale = pool_params
        pool_specs = [_halo_spec(tm, POOL_WIDTH), pl.BlockSpec(w_pool_b.shape, lambda b, i, pt: (0, 0, 0)),
                      pl.BlockSpec((1, POOL_WIDTH), lambda b, i, pt: (0, 0))]
        pool_args = [pool, w_pool_b, pool_scale]
    if cache_kt is not None:
        db, n_pages = page_table.shape
        nblk = n_pages // PAGES_PER_BLOCK
        need = -(-db * nblk // (nb * n_tiles))
        group = min(g for g in range(1, nblk + 1) if nblk % g == 0 and g >= need)
        groups_per_seq = nblk // group
        n_groups = db * groups_per_seq
        pps = group * PAGES_PER_BLOCK
        rider = (pps, groups_per_seq, n_groups)

        def group_of(b, i):
            g = jnp.minimum(b * n_tiles + i, n_groups - 1)
            return g // groups_per_seq, g % groups_per_seq

        cache_specs, cache_args = [pl.BlockSpec(memory_space=pl.ANY)], [cache_kt]
        scratch = [pltpu.VMEM((2, pps) + cache_kt.shape[2:], F32), pltpu.SemaphoreType.DMA((2,))]
        out_specs.append(pl.BlockSpec((1, group, N_HEADS, HEAD_DIM), lambda b, i, pt: group_of(b, i) + (0, 0)))
        out_shape.append(jax.ShapeDtypeStruct((db, nblk, N_HEADS, HEAD_DIM), F32))
    grid_spec = pltpu.PrefetchScalarGridSpec(
        num_scalar_prefetch=1,
        grid=(nb, n_tiles),
        in_specs=[row_spec(d), row_spec(ATTN_WIDTH), row_spec(POOL_WIDTH), mods_spec, vec, vec,
                  resident(w_out_b), resident(w_up_b), resident(w_down_b)] + pool_specs + cache_specs,
        out_specs=out_specs,
        scratch_shapes=scratch,
    )
    out = pl.pallas_call(
        functools.partial(_out_kernel, fused_pool=pool_params is not None, rider=rider),
        grid_spec=grid_spec,
        out_shape=out_shape,
        compiler_params=pltpu.CompilerParams(dimension_semantics=("arbitrary", "arbitrary"),
                                             vmem_limit_bytes=VMEM_LIMIT),
        name="out_mlp",
    )(page_table, x, attn, pool, mods, g_mlp, g_final, w_out_b, w_up_b, w_down_b, *pool_args, *cache_args)
    return out if cache_kt is not None else out[0]


def _select_kernel(q_ref, km_ref, sel_ref):
    for sq in range(q_ref.shape[0]):
        g = lax.dot_general(q_ref[sq].astype(F32), km_ref[sq], (((1,), (1,)), ((), ())),
                            precision=lax.Precision.HIGHEST, preferred_element_type=F32)
        rows, nblk = g.shape
        lane = lax.broadcasted_iota(jnp.int32, g.shape, 1)
        out_lane = lax.broadcasted_iota(jnp.int32, (rows, LANES), 1)
        out = jnp.zeros((rows, LANES), jnp.int32)
        for n in range(MOBA_TOPK):
            mx = jnp.max(g, axis=-1, keepdims=True)
            idx = jnp.min(jnp.where(g == mx, lane, nblk), axis=-1, keepdims=True)
            out = jnp.where(out_lane == n, idx, out)
            g = jnp.where(lane == idx, -jnp.inf, g)
        sel_ref[sq] = out


def _select(q_bd, kmean2d):
    db, rows, a = q_bd.shape
    nblk = kmean2d.shape[1]
    seqs = math.gcd(db, SELECT_SEQS_PER_STEP)
    return pl.pallas_call(
        _select_kernel,
        grid=(db // seqs,),
        in_specs=[pl.BlockSpec((seqs, rows, a), lambda b: (b, 0, 0)),
                  pl.BlockSpec((seqs, nblk, a), lambda b: (b, 0, 0))],
        out_specs=pl.BlockSpec((seqs, rows, LANES), lambda b: (b, 0, 0)),
        out_shape=jax.ShapeDtypeStruct((db, rows, LANES), jnp.int32),
        compiler_params=pltpu.CompilerParams(dimension_semantics=("arbitrary",)),
        name="moba_select",
    )(q_bd, kmean2d)


def _attn_sample_kernel(pt_ref, sel_ref, tab_ref, qkv_ref, badj_ref, bown_ref, ck_ref, cv_ref,
                        o_ref, kbuf, vbuf, sem, *, nblk, t_new):
    b = pl.program_id(0)
    n_seq = pl.num_programs(0)
    n_sel = MOBA_TOPK
    per_head = t_new * n_sel
    buf = b % 2

    def block_of(sb, h, t, n):
        return sel_ref[((sb * N_HEADS + h) * t_new + t) * n_sel + n]

    def copies(sb, h, bf):
        out = []
        for t in range(t_new):
            for n in range(n_sel):
                j = block_of(sb, h, t, n)
                slot = (bf * N_HEADS + h) * per_head + t * n_sel + n
                for pg in range(PAGES_PER_BLOCK):
                    page = pt_ref[sb, j * PAGES_PER_BLOCK + pg]
                    keys = pl.ds(pg * PAGE_SIZE, PAGE_SIZE)
                    out.append(pltpu.make_async_copy(ck_ref.at[0, page, h], kbuf.at[slot, :, keys], sem.at[0, bf]))
                    out.append(pltpu.make_async_copy(cv_ref.at[0, page, h], vbuf.at[slot, :, keys], sem.at[1, bf]))
        return out

    @pl.when(b == 0)
    def _():
        for h in range(N_HEADS):
            for c in copies(b, h, buf):
                c.start()

    for h in range(N_HEADS):
        for c in copies(b, h, buf):
            c.wait()

    def head_attention(h):
        slot0 = (buf * N_HEADS + h) * per_head
        c_far = tab_ref[h, N_BUCKETS - 1] * LOG2E
        kn = qkv_ref[0, N_HEADS + h]
        vn = qkv_ref[0, 2 * N_HEADS + h]
        out_lane = lax.broadcasted_iota(jnp.int32, (HEAD_DIM, LANES), 1)
        out = jnp.zeros((HEAD_DIM, LANES), F32)
        for t in range(t_new):
            qt = qkv_ref[0, h, :, t:t + 1]
            s_own = jnp.sum(kn * qt, axis=0, keepdims=True) + bown_ref[h, t:t + 1, 0:t_new]
            scores = []
            for n in range(n_sel):
                j = block_of(b, h, t, n)
                s = jnp.sum(kbuf[slot0 + t * n_sel + n] * qt, axis=0, keepdims=True)
                scores.append(s + jnp.where(j == nblk - 1, badj_ref[h, t:t + 1, :], c_far))
            m = jnp.max(s_own, axis=-1, keepdims=True)
            for s in scores:
                m = jnp.maximum(m, jnp.max(s, axis=-1, keepdims=True))
            p_own = jnp.exp2(s_own - m)
            l = jnp.sum(p_own, axis=-1, keepdims=True)
            acc = jnp.sum(vn * p_own, axis=-1, keepdims=True)
            pv = None
            for n, s in enumerate(scores):
                p = jnp.exp2(s - m)
                l = l + jnp.sum(p, axis=-1, keepdims=True)
                term = vbuf[slot0 + t * n_sel + n] * p
                pv = term if pv is None else pv + term
            acc = acc + jnp.sum(pv, axis=-1, keepdims=True)
            out = jnp.where(out_lane == t, acc / l, out)
        o_ref[0, h] = out

    nxt = jnp.minimum(b + 1, n_seq - 1)
    for h in range(N_HEADS):
        for c in copies(nxt, h, 1 - buf):
            c.start()
        head_attention(h)

    @pl.when(b == n_seq - 1)
    def _():
        for h in range(N_HEADS):
            for c in copies(nxt, h, 1 - buf):
                c.wait()


def _attn_sample(page_table, sel_flat, tab_t, qkv_t, badj, bown, cache_kt, cache_vt, nblk):
    db, nh3, dh, t_new = qkv_t.shape
    nh = nh3 // 3
    n_slots = 2 * nh * t_new * MOBA_TOPK
    grid_spec = pltpu.PrefetchScalarGridSpec(
        num_scalar_prefetch=2,
        grid=(db,),
        in_specs=[pl.BlockSpec(memory_space=pltpu.SMEM),
                  pl.BlockSpec((1, nh3, dh, t_new), lambda b, pt, sel: (b, 0, 0, 0)),
                  pl.BlockSpec(badj.shape, lambda b, pt, sel: (0, 0, 0)),
                  pl.BlockSpec(bown.shape, lambda b, pt, sel: (0, 0, 0)),
                  pl.BlockSpec(memory_space=pl.ANY), pl.BlockSpec(memory_space=pl.ANY)],
        out_specs=pl.BlockSpec((1, nh, dh, LANES), lambda b, pt, sel: (b, 0, 0, 0)),
        scratch_shapes=[pltpu.VMEM((n_slots, dh, MOBA_BLOCK), F32),
                        pltpu.VMEM((n_slots, dh, MOBA_BLOCK), F32),
                        pltpu.SemaphoreType.DMA((2, 2))],
    )
    return pl.pallas_call(
        functools.partial(_attn_sample_kernel, nblk=nblk, t_new=t_new),
        grid_spec=grid_spec,
        out_shape=jax.ShapeDtypeStruct((db, nh, dh, LANES), F32),
        compiler_params=pltpu.CompilerParams(dimension_semantics=("arbitrary",),
                                             vmem_limit_bytes=VMEM_LIMIT),
        name="moba_sample",
    )(page_table, sel_flat, tab_t, qkv_t, badj, bown, cache_kt, cache_vt)


def kernel(x_prompt, x_sample, cache_k, cache_v, state_pool, page_table, c_prompt, c_sample, w_ada, b_ada, norm_mix, w_in, rel_bias, w_pool, pool_scale, w_out, norm_mlp, w_up, w_down, norm_final):
    assert w_ada.shape[0] == 1, "single-layer decoder"
    b, s, d = x_prompt.shape
    db, t_new, _ = x_sample.shape
    n_pages = page_table.shape[1]
    past = n_pages * PAGE_SIZE
    assert past % MOBA_BLOCK == 0 and past // MOBA_BLOCK >= MOBA_TOPK
    nblk = past // MOBA_BLOCK

    w_in_b = w_in[0].astype(BF16)
    w_out_b = w_out[0].astype(BF16)
    w_up_b = w_up[0].astype(BF16)
    w_down_b = w_down[0].astype(BF16)
    w_pool_b = w_pool[0].astype(BF16)
    g_final = norm_final.reshape(1, d)
    tab_t = rel_bias.T

    mods = _ada(jnp.concatenate([c_prompt, c_sample], axis=0), w_ada, b_ada[0:1])
    mods_p = mods[:b].reshape(b, 1, N_MOD * d)
    mods_s = jnp.repeat(mods[b:], t_new, axis=0).reshape(1, db * t_new, N_MOD * d)
    bias_own, bias_adj = _bias_tiles(tab_t)

    cache_kt = cache_k.transpose(0, 1, 3, 4, 2)
    cache_vt = cache_v.transpose(0, 1, 3, 4, 2)

    q_p, k_p, v_p, u_p = _inproj(x_prompt, mods_p, norm_mix, w_in_b)
    attn_p = _attn_prompt(tab_t, q_p, k_p, v_p, bias_own, bias_adj)
    y_p, kmean = _out(x_prompt, attn_p, u_p, mods_p, norm_mlp, g_final, w_out_b, w_up_b, w_down_b,
                      page_table, pool_params=(w_pool_b, pool_scale), cache_kt=cache_kt,
                      tm=_row_tile(s, (OUT_TILE_WITH_PAGES, 128, 64, 32, 16)))

    xs = x_sample.reshape(1, db * t_new, d)
    q_s, k_s, v_s, u_s = _inproj(xs, mods_s, norm_mix, w_in_b)
    head_of_lane = jnp.arange(ATTN_WIDTH) // HEAD_DIM
    q_bt = q_s.reshape(db, 1, t_new, ATTN_WIDTH)
    q_bd = jnp.where(head_of_lane[None, None, None, :] == jnp.arange(N_HEADS)[None, :, None, None],
                     q_bt, jnp.zeros_like(q_bt)).reshape(db, N_HEADS * t_new, ATTN_WIDTH)
    sel = _select(q_bd, kmean.reshape(db, nblk, ATTN_WIDTH))[:, :, :MOBA_TOPK]
    assert t_new <= 8
    qkv_t = jnp.concatenate([q_s.astype(F32), k_s, v_s], axis=-1)
    qkv_t = qkv_t.reshape(db, t_new, 3 * N_HEADS, HEAD_DIM).transpose(0, 2, 3, 1)
    attn_s = _attn_sample(page_table, sel.reshape(-1), tab_t, qkv_t,
                          bias_adj[:, :, :8].transpose(0, 2, 1), bias_own[:, :LANES, :8].transpose(0, 2, 1),
                          cache_kt, cache_vt, nblk)
    attn_s = attn_s[..., :t_new].transpose(0, 3, 1, 2).reshape(1, db * t_new, ATTN_WIDTH).astype(BF16)

    u_full = jnp.concatenate([state_pool[0], u_s.reshape(db, t_new, POOL_WIDTH)], axis=1)
    ext_rows = POOL_HALO + 8
    u_ext = jnp.pad(u_full, ((0, 0), (1, ext_rows - 1 - u_full.shape[1]), (0, 0)))
    pool_ext = _pool(u_ext.reshape(1, db * ext_rows, POOL_WIDTH), w_pool_b, pool_scale, first_pos=past)
    pool_s = pool_ext.reshape(db, ext_rows, POOL_WIDTH)[:, POOL_HALO:POOL_HALO + t_new]
    pool_s = pool_s.reshape(1, db * t_new, POOL_WIDTH)
    y_s = _out(xs, attn_s, pool_s, mods_s, norm_mlp, g_final, w_out_b, w_up_b, w_down_b, page_table)

    heads = lambda z, n, l: z.reshape(1, n, l, N_HEADS, HEAD_DIM)
    return (y_p, y_s.reshape(db, t_new, d),
            heads(k_p, b, s), heads(v_p, b, s), u_p[:, -POOL_STATE:][None],
            heads(k_s, db, t_new), heads(v_s, db, t_new), u_full[:, -POOL_STATE:][None])
```

```python
import functools
import math

import numpy as np
import jax
import jax.numpy as jnp
from jax import lax
from jax.experimental import pallas as pl
from jax.experimental.pallas import tpu as pltpu

HEAD_DIM = 64
N_HEADS = 8
ATTN_WIDTH = N_HEADS * HEAD_DIM
POOL_WINDOWS = (2, 4, 8, 16)
POOL_GROUP = 128
POOL_WIDTH = POOL_GROUP * len(POOL_WINDOWS)
POOL_STATE = max(POOL_WINDOWS) - 1
POOL_HALO = POOL_STATE + 1
MOBA_BLOCK = 256
MOBA_TOPK = 3
N_BUCKETS = 32
MAX_DISTANCE = 128
PAGE_SIZE = 128
PAGES_PER_BLOCK = MOBA_BLOCK // PAGE_SIZE
N_MOD = 6
EPS = 1e-6
NEG = -1e30
SCALE = HEAD_DIM ** -0.5
LOG2E = math.log2(math.e)
Q_SCALE = SCALE * LOG2E
LANES = 128
BF16_SUBLANES = 16
VT_ROWS = HEAD_DIM + BF16_SUBLANES
FF_CHUNK = 1024
SELECT_SEQS_PER_STEP = 8
FAR_GROUP = 2
OUT_TILE_WITH_PAGES = 256
VMEM_LIMIT = 56 * 1024 * 1024

BF16 = jnp.bfloat16
F32 = jnp.float32


def _nt_dot(a, b):
    return lax.dot_general(a, b, (((1,), (1,)), ((), ())), preferred_element_type=F32)


def _dot(a, b):
    return jnp.dot(a, b, preferred_element_type=F32)


def _row_tile(n, candidates=(512, 256, 128, 64, 32, 16, 8)):
    for c in candidates:
        if n % c == 0:
            return c
    raise ValueError(f"row count {n} is not a multiple of 8")


def _t5_bucket_np(rel):
    n = np.maximum(rel, 0)
    max_exact = N_BUCKETS // 2
    nf = np.maximum(n, max_exact).astype(np.float32)
    large = max_exact + (np.log(nf / np.float32(max_exact)) / np.float32(math.log(MAX_DISTANCE / max_exact))
                         * np.float32(N_BUCKETS - max_exact)).astype(np.int32)
    large = np.minimum(large, N_BUCKETS - 1)
    return np.where(n < max_exact, n, large).astype(np.int32)


def _rms(x, g):
    return x * lax.rsqrt(jnp.mean(x * x, axis=-1, keepdims=True) + EPS) * g


def _ada_kernel(c_ref, w_ref, b_ref, o_ref):
    c = c_ref[...]
    s = c / (1.0 + jnp.exp(-c))
    o_ref[...] = _dot(s.astype(BF16), w_ref[0].astype(BF16)) + b_ref[...]


def _ada(c_all, w_ada, b_ada):
    n, d = c_all.shape
    width = w_ada.shape[-1]
    tn = 1024
    return pl.pallas_call(
        _ada_kernel,
        grid=(width // tn,),
        in_specs=[pl.BlockSpec((n, d), lambda j: (0, 0)),
                  pl.BlockSpec((1, d, tn), lambda j: (0, 0, j)),
                  pl.BlockSpec((1, tn), lambda j: (0, j))],
        out_specs=pl.BlockSpec((n, tn), lambda j: (0, j)),
        out_shape=jax.ShapeDtypeStruct((n, width), F32),
        compiler_params=pltpu.CompilerParams(dimension_semantics=("arbitrary",), vmem_limit_bytes=VMEM_LIMIT),
        name="ada_mod",
    )(c_all, w_ada, b_ada)


def _inproj_kernel(x_ref, mods_ref, g_ref, w_ref, q_ref, k_ref, v_ref, u_ref):
    d = x_ref.shape[-1]
    x = x_ref[0]
    shift = mods_ref[0, :, 0:d]
    scale = mods_ref[0, :, d:2 * d]
    h = _rms(x, g_ref[...]) * (1.0 + scale) + shift
    r = _dot(h.astype(BF16), w_ref[...])
    a = ATTN_WIDTH
    q_ref[0] = (r[:, 0:a] * Q_SCALE).astype(BF16)
    k_ref[0] = r[:, a:2 * a]
    v_ref[0] = r[:, 2 * a:3 * a]
    u_ref[0] = r[:, 3 * a:]


def _inproj(x, mods, g, w_in_b):
    nb, s, d = x.shape
    r = mods.shape[1]
    ts = _row_tile(s)
    width = w_in_b.shape[1]
    row_spec = lambda w: pl.BlockSpec((1, ts, w), lambda b, i: (b, i, 0))
    mods_spec = (pl.BlockSpec((1, 1, N_MOD * d), lambda b, i: (b, 0, 0)) if r == 1
                 else pl.BlockSpec((1, ts, N_MOD * d), lambda b, i: (b, i, 0)))
    return pl.pallas_call(
        _inproj_kernel,
        grid=(nb, s // ts),
        in_specs=[row_spec(d), mods_spec,
                  pl.BlockSpec((1, d), lambda b, i: (0, 0)),
                  pl.BlockSpec((d, width), lambda b, i: (0, 0))],
        out_specs=[row_spec(ATTN_WIDTH), row_spec(ATTN_WIDTH), row_spec(ATTN_WIDTH), row_spec(POOL_WIDTH)],
        out_shape=[jax.ShapeDtypeStruct((nb, s, ATTN_WIDTH), BF16),
                   jax.ShapeDtypeStruct((nb, s, ATTN_WIDTH), F32),
                   jax.ShapeDtypeStruct((nb, s, ATTN_WIDTH), F32),
                   jax.ShapeDtypeStruct((nb, s, POOL_WIDTH), F32)],
        compiler_params=pltpu.CompilerParams(dimension_semantics=("arbitrary", "arbitrary"),
                                             vmem_limit_bytes=VMEM_LIMIT),
        name="in_proj",
    )(x, mods, g, w_in_b)


def _bias_kernel(tab_ref, idx_own_ref, idx_adj_ref, own_ref, adj_ref):
    h = pl.program_id(0)
    io = idx_own_ref[...]
    ia = idx_adj_ref[...]
    bo = jnp.zeros(io.shape, F32)
    ba = jnp.zeros(ia.shape, F32)
    for b in range(N_BUCKETS):
        t = tab_ref[h, b]
        bo = jnp.where(io == b, t, bo)
        ba = jnp.where(ia == b, t, ba)
    own_ref[0] = jnp.where(io < 0, NEG, bo * LOG2E)
    adj_ref[0] = ba * LOG2E


def _bias_tiles(tab_t):
    blk = MOBA_BLOCK
    r = np.arange(blk)[None, :]
    c = np.arange(blk)[:, None]
    idx_own = np.where(r >= c, _t5_bucket_np(r - c), -1).astype(np.int32)
    idx_adj = _t5_bucket_np(blk + r - c)
    tile = pl.BlockSpec((blk, blk), lambda h: (0, 0))
    out = pl.BlockSpec((1, blk, blk), lambda h: (h, 0, 0))
    return pl.pallas_call(
        _bias_kernel,
        grid=(N_HEADS,),
        in_specs=[pl.BlockSpec(memory_space=pltpu.SMEM), tile, tile],
        out_specs=[out, out],
        out_shape=[jax.ShapeDtypeStruct((N_HEADS, blk, blk), F32)] * 2,
        compiler_params=pltpu.CompilerParams(dimension_semantics=("arbitrary",)),
        name="rel_bias_tiles",
    )(tab_t, jnp.asarray(idx_own), jnp.asarray(idx_adj))


def _attn_prompt_kernel(tab_ref, q_ref, k_ref, v_ref, bown_ref, badj_ref, o_ref,
                        kb_ref, vt_ref, km_ref, qm_ref, madd_ref, m_ref, acc_ref, s_ref, *, nb):
    i = pl.program_id(1)
    blk = MOBA_BLOCK
    nbp = km_ref.shape[0]

    @pl.when(i == 0)
    def _():
        kb_ref[...] = k_ref[0].astype(BF16)
        ones = jnp.ones((VT_ROWS - HEAD_DIM, blk), BF16)
        for j in range(nb):
            vt = v_ref[0, j * blk:(j + 1) * blk, :].T.astype(BF16)
            for h in range(N_HEADS):
                vt_ref[j, h, 0:HEAD_DIM, :] = vt[h * HEAD_DIM:(h + 1) * HEAD_DIM, :]
                vt_ref[j, h, HEAD_DIM:, :] = ones
        means = [jnp.mean(k_ref[0, j * blk:(j + 1) * blk, :], axis=0, keepdims=True) for j in range(nb)]
        means += [jnp.zeros_like(means[0])] * (nbp - nb)
        km_ref[...] = jnp.concatenate(means, axis=0)

    lane = lax.broadcasted_iota(jnp.int32, (blk, LANES), 1)
    blk_row = lax.broadcasted_iota(jnp.int32, (nbp, blk), 0)
    valid = blk_row < i
    jm1 = jnp.maximum(i - 1, 0)
    own0 = pl.multiple_of(i * blk, blk)
    adj0 = pl.multiple_of(jm1 * blk, blk)
    heads_per_vreg = LANES // HEAD_DIM

    def lanes_of(h):
        p = h // heads_per_vreg
        return slice(p * LANES, (p + 1) * LANES)

    for h in range(N_HEADS):
        hh = h % heads_per_vreg
        q2 = q_ref[0, :, lanes_of(h)]
        hmask = (lane >= hh * HEAD_DIM) & (lane < (hh + 1) * HEAD_DIM)
        qm = jnp.where(hmask, q2, jnp.zeros_like(q2))
        qm_ref[h] = qm
        gate = _nt_dot(km_ref[:, lanes_of(h)].astype(BF16), qm)
        for j in range(nb):
            gj = gate[j:j + 1, :]
            beats = ((gate > gj) | ((gate == gj) & (blk_row < j))) & valid
            cnt = jnp.sum(beats.astype(F32), axis=0, keepdims=True)
            cnt = cnt + jnp.where(j < i, 0.0, float(nb))
            madd_ref[h, j:j + 1, :] = jnp.where(cnt < MOBA_TOPK, 0.0, NEG)

    def key_blocks(blocks):
        for n, (key0, _, tile_bias, _, _) in enumerate(blocks):
            for h in range(N_HEADS):
                s = _nt_dot(kb_ref[pl.ds(key0, blk), lanes_of(h)], qm_ref[h])
                s_ref[n, h] = s if tile_bias is None else s + tile_bias(h)
        for n, (_, blk_idx, _, query_bias, first) in enumerate(blocks):
            for h in range(N_HEADS):
                s = s_ref[n, h]
                m_blk = jnp.max(s, axis=0, keepdims=True)
                if query_bias is not None:
                    qb = query_bias(h)
                    m_blk = m_blk + qb
                if first:
                    m_new = m_blk
                else:
                    m_old = m_ref[h]
                    m_new = jnp.maximum(m_old, m_blk)
                ref_row = m_new if query_bias is None else m_new - qb
                p = jnp.exp2(s - ref_row).astype(BF16)
                pv = _dot(vt_ref[blk_idx, h], p)
                acc_ref[h] = pv if first else jnp.exp2(m_old - m_new) * acc_ref[h] + pv
                m_ref[h] = m_new

    def far_block(j):
        return (pl.multiple_of(j * blk, blk), j, None,
                lambda h: madd_ref[h, pl.ds(j, 1), :] + tab_ref[h, N_BUCKETS - 1] * LOG2E, False)

    key_blocks([(own0, i, lambda h: bown_ref[h], None, True),
                (adj0, jm1, lambda h: badj_ref[h], lambda h: madd_ref[h, pl.ds(jm1, 1), :], False)])

    def far_pair(jj, carry):
        key_blocks([far_block(FAR_GROUP * jj + n) for n in range(FAR_GROUP)])
        return carry

    lax.fori_loop(0, jm1 // FAR_GROUP, far_pair, 0)
    for n in range(1, FAR_GROUP):
        @pl.when(jm1 % FAR_GROUP >= n)
        def _():
            key_blocks([far_block(jm1 - n)])

    for p in range(ATTN_WIDTH // LANES):
        halves = []
        for h in range(p * heads_per_vreg, (p + 1) * heads_per_vreg):
            acc = acc_ref[h]
            halves.append(acc[0:HEAD_DIM, :] / acc[HEAD_DIM:HEAD_DIM + 1, :])
        o_ref[0, :, p * LANES:(p + 1) * LANES] = jnp.concatenate(halves, axis=0).T.astype(BF16)


def _attn_prompt(tab_t, q, k, v, bias_own_t, bias_adj_t):
    b, s, a = q.shape
    blk = MOBA_BLOCK
    nb = s // blk
    nbp = 8
    assert s % blk == 0 and nb <= nbp
    full = pl.BlockSpec((1, s, a), lambda bi, i: (bi, 0, 0))
    tile = pl.BlockSpec((1, blk, a), lambda bi, i: (bi, i, 0))
    bias = pl.BlockSpec((N_HEADS, blk, blk), lambda bi, i: (0, 0, 0))
    return pl.pallas_call(
        functools.partial(_attn_prompt_kernel, nb=nb),
        grid=(b, nb),
        in_specs=[pl.BlockSpec(memory_space=pltpu.SMEM), tile, full, full, bias, bias],
        out_specs=tile,
        out_shape=jax.ShapeDtypeStruct((b, s, a), BF16),
        scratch_shapes=[pltpu.VMEM((s, a), BF16), pltpu.VMEM((nb, N_HEADS, VT_ROWS, blk), BF16),
                        pltpu.VMEM((nbp, a), F32), pltpu.VMEM((N_HEADS, blk, LANES), BF16),
                        pltpu.VMEM((N_HEADS, nbp, blk), F32), pltpu.VMEM((N_HEADS, 1, blk), F32),
                        pltpu.VMEM((N_HEADS, VT_ROWS, blk), F32),
                        pltpu.VMEM((FAR_GROUP, N_HEADS, blk, blk), F32)],
        compiler_params=pltpu.CompilerParams(dimension_semantics=("arbitrary", "arbitrary"),
                                             vmem_limit_bytes=VMEM_LIMIT),
        name="moba_prompt",
    )(tab_t, q, k, v, bias_own_t, bias_adj_t)


def _pool_rows(u, hist, w_ref, scale_ref, first_row_pos):
    tp = u.shape[0]
    ext = jnp.concatenate([hist, u], axis=0)
    pos = first_row_pos + lax.broadcasted_iota(jnp.int32, (tp, 1), 0)
    outs = []
    for g, w in enumerate(POOL_WINDOWS):
        gs = slice(g * POOL_GROUP, (g + 1) * POOL_GROUP)
        s = ext[:, gs]
        shift = 1
        while shift < w:
            s = s + pltpu.roll(s, shift, 0)
            shift *= 2
        cnt = jnp.minimum(pos + 1, w).astype(F32)
        d = s[POOL_HALO:, :] / cnt - u[:, gs]
        outs.append((_dot(d.astype(BF16), w_ref[g]) * scale_ref[:, gs]).astype(BF16))
    return jnp.concatenate(outs, axis=1)


def _pool_kernel(u_ref, hist_ref, w_ref, scale_ref, o_ref, *, first_pos):
    i = pl.program_id(1)
    hist = jnp.where(i > 0, hist_ref[0], 0.0)
    o_ref[0] = _pool_rows(u_ref[0], hist, w_ref, scale_ref, first_pos + i * u_ref.shape[1])


def _halo_spec(tile_rows, width):
    per_tile = tile_rows // POOL_HALO
    return pl.BlockSpec((1, POOL_HALO, width), lambda b, i, *_: (b, jnp.maximum(i * per_tile - 1, 0), 0))


def _pool(u, w_pool_b, pool_scale, first_pos):
    nb, s, c = u.shape
    tp = _row_tile(s, (512, 256, 128, 64, 32, 16))
    return pl.pallas_call(
        functools.partial(_pool_kernel, first_pos=first_pos),
        grid=(nb, s // tp),
        in_specs=[pl.BlockSpec((1, tp, c), lambda b, i: (b, i, 0)),
                  _halo_spec(tp, c),
                  pl.BlockSpec(w_pool_b.shape, lambda b, i: (0, 0, 0)),
                  pl.BlockSpec((1, c), lambda b, i: (0, 0))],
        out_specs=pl.BlockSpec((1, tp, c), lambda b, i: (b, i, 0)),
        out_shape=jax.ShapeDtypeStruct((nb, s, c), BF16),
        compiler_params=pltpu.CompilerParams(dimension_semantics=("arbitrary", "arbitrary"),
                                             vmem_limit_bytes=VMEM_LIMIT),
        name="multi_pool",
    )(u, u, w_pool_b, pool_scale)


def _out_kernel(pt_ref, x_ref, attn_ref, pool_ref, mods_ref, gm_ref, gf_ref, wo_ref, wu_ref, wd_ref, *refs,
                fused_pool, rider):
    if fused_pool:
        hist_ref, wp_ref, ps_ref = refs[:3]
        refs = refs[3:]
    n_chunks = wu_ref.shape[1] // FF_CHUNK
    if rider is None:
        (y_ref,) = refs
        n_pages = 0
    else:
        ck_ref, y_ref, kmean_ref, pbuf, psem = refs
        n_pages, groups_per_seq, n_groups = rider
        step = pl.program_id(0) * pl.num_programs(1) + pl.program_id(1)
        last = pl.num_programs(0) * pl.num_programs(1) - 1
        slot = step % 2
        nxt = jnp.minimum(step + 1, last)

        def page_copies(st, sl, first, count):
            g = jnp.minimum(st, n_groups - 1)
            seq, part = g // groups_per_seq, g % groups_per_seq
            return [pltpu.make_async_copy(ck_ref.at[0, pt_ref[seq, part * n_pages + n]], pbuf.at[sl, n],
                                          psem.at[sl]) for n in range(first, first + count)]

        @pl.when(step == 0)
        def _():
            for c in page_copies(step, slot, 0, n_pages):
                c.start()

        for c in page_copies(nxt, 1 - slot, 0, n_pages):
            c.start()
        for c in page_copies(step, slot, 0, n_pages):
            c.wait()
    n_blocks = n_pages // PAGES_PER_BLOCK
    if fused_pool:
        i = pl.program_id(1)
        hist = jnp.where(i > 0, hist_ref[0], 0.0)
        pool = _pool_rows(pool_ref[0], hist, wp_ref, ps_ref, i * x_ref.shape[1])
    else:
        pool = pool_ref[0]
    d = x_ref.shape[-1]
    a = ATTN_WIDTH
    x = x_ref[0]
    g1 = mods_ref[0, :, 2 * d:3 * d]
    sh2 = mods_ref[0, :, 3 * d:4 * d]
    sc2 = mods_ref[0, :, 4 * d:5 * d]
    g2 = mods_ref[0, :, 5 * d:6 * d]
    mix = _dot(attn_ref[0], wo_ref[0:a, :]) + _dot(pool, wo_ref[a:, :])
    x1 = x + g1 * mix
    hb = (_rms(x1, gm_ref[...]) * (1.0 + sc2) + sh2).astype(BF16)
    acc = jnp.zeros(x.shape, F32)
    for c in range(n_chunks):
        cs = slice(c * FF_CHUNK, (c + 1) * FF_CHUNK)
        t = jnp.maximum(_dot(hb, wu_ref[:, cs]), 0.0)
        acc = acc + _dot((t * t).astype(BF16), wd_ref[cs, :])
        if rider is not None:
            for blk in range(c * n_blocks // n_chunks, (c + 1) * n_blocks // n_chunks):
                ksum = pbuf[slot, blk * PAGES_PER_BLOCK]
                for pg in range(1, PAGES_PER_BLOCK):
                    ksum = ksum + pbuf[slot, blk * PAGES_PER_BLOCK + pg]
                kmean_ref[0, blk] = jnp.sum(ksum, axis=-1) * (1.0 / MOBA_BLOCK)
    x2 = x1 + g2 * acc
    y_ref[0] = _rms(x2, gf_ref[...])
    if rider is not None:
        @pl.when(step == last)
        def _():
            for cp in page_copies(nxt, 1 - slot, 0, n_pages):
                cp.wait()


def _out(x, attn, pool, mods, g_mlp, g_final, w_out_b, w_up_b, w_down_b, page_table, pool_params=None,
         cache_kt=None, tm=None):
    nb, s, d = x.shape
    r = mods.shape[1]
    tm = tm or _row_tile(s)
    n_tiles = s // tm
    row_spec = lambda w: pl.BlockSpec((1, tm, w), lambda b, i, pt: (b, i, 0))
    mods_spec = (pl.BlockSpec((1, 1, N_MOD * d), lambda b, i, pt: (b, 0, 0)) if r == 1
                 else pl.BlockSpec((1, tm, N_MOD * d), lambda b, i, pt: (b, i, 0)))
    resident = lambda w: pl.BlockSpec(w.shape, lambda b, i, pt: (0, 0), pipeline_mode=pl.Buffered(1))
    vec = pl.BlockSpec((1, d), lambda b, i, pt: (0, 0))
    out_specs, out_shape = [row_spec(d)], [jax.ShapeDtypeStruct((nb, s, d), F32)]
    pool_specs, pool_args, cache_specs, cache_args, scratch, rider = [], [], [], [], [], None
    if pool_params is not None:
        w_pool_b, pool_scale = pool_params
        pool_specs = [_halo_spec(tm, POOL_WIDTH), pl.BlockSpec(w_pool_b.shape, lambda b, i, pt: (0, 0, 0)),
                      pl.BlockSpec((1, POOL_WIDTH), lambda b, i, pt: (0, 0))]
        pool_args = [pool, w_pool_b, pool_scale]
    if cache_kt is not None:
        db, n_pages = page_table.shape
        nblk = n_pages // PAGES_PER_BLOCK
        need = -(-db * nblk // (nb * n_tiles))
        group = min(g for g in range(1, nblk + 1) if nblk % g == 0 and g >= need)
        groups_per_seq = nblk // group
        n_groups = db * groups_per_seq
        pps = group * PAGES_PER_BLOCK
        rider = (pps, groups_per_seq, n_groups)

        def group_of(b, i):
            g = jnp.minimum(b * n_tiles + i, n_groups - 1)
            return g // groups_per_seq, g % groups_per_seq

        cache_specs, cache_args = [pl.BlockSpec(memory_space=pl.ANY)], [cache_kt]
        scratch = [pltpu.VMEM((2, pps) + cache_kt.shape[2:], F32), pltpu.SemaphoreType.DMA((2,))]
        out_specs.append(pl.BlockSpec((1, group, N_HEADS, HEAD_DIM), lambda b, i, pt: group_of(b, i) + (0, 0)))
        out_shape.append(jax.ShapeDtypeStruct((db, nblk, N_HEADS, HEAD_DIM), F32))
    grid_spec = pltpu.PrefetchScalarGridSpec(
        num_scalar_prefetch=1,
        grid=(nb, n_tiles),
        in_specs=[row_spec(d), row_spec(ATTN_WIDTH), row_spec(POOL_WIDTH), mods_spec, vec, vec,
                  resident(w_out_b), resident(w_up_b), resident(w_down_b)] + pool_specs + cache_specs,
        out_specs=out_specs,
        scratch_shapes=scratch,
    )
    out = pl.pallas_call(
        functools.partial(_out_kernel, fused_pool=pool_params is not None, rider=rider),
        grid_spec=grid_spec,
        out_shape=out_shape,
        compiler_params=pltpu.CompilerParams(dimension_semantics=("arbitrary", "arbitrary"),
                                             vmem_limit_bytes=VMEM_LIMIT),
        name="out_mlp",
    )(page_table, x, attn, pool, mods, g_mlp, g_final, w_out_b, w_up_b, w_down_b, *pool_args, *cache_args)
    return out if cache_kt is not None else out[0]


def _select_kernel(q_ref, km_ref, sel_ref):
    for sq in range(q_ref.shape[0]):
        g = lax.dot_general(q_ref[sq].astype(F32), km_ref[sq], (((1,), (1,)), ((), ())),
                            precision=lax.Precision.HIGHEST, preferred_element_type=F32)
        rows, nblk = g.shape
        lane = lax.broadcasted_iota(jnp.int32, g.shape, 1)
        out_lane = lax.broadcasted_iota(jnp.int32, (rows, LANES), 1)
        out = jnp.zeros((rows, LANES), jnp.int32)
        for n in range(MOBA_TOPK):
            mx = jnp.max(g, axis=-1, keepdims=True)
            idx = jnp.min(jnp.where(g == mx, lane, nblk), axis=-1, keepdims=True)
            out = jnp.where(out_lane == n, idx, out)
            g = jnp.where(lane == idx, -jnp.inf, g)
        sel_ref[sq] = out


def _select(q_bd, kmean2d):
    db, rows, a = q_bd.shape
    nblk = kmean2d.shape[1]
    seqs = math.gcd(db, SELECT_SEQS_PER_STEP)
    return pl.pallas_call(
        _select_kernel,
        grid=(db // seqs,),
        in_specs=[pl.BlockSpec((seqs, rows, a), lambda b: (b, 0, 0)),
                  pl.BlockSpec((seqs, nblk, a), lambda b: (b, 0, 0))],
        out_specs=pl.BlockSpec((seqs, rows, LANES), lambda b: (b, 0, 0)),
        out_shape=jax.ShapeDtypeStruct((db, rows, LANES), jnp.int32),
        compiler_params=pltpu.CompilerParams(dimension_semantics=("arbitrary",)),
        name="moba_select",
    )(q_bd, kmean2d)


def _attn_sample_kernel(pt_ref, sel_ref, tab_ref, qkv_ref, badj_ref, bown_ref, ck_ref, cv_ref,
                        o_ref, kbuf, vbuf, sem, *, nblk, t_new):
    b = pl.program_id(0)
    n_seq = pl.num_programs(0)
    n_sel = MOBA_TOPK
    per_head = t_new * n_sel
    buf = b % 2

    def block_of(sb, h, t, n):
        return sel_ref[((sb * N_HEADS + h) * t_new + t) * n_sel + n]

    def copies(sb, h, bf):
        out = []
        for t in range(t_new):
            for n in range(n_sel):
                j = block_of(sb, h, t, n)
                slot = (bf * N_HEADS + h) * per_head + t * n_sel + n
                for pg in range(PAGES_PER_BLOCK):
                    page = pt_ref[sb, j * PAGES_PER_BLOCK + pg]
                    keys = pl.ds(pg * PAGE_SIZE, PAGE_SIZE)
                    out.append(pltpu.make_async_copy(ck_ref.at[0, page, h], kbuf.at[slot, :, keys], sem.at[0, bf]))
                    out.append(pltpu.make_async_copy(cv_ref.at[0, page, h], vbuf.at[slot, :, keys], sem.at[1, bf]))
        return out

    @pl.when(b == 0)
    def _():
        for h in range(N_HEADS):
            for c in copies(b, h, buf):
                c.start()

    for h in range(N_HEADS):
        for c in copies(b, h, buf):
            c.wait()

    def head_attention(h):
        slot0 = (buf * N_HEADS + h) * per_head
        c_far = tab_ref[h, N_BUCKETS - 1] * LOG2E
        kn = qkv_ref[0, N_HEADS + h]
        vn = qkv_ref[0, 2 * N_HEADS + h]
        out_lane = lax.broadcasted_iota(jnp.int32, (HEAD_DIM, LANES), 1)
        out = jnp.zeros((HEAD_DIM, LANES), F32)
        for t in range(t_new):
            qt = qkv_ref[0, h, :, t:t + 1]
            s_own = jnp.sum(kn * qt, axis=0, keepdims=True) + bown_ref[h, t:t + 1, 0:t_new]
            scores = []
            for n in range(n_sel):
                j = block_of(b, h, t, n)
                s = jnp.sum(kbuf[slot0 + t * n_sel + n] * qt, axis=0, keepdims=True)
                scores.append(s + jnp.where(j == nblk - 1, badj_ref[h, t:t + 1, :], c_far))
            m = jnp.max(s_own, axis=-1, keepdims=True)
            for s in scores:
                m = jnp.maximum(m, jnp.max(s, axis=-1, keepdims=True))
            p_own = jnp.exp2(s_own - m)
            l = jnp.sum(p_own, axis=-1, keepdims=True)
            acc = jnp.sum(vn * p_own, axis=-1, keepdims=True)
            pv = None
            for n, s in enumerate(scores):
                p = jnp.exp2(s - m)
                l = l + jnp.sum(p, axis=-1, keepdims=True)
                term = vbuf[slot0 + t * n_sel + n] * p
                pv = term if pv is None else pv + term
            acc = acc + jnp.sum(pv, axis=-1, keepdims=True)
            out = jnp.where(out_lane == t, acc / l, out)
        o_ref[0, h] = out

    nxt = jnp.minimum(b + 1, n_seq - 1)
    for h in range(N_HEADS):
        for c in copies(nxt, h, 1 - buf):
            c.start()
        head_attention(h)

    @pl.when(b == n_seq - 1)
    def _():
        for h in range(N_HEADS):
            for c in copies(nxt, h, 1 - buf):
                c.wait()


def _attn_sample(page_table, sel_flat, tab_t, qkv_t, badj, bown, cache_kt, cache_vt, nblk):
    db, nh3, dh, t_new = qkv_t.shape
    nh = nh3 // 3
    n_slots = 2 * nh * t_new * MOBA_TOPK
    grid_spec = pltpu.PrefetchScalarGridSpec(
        num_scalar_prefetch=2,
        grid=(db,),
        in_specs=[pl.BlockSpec(memory_space=pltpu.SMEM),
                  pl.BlockSpec((1, nh3, dh, t_new), lambda b, pt, sel: (b, 0, 0, 0)),
                  pl.BlockSpec(badj.shape, lambda b, pt, sel: (0, 0, 0)),
                  pl.BlockSpec(bown.shape, lambda b, pt, sel: (0, 0, 0)),
                  pl.BlockSpec(memory_space=pl.ANY), pl.BlockSpec(memory_space=pl.ANY)],
        out_specs=pl.BlockSpec((1, nh, dh, LANES), lambda b, pt, sel: (b, 0, 0, 0)),
        scratch_shapes=[pltpu.VMEM((n_slots, dh, MOBA_BLOCK), F32),
                        pltpu.VMEM((n_slots, dh, MOBA_BLOCK), F32),
                        pltpu.SemaphoreType.DMA((2, 2))],
    )
    return pl.pallas_call(
        functools.partial(_attn_sample_kernel, nblk=nblk, t_new=t_new),
        grid_spec=grid_spec,
        out_shape=jax.ShapeDtypeStruct((db, nh, dh, LANES), F32),
        compiler_params=pltpu.CompilerParams(dimension_semantics=("arbitrary",),
                                             vmem_limit_bytes=VMEM_LIMIT),
        name="moba_sample",
    )(page_table, sel_flat, tab_t, qkv_t, badj, bown, cache_kt, cache_vt)


def kernel(x_prompt, x_sample, cache_k, cache_v, state_pool, page_table, c_prompt, c_sample, w_ada, b_ada, norm_mix, w_in, rel_bias, w_pool, pool_scale, w_out, norm_mlp, w_up, w_down, norm_final):
    assert w_ada.shape[0] == 1, "single-layer decoder"
    b, s, d = x_prompt.shape
    db, t_new, _ = x_sample.shape
    n_pages = page_table.shape[1]
    past = n_pages * PAGE_SIZE
    assert past % MOBA_BLOCK == 0 and past // MOBA_BLOCK >= MOBA_TOPK
    nblk = past // MOBA_BLOCK

    w_in_b = w_in[0].astype(BF16)
    w_out_b = w_out[0].astype(BF16)
    w_up_b = w_up[0].astype(BF16)
    w_down_b = w_down[0].astype(BF16)
    w_pool_b = w_pool[0].astype(BF16)
    g_final = norm_final.reshape(1, d)
    tab_t = rel_bias.T

    mods = _ada(jnp.concatenate([c_prompt, c_sample], axis=0), w_ada, b_ada[0:1])
    mods_p = mods[:b].reshape(b, 1, N_MOD * d)
    mods_s = jnp.repeat(mods[b:], t_new, axis=0).reshape(1, db * t_new, N_MOD * d)
    bias_own, bias_adj = _bias_tiles(tab_t)

    cache_kt = cache_k.transpose(0, 1, 3, 4, 2)
    cache_vt = cache_v.transpose(0, 1, 3, 4, 2)

    q_p, k_p, v_p, u_p = _inproj(x_prompt, mods_p, norm_mix, w_in_b)
    attn_p = _attn_prompt(tab_t, q_p, k_p, v_p, bias_own, bias_adj)
    y_p, kmean = _out(x_prompt, attn_p, u_p, mods_p, norm_mlp, g_final, w_out_b, w_up_b, w_down_b,
                      page_table, pool_params=(w_pool_b, pool_scale), cache_kt=cache_kt,
                      tm=_row_tile(s, (OUT_TILE_WITH_PAGES, 128, 64, 32, 16)))

    xs = x_sample.reshape(1, db * t_new, d)
    q_s, k_s, v_s, u_s = _inproj(xs, mods_s, norm_mix, w_in_b)
    head_of_lane = jnp.arange(ATTN_WIDTH) // HEAD_DIM
    q_bt = q_s.reshape(db, 1, t_new, ATTN_WIDTH)
    q_bd = jnp.where(head_of_lane[None, None, None, :] == jnp.arange(N_HEADS)[None, :, None, None],
                     q_bt, jnp.zeros_like(q_bt)).reshape(db, N_HEADS * t_new, ATTN_WIDTH)
    sel = _select(q_bd, kmean.reshape(db, nblk, ATTN_WIDTH))[:, :, :MOBA_TOPK]
    assert t_new <= 8
    qkv_t = jnp.concatenate([q_s.astype(F32), k_s, v_s], axis=-1)
    qkv_t = qkv_t.reshape(db, t_new, 3 * N_HEADS, HEAD_DIM).transpose(0, 2, 3, 1)
    attn_s = _attn_sample(page_table, sel.reshape(-1), tab_t, qkv_t,
                          bias_adj[:, :, :8].transpose(0, 2, 1), bias_own[:, :LANES, :8].transpose(0, 2, 1),
                          cache_kt, cache_vt, nblk)
    attn_s = attn_s[..., :t_new].transpose(0, 3, 1, 2).reshape(1, db * t_new, ATTN_WIDTH).astype(BF16)

    u_full = jnp.concatenate([state_pool[0], u_s.reshape(db, t_new, POOL_WIDTH)], axis=1)
    ext_rows = POOL_HALO + 8
    u_ext = jnp.pad(u_full, ((0, 0), (1, ext_rows - 1 - u_full.shape[1]), (0, 0)))
    pool_ext = _pool(u_ext.reshape(1, db * ext_rows, POOL_WIDTH), w_pool_b, pool_scale, first_pos=past)
    pool_s = pool_ext.reshape(db, ext_rows, POOL_WIDTH)[:, POOL_HALO:POOL_HALO + t_new]
    pool_s = pool_s.reshape(1, db * t_new, POOL_WIDTH)
    y_s = _out(xs, attn_s, pool_s, mods_s, norm_mlp, g_final, w_out_b, w_up_b, w_down_b, page_table)

    heads = lambda z, n, l: z.reshape(1, n, l, N_HEADS, HEAD_DIM)
    return (y_p, y_s.reshape(db, t_new, d),
            heads(k_p, b, s), heads(v_p, b, s), u_p[:, -POOL_STATE:][None],
            heads(k_s, db, t_new), heads(v_s, db, t_new), u_full[:, -POOL_STATE:][None])
```

```python
import functools
import math

import numpy as np
import jax
import jax.numpy as jnp
from jax import lax
from jax.experimental import pallas as pl
from jax.experimental.pallas import tpu as pltpu

HEAD_DIM = 64
N_HEADS = 8
ATTN_WIDTH = N_HEADS * HEAD_DIM
POOL_WINDOWS = (2, 4, 8, 16)
POOL_GROUP = 128
POOL_WIDTH = POOL_GROUP * len(POOL_WINDOWS)
POOL_STATE = max(POOL_WINDOWS) - 1
POOL_HALO = POOL_STATE + 1
MOBA_BLOCK = 256
MOBA_TOPK = 3
N_BUCKETS = 32
MAX_DISTANCE = 128
PAGE_SIZE = 128
PAGES_PER_BLOCK = MOBA_BLOCK // PAGE_SIZE
N_MOD = 6
EPS = 1e-6
NEG = -1e30
SCALE = HEAD_DIM ** -0.5
LOG2E = math.log2(math.e)
Q_SCALE = SCALE * LOG2E
LANES = 128
BF16_SUBLANES = 16
VT_ROWS = HEAD_DIM + BF16_SUBLANES
FF_CHUNK = 1024
SAMPLE_ROWS = 8
SELECT_SEQS_PER_STEP = 8
FAR_GROUP = 2
OUT_TILE_WITH_PAGES = 256
VMEM_LIMIT = 56 * 1024 * 1024

BF16 = jnp.bfloat16
F32 = jnp.float32


def _nt_dot(a, b):
    return lax.dot_general(a, b, (((1,), (1,)), ((), ())), preferred_element_type=F32)


def _dot(a, b):
    return jnp.dot(a, b, preferred_element_type=F32)


def _row_tile(n, candidates=(512, 256, 128, 64, 32, 16, 8)):
    for c in candidates:
        if n % c == 0:
            return c
    raise ValueError(f"row count {n} is not a multiple of 8")


def _t5_bucket_np(rel):
    n = np.maximum(rel, 0)
    max_exact = N_BUCKETS // 2
    nf = np.maximum(n, max_exact).astype(np.float32)
    large = max_exact + (np.log(nf / np.float32(max_exact)) / np.float32(math.log(MAX_DISTANCE / max_exact))
                         * np.float32(N_BUCKETS - max_exact)).astype(np.int32)
    large = np.minimum(large, N_BUCKETS - 1)
    return np.where(n < max_exact, n, large).astype(np.int32)


def _rms(x, g):
    return x * lax.rsqrt(jnp.mean(x * x, axis=-1, keepdims=True) + EPS) * g


def _ada_kernel(c_ref, w_ref, b_ref, o_ref):
    c = c_ref[...]
    s = c / (1.0 + jnp.exp(-c))
    o_ref[...] = _dot(s.astype(BF16), w_ref[0].astype(BF16)) + b_ref[...]


def _ada(c_all, w_ada, b_ada):
    n, d = c_all.shape
    width = w_ada.shape[-1]
    tn = 1024
    return pl.pallas_call(
        _ada_kernel,
        grid=(width // tn,),
        in_specs=[pl.BlockSpec((n, d), lambda j: (0, 0)),
                  pl.BlockSpec((1, d, tn), lambda j: (0, 0, j)),
                  pl.BlockSpec((1, tn), lambda j: (0, j))],
        out_specs=pl.BlockSpec((n, tn), lambda j: (0, j)),
        out_shape=jax.ShapeDtypeStruct((n, width), F32),
        compiler_params=pltpu.CompilerParams(dimension_semantics=("arbitrary",), vmem_limit_bytes=VMEM_LIMIT),
        name="ada_mod",
    )(c_all, w_ada, b_ada)


def _inproj_kernel(x_ref, mods_ref, g_ref, w_ref, q_ref, k_ref, v_ref, u_ref):
    d = x_ref.shape[-1]
    x = x_ref[0]
    shift = mods_ref[0, :, 0:d]
    scale = mods_ref[0, :, d:2 * d]
    h = _rms(x, g_ref[...]) * (1.0 + scale) + shift
    r = _dot(h.astype(BF16), w_ref[...])
    a = ATTN_WIDTH
    q_ref[0] = (r[:, 0:a] * Q_SCALE).astype(BF16)
    k_ref[0] = r[:, a:2 * a]
    v_ref[0] = r[:, 2 * a:3 * a]
    u_ref[0] = r[:, 3 * a:]


def _inproj(x, mods, g, w_in_b):
    nb, s, d = x.shape
    r = mods.shape[1]
    ts = _row_tile(s)
    width = w_in_b.shape[1]
    row_spec = lambda w: pl.BlockSpec((1, ts, w), lambda b, i: (b, i, 0))
    mods_spec = (pl.BlockSpec((1, 1, N_MOD * d), lambda b, i: (b, 0, 0)) if r == 1
                 else pl.BlockSpec((1, ts, N_MOD * d), lambda b, i: (b, i, 0)))
    return pl.pallas_call(
        _inproj_kernel,
        grid=(nb, s // ts),
        in_specs=[row_spec(d), mods_spec,
                  pl.BlockSpec((1, d), lambda b, i: (0, 0)),
                  pl.BlockSpec((d, width), lambda b, i: (0, 0))],
        out_specs=[row_spec(ATTN_WIDTH), row_spec(ATTN_WIDTH), row_spec(ATTN_WIDTH), row_spec(POOL_WIDTH)],
        out_shape=[jax.ShapeDtypeStruct((nb, s, ATTN_WIDTH), BF16),
                   jax.ShapeDtypeStruct((nb, s, ATTN_WIDTH), F32),
                   jax.ShapeDtypeStruct((nb, s, ATTN_WIDTH), F32),
                   jax.ShapeDtypeStruct((nb, s, POOL_WIDTH), F32)],
        compiler_params=pltpu.CompilerParams(dimension_semantics=("arbitrary", "arbitrary"),
                                             vmem_limit_bytes=VMEM_LIMIT),
        name="in_proj",
    )(x, mods, g, w_in_b)


def _bias_kernel(tab_ref, idx_own_ref, idx_adj_ref, own_ref, adj_ref):
    h = pl.program_id(0)
    io = idx_own_ref[...]
    ia = idx_adj_ref[...]
    bo = jnp.zeros(io.shape, F32)
    ba = jnp.zeros(ia.shape, F32)
    for b in range(N_BUCKETS):
        t = tab_ref[h, b]
        bo = jnp.where(io == b, t, bo)
        ba = jnp.where(ia == b, t, ba)
    own_ref[0] = jnp.where(io < 0, NEG, bo * LOG2E)
    adj_ref[0] = ba * LOG2E


def _bias_tiles(tab_t):
    blk = MOBA_BLOCK
    r = np.arange(blk)[None, :]
    c = np.arange(blk)[:, None]
    idx_own = np.where(r >= c, _t5_bucket_np(r - c), -1).astype(np.int32)
    idx_adj = _t5_bucket_np(blk + r - c)
    tile = pl.BlockSpec((blk, blk), lambda h: (0, 0))
    out = pl.BlockSpec((1, blk, blk), lambda h: (h, 0, 0))
    return pl.pallas_call(
        _bias_kernel,
        grid=(N_HEADS,),
        in_specs=[pl.BlockSpec(memory_space=pltpu.SMEM), tile, tile],
        out_specs=[out, out],
        out_shape=[jax.ShapeDtypeStruct((N_HEADS, blk, blk), F32)] * 2,
        compiler_params=pltpu.CompilerParams(dimension_semantics=("arbitrary",)),
        name="rel_bias_tiles",
    )(tab_t, jnp.asarray(idx_own), jnp.asarray(idx_adj))


def _attn_prompt_kernel(tab_ref, q_ref, k_ref, v_ref, bown_ref, badj_ref, o_ref,
                        kb_ref, vt_ref, km_ref, qm_ref, madd_ref, m_ref, acc_ref, s_ref, *, nb):
    i = pl.program_id(1)
    blk = MOBA_BLOCK
    nbp = km_ref.shape[0]

    @pl.when(i == 0)
    def _():
        kb_ref[...] = k_ref[0].astype(BF16)
        ones = jnp.ones((VT_ROWS - HEAD_DIM, blk), BF16)
        for j in range(nb):
            vt = v_ref[0, j * blk:(j + 1) * blk, :].T.astype(BF16)
            for h in range(N_HEADS):
                vt_ref[j, h, 0:HEAD_DIM, :] = vt[h * HEAD_DIM:(h + 1) * HEAD_DIM, :]
                vt_ref[j, h, HEAD_DIM:, :] = ones
        means = [jnp.mean(k_ref[0, j * blk:(j + 1) * blk, :], axis=0, keepdims=True) for j in range(nb)]
        means += [jnp.zeros_like(means[0])] * (nbp - nb)
        km_ref[...] = jnp.concatenate(means, axis=0)

    lane = lax.broadcasted_iota(jnp.int32, (blk, LANES), 1)
    blk_row = lax.broadcasted_iota(jnp.int32, (nbp, blk), 0)
    valid = blk_row < i
    jm1 = jnp.maximum(i - 1, 0)
    own0 = pl.multiple_of(i * blk, blk)
    adj0 = pl.multiple_of(jm1 * blk, blk)
    heads_per_vreg = LANES // HEAD_DIM

    def lanes_of(h):
        p = h // heads_per_vreg
        return slice(p * LANES, (p + 1) * LANES)

    for h in range(N_HEADS):
        hh = h % heads_per_vreg
        q2 = q_ref[0, :, lanes_of(h)]
        hmask = (lane >= hh * HEAD_DIM) & (lane < (hh + 1) * HEAD_DIM)
        qm = jnp.where(hmask, q2, jnp.zeros_like(q2))
        qm_ref[h] = qm
        gate = _nt_dot(km_ref[:, lanes_of(h)].astype(BF16), qm)
        for j in range(nb):
            gj = gate[j:j + 1, :]
            beats = ((gate > gj) | ((gate == gj) & (blk_row < j))) & valid
            cnt = jnp.sum(beats.astype(F32), axis=0, keepdims=True)
            cnt = cnt + jnp.where(j < i, 0.0, float(nb))
            madd_ref[h, j:j + 1, :] = jnp.where(cnt < MOBA_TOPK, 0.0, NEG)

    def key_blocks(blocks):
        for n, (key0, _, tile_bias, _, _) in enumerate(blocks):
            for h in range(N_HEADS):
                s = _nt_dot(kb_ref[pl.ds(key0, blk), lanes_of(h)], qm_ref[h])
                s_ref[n, h] = s if tile_bias is None else s + tile_bias(h)
        for n, (_, blk_idx, _, query_bias, first) in enumerate(blocks):
            for h in range(N_HEADS):
                s = s_ref[n, h]
                m_blk = jnp.max(s, axis=0, keepdims=True)
                if query_bias is not None:
                    qb = query_bias(h)
                    m_blk = m_blk + qb
                if first:
                    m_new = m_blk
                else:
                    m_old = m_ref[h]
                    m_new = jnp.maximum(m_old, m_blk)
                ref_row = m_new if query_bias is None else m_new - qb
                p = jnp.exp2(s - ref_row).astype(BF16)
                pv = _dot(vt_ref[blk_idx, h], p)
                acc_ref[h] = pv if first else jnp.exp2(m_old - m_new) * acc_ref[h] + pv
                m_ref[h] = m_new

    def far_block(j):
        return (pl.multiple_of(j * blk, blk), j, None,
                lambda h: madd_ref[h, pl.ds(j, 1), :] + tab_ref[h, N_BUCKETS - 1] * LOG2E, False)

    key_blocks([(own0, i, lambda h: bown_ref[h], None, True),
                (adj0, jm1, lambda h: badj_ref[h], lambda h: madd_ref[h, pl.ds(jm1, 1), :], False)])

    def far_pair(jj, carry):
        key_blocks([far_block(FAR_GROUP * jj + n) for n in range(FAR_GROUP)])
        return carry

    lax.fori_loop(0, jm1 // FAR_GROUP, far_pair, 0)
    for n in range(1, FAR_GROUP):
        @pl.when(jm1 % FAR_GROUP >= n)
        def _():
            key_blocks([far_block(jm1 - n)])

    for p in range(ATTN_WIDTH // LANES):
        halves = []
        for h in range(p * heads_per_vreg, (p + 1) * heads_per_vreg):
            acc = acc_ref[h]
            halves.append(acc[0:HEAD_DIM, :] / acc[HEAD_DIM:HEAD_DIM + 1, :])
        o_ref[0, :, p * LANES:(p + 1) * LANES] = jnp.concatenate(halves, axis=0).T.astype(BF16)


def _attn_prompt(tab_t, q, k, v, bias_own_t, bias_adj_t):
    b, s, a = q.shape
    blk = MOBA_BLOCK
    nb = s // blk
    nbp = 8
    assert s % blk == 0 and nb <= nbp
    full = pl.BlockSpec((1, s, a), lambda bi, i: (bi, 0, 0))
    tile = pl.BlockSpec((1, blk, a), lambda bi, i: (bi, i, 0))
    bias = pl.BlockSpec((N_HEADS, blk, blk), lambda bi, i: (0, 0, 0))
    return pl.pallas_call(
        functools.partial(_attn_prompt_kernel, nb=nb),
        grid=(b, nb),
        in_specs=[pl.BlockSpec(memory_space=pltpu.SMEM), tile, full, full, bias, bias],
        out_specs=tile,
        out_shape=jax.ShapeDtypeStruct((b, s, a), BF16),
        scratch_shapes=[pltpu.VMEM((s, a), BF16), pltpu.VMEM((nb, N_HEADS, VT_ROWS, blk), BF16),
                        pltpu.VMEM((nbp, a), F32), pltpu.VMEM((N_HEADS, blk, LANES), BF16),
                        pltpu.VMEM((N_HEADS, nbp, blk), F32), pltpu.VMEM((N_HEADS, 1, blk), F32),
                        pltpu.VMEM((N_HEADS, VT_ROWS, blk), F32),
                        pltpu.VMEM((FAR_GROUP, N_HEADS, blk, blk), F32)],
        compiler_params=pltpu.CompilerParams(dimension_semantics=("arbitrary", "arbitrary"),
                                             vmem_limit_bytes=VMEM_LIMIT),
        name="moba_prompt",
    )(tab_t, q, k, v, bias_own_t, bias_adj_t)


def _pool_rows(u, hist, w_ref, scale_ref, first_row_pos):
    tp = u.shape[0]
    ext = jnp.concatenate([hist, u], axis=0)
    pos = first_row_pos + lax.broadcasted_iota(jnp.int32, (tp, 1), 0)
    outs = []
    for g, w in enumerate(POOL_WINDOWS):
        gs = slice(g * POOL_GROUP, (g + 1) * POOL_GROUP)
        s = ext[:, gs]
        shift = 1
        while shift < w:
            s = s + pltpu.roll(s, shift, 0)
            shift *= 2
        cnt = jnp.minimum(pos + 1, w).astype(F32)
        d = s[POOL_HALO:, :] / cnt - u[:, gs]
        outs.append((_dot(d.astype(BF16), w_ref[g]) * scale_ref[:, gs]).astype(BF16))
    return jnp.concatenate(outs, axis=1)


def _pool_kernel(u_ref, hist_ref, w_ref, scale_ref, o_ref, *, first_pos):
    i = pl.program_id(1)
    hist = jnp.where(i > 0, hist_ref[0], 0.0)
    o_ref[0] = _pool_rows(u_ref[0], hist, w_ref, scale_ref, first_pos + i * u_ref.shape[1])


def _halo_spec(tile_rows, width):
    per_tile = tile_rows // POOL_HALO
    return pl.BlockSpec((1, POOL_HALO, width), lambda b, i, *_: (b, jnp.maximum(i * per_tile - 1, 0), 0))


def _pool(u, w_pool_b, pool_scale, first_pos):
    nb, s, c = u.shape
    tp = _row_tile(s, (512, 256, 128, 64, 32, 16))
    return pl.pallas_call(
        functools.partial(_pool_kernel, first_pos=first_pos),
        grid=(nb, s // tp),
        in_specs=[pl.BlockSpec((1, tp, c), lambda b, i: (b, i, 0)),
                  _halo_spec(tp, c),
                  pl.BlockSpec(w_pool_b.shape, lambda b, i: (0, 0, 0)),
                  pl.BlockSpec((1, c), lambda b, i: (0, 0))],
        out_specs=pl.BlockSpec((1, tp, c), lambda b, i: (b, i, 0)),
        out_shape=jax.ShapeDtypeStruct((nb, s, c), BF16),
        compiler_params=pltpu.CompilerParams(dimension_semantics=("arbitrary", "arbitrary"),
                                             vmem_limit_bytes=VMEM_LIMIT),
        name="multi_pool",
    )(u, u, w_pool_b, pool_scale)


def _out_kernel(pt_ref, x_ref, attn_ref, pool_ref, mods_ref, gm_ref, gf_ref, wo_ref, wu_ref, wd_ref, *refs,
                fused_pool, rider):
    if fused_pool:
        hist_ref, wp_ref, ps_ref = refs[:3]
        refs = refs[3:]
    n_chunks = wu_ref.shape[1] // FF_CHUNK
    if rider is None:
        (y_ref,) = refs
        n_pages = 0
    else:
        ck_ref, y_ref, kmean_ref, pbuf, psem = refs
        n_pages, groups_per_seq, n_groups = rider
        step = pl.program_id(0) * pl.num_programs(1) + pl.program_id(1)
        last = pl.num_programs(0) * pl.num_programs(1) - 1
        slot = step % 2
        nxt = jnp.minimum(step + 1, last)

        def page_copies(st, sl, first, count):
            g = jnp.minimum(st, n_groups - 1)
            seq, part = g // groups_per_seq, g % groups_per_seq
            return [pltpu.make_async_copy(ck_ref.at[0, pt_ref[seq, part * n_pages + n]], pbuf.at[sl, n],
                                          psem.at[sl]) for n in range(first, first + count)]

        @pl.when(step == 0)
        def _():
            for c in page_copies(step, slot, 0, n_pages):
                c.start()

        for c in page_copies(nxt, 1 - slot, 0, n_pages):
            c.start()
        for c in page_copies(step, slot, 0, n_pages):
            c.wait()
    n_blocks = n_pages // PAGES_PER_BLOCK
    if fused_pool:
        i = pl.program_id(1)
        hist = jnp.where(i > 0, hist_ref[0], 0.0)
        pool = _pool_rows(pool_ref[0], hist, wp_ref, ps_ref, i * x_ref.shape[1])
    else:
        pool = pool_ref[0]
    d = x_ref.shape[-1]
    a = ATTN_WIDTH
    x = x_ref[0]
    g1 = mods_ref[0, :, 2 * d:3 * d]
    sh2 = mods_ref[0, :, 3 * d:4 * d]
    sc2 = mods_ref[0, :, 4 * d:5 * d]
    g2 = mods_ref[0, :, 5 * d:6 * d]
    mix = _dot(attn_ref[0], wo_ref[0:a, :]) + _dot(pool, wo_ref[a:, :])
    x1 = x + g1 * mix
    hb = (_rms(x1, gm_ref[...]) * (1.0 + sc2) + sh2).astype(BF16)
    acc = jnp.zeros(x.shape, F32)
    for c in range(n_chunks):
        cs = slice(c * FF_CHUNK, (c + 1) * FF_CHUNK)
        t = jnp.maximum(_dot(hb, wu_ref[:, cs]), 0.0)
        acc = acc + _dot((t * t).astype(BF16), wd_ref[cs, :])
        if rider is not None:
            for blk in range(c * n_blocks // n_chunks, (c + 1) * n_blocks // n_chunks):
                ksum = pbuf[slot, blk * PAGES_PER_BLOCK]
                for pg in range(1, PAGES_PER_BLOCK):
                    ksum = ksum + pbuf[slot, blk * PAGES_PER_BLOCK + pg]
                kmean_ref[0, blk] = jnp.sum(ksum, axis=-1) * (1.0 / MOBA_BLOCK)
    x2 = x1 + g2 * acc
    y_ref[0] = _rms(x2, gf_ref[...])
    if rider is not None:
        @pl.when(step == last)
        def _():
            for cp in page_copies(nxt, 1 - slot, 0, n_pages):
                cp.wait()


def _out(x, attn, pool, mods, g_mlp, g_final, w_out_b, w_up_b, w_down_b, page_table, pool_params=None,
         cache_kt=None, tm=None):
    nb, s, d = x.shape
    r = mods.shape[1]
    tm = tm or _row_tile(s)
    n_tiles = s // tm
    row_spec = lambda w: pl.BlockSpec((1, tm, w), lambda b, i, pt: (b, i, 0))
    mods_spec = (pl.BlockSpec((1, 1, N_MOD * d), lambda b, i, pt: (b, 0, 0)) if r == 1
                 else pl.BlockSpec((1, tm, N_MOD * d), lambda b, i, pt: (b, i, 0)))
    resident = lambda w: pl.BlockSpec(w.shape, lambda b, i, pt: (0, 0), pipeline_mode=pl.Buffered(1))
    vec = pl.BlockSpec((1, d), lambda b, i, pt: (0, 0))
    out_specs, out_shape = [row_spec(d)], [jax.ShapeDtypeStruct((nb, s, d), F32)]
    pool_specs, pool_args, cache_specs, cache_args, scratch, rider = [], [], [], [], [], None
    if pool_params is not None:
        w_pool_b, pool_scale = pool_params
        pool_specs = [_halo_spec(tm, POOL_WIDTH), pl.BlockSpec(w_pool_b.shape, lambda b, i, pt: (0, 0, 0)),
                      pl.BlockSpec((1, POOL_WIDTH), lambda b, i, pt: (0, 0))]
        pool_args = [pool, w_pool_b, pool_scale]
    if cache_kt is not None:
        db, n_pages = page_table.shape
        nblk = n_pages // PAGES_PER_BLOCK
        need = -(-db * nblk // (nb * n_tiles))
        group = min(g for g in range(1, nblk + 1) if nblk % g == 0 and g >= need)
        groups_per_seq = nblk // group
        n_groups = db * groups_per_seq
        pps = group * PAGES_PER_BLOCK
        rider = (pps, groups_per_seq, n_groups)

        def group_of(b, i):
            g = jnp.minimum(b * n_tiles + i, n_groups - 1)
            return g // groups_per_seq, g % groups_per_seq

        cache_specs, cache_args = [pl.BlockSpec(memory_space=pl.ANY)], [cache_kt]
        scratch = [pltpu.VMEM((2, pps) + cache_kt.shape[2:], F32), pltpu.SemaphoreType.DMA((2,))]
        out_specs.append(pl.BlockSpec((1, group, N_HEADS, HEAD_DIM), lambda b, i, pt: group_of(b, i) + (0, 0)))
        out_shape.append(jax.ShapeDtypeStruct((db, nblk, N_HEADS, HEAD_DIM), F32))
    grid_spec = pltpu.PrefetchScalarGridSpec(
        num_scalar_prefetch=1,
        grid=(nb, n_tiles),
        in_specs=[row_spec(d), row_spec(ATTN_WIDTH), row_spec(POOL_WIDTH), mods_spec, vec, vec,
                  resident(w_out_b), resident(w_up_b), resident(w_down_b)] + pool_specs + cache_specs,
        out_specs=out_specs,
        scratch_shapes=scratch,
    )
    out = pl.pallas_call(
        functools.partial(_out_kernel, fused_pool=pool_params is not None, rider=rider),
        grid_spec=grid_spec,
        out_shape=out_shape,
        compiler_params=pltpu.CompilerParams(dimension_semantics=("arbitrary", "arbitrary"),
                                             vmem_limit_bytes=VMEM_LIMIT),
        name="out_mlp",
    )(page_table, x, attn, pool, mods, g_mlp, g_final, w_out_b, w_up_b, w_down_b, *pool_args, *cache_args)
    return out if cache_kt is not None else out[0]


def _select_kernel(q_ref, km_ref, sel_ref):
    for sq in range(q_ref.shape[0]):
        g = lax.dot_general(q_ref[sq].astype(F32), km_ref[sq], (((1,), (1,)), ((), ())),
                            precision=lax.Precision.HIGHEST, preferred_element_type=F32)
        rows, nblk = g.shape
        lane = lax.broadcasted_iota(jnp.int32, g.shape, 1)
        out_lane = lax.broadcasted_iota(jnp.int32, (rows, LANES), 1)
        out = jnp.zeros((rows, LANES), jnp.int32)
        for n in range(MOBA_TOPK):
            mx = jnp.max(g, axis=-1, keepdims=True)
            idx = jnp.min(jnp.where(g == mx, lane, nblk), axis=-1, keepdims=True)
            out = jnp.where(out_lane == n, idx, out)
            g = jnp.where(lane == idx, -jnp.inf, g)
        sel_ref[sq] = out


def _select(q_bd, kmean2d):
    db, rows, a = q_bd.shape
    nblk = kmean2d.shape[1]
    seqs = math.gcd(db, SELECT_SEQS_PER_STEP)
    return pl.pallas_call(
        _select_kernel,
        grid=(db // seqs,),
        in_specs=[pl.BlockSpec((seqs, rows, a), lambda b: (b, 0, 0)),
                  pl.BlockSpec((seqs, nblk, a), lambda b: (b, 0, 0))],
        out_specs=pl.BlockSpec((seqs, rows, LANES), lambda b: (b, 0, 0)),
        out_shape=jax.ShapeDtypeStruct((db, rows, LANES), jnp.int32),
        compiler_params=pltpu.CompilerParams(dimension_semantics=("arbitrary",)),
        name="moba_select",
    )(q_bd, kmean2d)


def _attn_sample_kernel(pt_ref, sel_ref, tab_ref, q_ref, kn_ref, vn_ref, badj_ref, bown_ref, ck_ref, cv_ref,
                        o_ref, kbuf, vbuf, sem, *, nblk, t_new):
    b = pl.program_id(0)
    n_seq = pl.num_programs(0)
    n_sel = MOBA_TOPK
    per_head = t_new * n_sel
    buf = b % 2

    def block_of(sb, h, t, n):
        return sel_ref[((sb * N_HEADS + h) * t_new + t) * n_sel + n]

    def copies(sb, h, bf):
        out = []
        for t in range(t_new):
            for n in range(n_sel):
                j = block_of(sb, h, t, n)
                slot = (bf * N_HEADS + h) * per_head + t * n_sel + n
                for pg in range(PAGES_PER_BLOCK):
                    page = pt_ref[sb, j * PAGES_PER_BLOCK + pg]
                    keys = pl.ds(pg * PAGE_SIZE, PAGE_SIZE)
                    out.append(pltpu.make_async_copy(ck_ref.at[0, page, h], kbuf.at[slot, :, keys], sem.at[0, bf]))
                    out.append(pltpu.make_async_copy(cv_ref.at[0, page, h], vbuf.at[slot, :, keys], sem.at[1, bf]))
        return out

    @pl.when(b == 0)
    def _():
        for h in range(N_HEADS):
            for c in copies(b, h, buf):
                c.start()

    for h in range(N_HEADS):
        for c in copies(b, h, buf):
            c.wait()

    tok_row = lax.broadcasted_iota(jnp.int32, (SAMPLE_ROWS, MOBA_BLOCK), 0)
    ones_rows = jnp.ones((BF16_SUBLANES, MOBA_BLOCK), BF16)

    def head_attention(h):
        slot0 = (buf * N_HEADS + h) * per_head
        hl = slice(h * HEAD_DIM, (h + 1) * HEAD_DIM)
        c_far = tab_ref[h, N_BUCKETS - 1] * LOG2E
        q_h = q_ref[0, :, hl]
        scores = []
        for t in range(t_new):
            for n in range(n_sel):
                j = block_of(b, h, t, n)
                s = _dot(q_h, kbuf[slot0 + t * n_sel + n].astype(BF16))
                bias = jnp.where(j == nblk - 1, badj_ref[h], c_far)
                scores.append(jnp.where(tok_row == t, s + bias, NEG))
        s_own = _nt_dot(q_h, kn_ref[0, :, hl].astype(BF16)) + bown_ref[h]
        m_tile = scores[0]
        for s in scores[1:]:
            m_tile = jnp.maximum(m_tile, s)
        m = jnp.maximum(jnp.max(m_tile, axis=-1, keepdims=True), jnp.max(s_own, axis=-1, keepdims=True))
        p_own = jnp.exp2(s_own - m).astype(BF16)
        v_own = jnp.concatenate([vn_ref[0, h].astype(BF16), ones_rows[:, :SAMPLE_ROWS]], axis=0)
        acc = _nt_dot(v_own, p_own)
        for i_blk, s in enumerate(scores):
            p = jnp.exp2(s - m).astype(BF16)
            v_blk = jnp.concatenate([vbuf[slot0 + i_blk].astype(BF16), ones_rows], axis=0)
            acc = acc + _nt_dot(v_blk, p)
        o_ref[0, h] = acc[0:HEAD_DIM, :] / acc[HEAD_DIM:HEAD_DIM + 1, :]

    nxt = jnp.minimum(b + 1, n_seq - 1)
    for h in range(N_HEADS):
        for c in copies(nxt, h, 1 - buf):
            c.start()
        head_attention(h)

    @pl.when(b == n_seq - 1)
    def _():
        for h in range(N_HEADS):
            for c in copies(nxt, h, 1 - buf):
                c.wait()


def _attn_sample(page_table, sel_flat, tab_t, q_rows, k_rows, v_cols, badj, bown, cache_kt, cache_vt, nblk, t_new):
    db, rows, a = q_rows.shape
    nh, dh = v_cols.shape[1:3]
    n_slots = 2 * nh * t_new * MOBA_TOPK
    grid_spec = pltpu.PrefetchScalarGridSpec(
        num_scalar_prefetch=2,
        grid=(db,),
        in_specs=[pl.BlockSpec(memory_space=pltpu.SMEM),
                  pl.BlockSpec((1, rows, a), lambda b, pt, sel: (b, 0, 0)),
                  pl.BlockSpec((1, rows, a), lambda b, pt, sel: (b, 0, 0)),
                  pl.BlockSpec((1, nh, dh, rows), lambda b, pt, sel: (b, 0, 0, 0)),
                  pl.BlockSpec(badj.shape, lambda b, pt, sel: (0, 0, 0)),
                  pl.BlockSpec(bown.shape, lambda b, pt, sel: (0, 0, 0)),
                  pl.BlockSpec(memory_space=pl.ANY), pl.BlockSpec(memory_space=pl.ANY)],
        out_specs=pl.BlockSpec((1, nh, dh, rows), lambda b, pt, sel: (b, 0, 0, 0)),
        scratch_shapes=[pltpu.VMEM((n_slots, dh, MOBA_BLOCK), F32),
                        pltpu.VMEM((n_slots, dh, MOBA_BLOCK), F32),
                        pltpu.SemaphoreType.DMA((2, 2))],
    )
    return pl.pallas_call(
        functools.partial(_attn_sample_kernel, nblk=nblk, t_new=t_new),
        grid_spec=grid_spec,
        out_shape=jax.ShapeDtypeStruct((db, nh, dh, rows), F32),
        compiler_params=pltpu.CompilerParams(dimension_semantics=("arbitrary",),
                                             vmem_limit_bytes=VMEM_LIMIT),
        name="moba_sample",
    )(page_table, sel_flat, tab_t, q_rows, k_rows, v_cols, badj, bown, cache_kt, cache_vt)


def kernel(x_prompt, x_sample, cache_k, cache_v, state_pool, page_table, c_prompt, c_sample, w_ada, b_ada, norm_mix, w_in, rel_bias, w_pool, pool_scale, w_out, norm_mlp, w_up, w_down, norm_final):
    assert w_ada.shape[0] == 1, "single-layer decoder"
    b, s, d = x_prompt.shape
    db, t_new, _ = x_sample.shape
    n_pages = page_table.shape[1]
    past = n_pages * PAGE_SIZE
    assert past % MOBA_BLOCK == 0 and past // MOBA_BLOCK >= MOBA_TOPK
    nblk = past // MOBA_BLOCK

    w_in_b = w_in[0].astype(BF16)
    w_out_b = w_out[0].astype(BF16)
    w_up_b = w_up[0].astype(BF16)
    w_down_b = w_down[0].astype(BF16)
    w_pool_b = w_pool[0].astype(BF16)
    g_final = norm_final.reshape(1, d)
    tab_t = rel_bias.T

    mods = _ada(jnp.concatenate([c_prompt, c_sample], axis=0), w_ada, b_ada[0:1])
    mods_p = mods[:b].reshape(b, 1, N_MOD * d)
    mods_s = jnp.repeat(mods[b:], t_new, axis=0).reshape(1, db * t_new, N_MOD * d)
    bias_own, bias_adj = _bias_tiles(tab_t)

    cache_kt = cache_k.transpose(0, 1, 3, 4, 2)
    cache_vt = cache_v.transpose(0, 1, 3, 4, 2)

    q_p, k_p, v_p, u_p = _inproj(x_prompt, mods_p, norm_mix, w_in_b)
    attn_p = _attn_prompt(tab_t, q_p, k_p, v_p, bias_own, bias_adj)
    y_p, kmean = _out(x_prompt, attn_p, u_p, mods_p, norm_mlp, g_final, w_out_b, w_up_b, w_down_b,
                      page_table, pool_params=(w_pool_b, pool_scale), cache_kt=cache_kt,
                      tm=_row_tile(s, (OUT_TILE_WITH_PAGES, 128, 64, 32, 16)))

    xs = x_sample.reshape(1, db * t_new, d)
    q_s, k_s, v_s, u_s = _inproj(xs, mods_s, norm_mix, w_in_b)
    head_of_lane = jnp.arange(ATTN_WIDTH) // HEAD_DIM
    q_bt = q_s.reshape(db, 1, t_new, ATTN_WIDTH)
    q_bd = jnp.where(head_of_lane[None, None, None, :] == jnp.arange(N_HEADS)[None, :, None, None],
                     q_bt, jnp.zeros_like(q_bt)).reshape(db, N_HEADS * t_new, ATTN_WIDTH)
    sel = _select(q_bd, kmean.reshape(db, nblk, ATTN_WIDTH))[:, :, :MOBA_TOPK]
    assert t_new <= SAMPLE_ROWS
    pad_rows = lambda z: jnp.pad(z.reshape(db, t_new, ATTN_WIDTH), ((0, 0), (0, SAMPLE_ROWS - t_new), (0, 0)))
    v_cols = pad_rows(v_s).reshape(db, SAMPLE_ROWS, N_HEADS, HEAD_DIM).transpose(0, 2, 3, 1)
    badj_rows = bias_adj[:, :, :SAMPLE_ROWS].transpose(0, 2, 1)
    bown_rows = bias_own[:, :SAMPLE_ROWS, :SAMPLE_ROWS].transpose(0, 2, 1)
    attn_s = _attn_sample(page_table, sel.reshape(-1), tab_t, pad_rows(q_s), pad_rows(k_s), v_cols,
                          badj_rows, bown_rows, cache_kt, cache_vt, nblk, t_new)
    attn_s = attn_s[..., :t_new].transpose(0, 3, 1, 2).reshape(1, db * t_new, ATTN_WIDTH).astype(BF16)

    u_full = jnp.concatenate([state_pool[0], u_s.reshape(db, t_new, POOL_WIDTH)], axis=1)
    ext_rows = POOL_HALO + 8
    u_ext = jnp.pad(u_full, ((0, 0), (1, ext_rows - 1 - u_full.shape[1]), (0, 0)))
    pool_ext = _pool(u_ext.reshape(1, db * ext_rows, POOL_WIDTH), w_pool_b, pool_scale, first_pos=past)
    pool_s = pool_ext.reshape(db, ext_rows, POOL_WIDTH)[:, POOL_HALO:POOL_HALO + t_new]
    pool_s = pool_s.reshape(1, db * t_new, POOL_WIDTH)
    y_s = _out(xs, attn_s, pool_s, mods_s, norm_mlp, g_final, w_out_b, w_up_b, w_down_b, page_table)

    heads = lambda z, n, l: z.reshape(1, n, l, N_HEADS, HEAD_DIM)
    return (y_p, y_s.reshape(db, t_new, d),
            heads(k_p, b, s), heads(v_p, b, s), u_p[:, -POOL_STATE:][None],
            heads(k_s, db, t_new), heads(v_s, db, t_new), u_full[:, -POOL_STATE:][None])
```

```python
import functools
import math

import numpy as np
import jax
import jax.numpy as jnp
from jax import lax
from jax.experimental import pallas as pl
from jax.experimental.pallas import tpu as pltpu

HEAD_DIM = 64
N_HEADS = 8
ATTN_WIDTH = N_HEADS * HEAD_DIM
POOL_WINDOWS = (2, 4, 8, 16)
POOL_GROUP = 128
POOL_WIDTH = POOL_GROUP * len(POOL_WINDOWS)
POOL_STATE = max(POOL_WINDOWS) - 1
POOL_HALO = POOL_STATE + 1
MOBA_BLOCK = 256
MOBA_TOPK = 3
N_BUCKETS = 32
MAX_DISTANCE = 128
PAGE_SIZE = 128
PAGES_PER_BLOCK = MOBA_BLOCK // PAGE_SIZE
N_MOD = 6
EPS = 1e-6
NEG = -1e30
SCALE = HEAD_DIM ** -0.5
LOG2E = math.log2(math.e)
Q_SCALE = SCALE * LOG2E
LANES = 128
BF16_SUBLANES = 16
VT_ROWS = HEAD_DIM + BF16_SUBLANES
FF_CHUNK = 1024
INPROJ_TILE = 1024
SAMPLE_ROWS = 8
SELECT_SEQS_PER_STEP = 8
FAR_GROUP = 2
OUT_TILE_WITH_PAGES = 256
VMEM_LIMIT = 56 * 1024 * 1024

BF16 = jnp.bfloat16
F32 = jnp.float32


def _nt_dot(a, b):
    return lax.dot_general(a, b, (((1,), (1,)), ((), ())), preferred_element_type=F32)


def _dot(a, b):
    return jnp.dot(a, b, preferred_element_type=F32)


def _row_tile(n, candidates=(512, 256, 128, 64, 32, 16, 8)):
    for c in candidates:
        if n % c == 0:
            return c
    raise ValueError(f"row count {n} is not a multiple of 8")


def _t5_bucket_np(rel):
    n = np.maximum(rel, 0)
    max_exact = N_BUCKETS // 2
    nf = np.maximum(n, max_exact).astype(np.float32)
    large = max_exact + (np.log(nf / np.float32(max_exact)) / np.float32(math.log(MAX_DISTANCE / max_exact))
                         * np.float32(N_BUCKETS - max_exact)).astype(np.int32)
    large = np.minimum(large, N_BUCKETS - 1)
    return np.where(n < max_exact, n, large).astype(np.int32)


def _rms(x, g):
    return x * lax.rsqrt(jnp.mean(x * x, axis=-1, keepdims=True) + EPS) * g


def _rms_modulated(x, g, scale, shift):
    return x * lax.rsqrt(jnp.mean(x * x, axis=-1, keepdims=True) + EPS) * (g * (1.0 + scale)) + shift


def _ada_kernel(c_ref, w_ref, b_ref, o_ref):
    c = c_ref[...]
    s = c / (1.0 + jnp.exp(-c))
    o_ref[...] = _dot(s.astype(BF16), w_ref[0].astype(BF16)) + b_ref[...]


def _ada(c_all, w_ada, b_ada):
    n, d = c_all.shape
    width = w_ada.shape[-1]
    tn = 1024
    return pl.pallas_call(
        _ada_kernel,
        grid=(width // tn,),
        in_specs=[pl.BlockSpec((n, d), lambda j: (0, 0)),
                  pl.BlockSpec((1, d, tn), lambda j: (0, 0, j)),
                  pl.BlockSpec((1, tn), lambda j: (0, j))],
        out_specs=pl.BlockSpec((n, tn), lambda j: (0, j)),
        out_shape=jax.ShapeDtypeStruct((n, width), F32),
        compiler_params=pltpu.CompilerParams(dimension_semantics=("arbitrary",), vmem_limit_bytes=VMEM_LIMIT),
        name="ada_mod",
    )(c_all, w_ada, b_ada)


def _inproj_kernel(x_ref, mods_ref, g_ref, w_ref, *refs, fused_pool):
    if fused_pool:
        wp_ref, ps_ref, q_ref, k_ref, v_ref, pool_ref, tail_ref, hist_ref = refs
        i = pl.program_id(1)

        @pl.when(i == 0)
        def _():
            hist_ref[...] = jnp.zeros(hist_ref.shape, F32)
    else:
        q_ref, k_ref, v_ref, u_ref = refs
    d = x_ref.shape[-1]
    ts = x_ref.shape[1]
    x = x_ref[0]
    shift = mods_ref[0, :, 0:d]
    scale = mods_ref[0, :, d:2 * d]
    hb = _rms_modulated(x, g_ref[...], scale, shift).astype(BF16)
    a = ATTN_WIDTH
    u = _dot(hb, w_ref[:, 3 * a:])
    if fused_pool:
        pool_ref[0] = _pool_rows(u, hist_ref[...], wp_ref, ps_ref, i * ts)
        hist_ref[...] = u[ts - POOL_HALO:, :]
        tail_ref[0] = u[ts - POOL_HALO:, :]
    else:
        u_ref[0] = u
    r = _dot(hb, w_ref[:, 0:3 * a])
    q_ref[0] = (r[:, 0:a] * Q_SCALE).astype(BF16)
    k_ref[0] = r[:, a:2 * a]
    v_ref[0] = r[:, 2 * a:3 * a]


def _inproj(x, mods, g, w_in_b, pool_params=None):
    nb, s, d = x.shape
    r = mods.shape[1]
    ts = _row_tile(s, (INPROJ_TILE, 512, 256, 128, 64, 32, 16, 8))
    width = w_in_b.shape[1]
    row_spec = lambda w: pl.BlockSpec((1, ts, w), lambda b, i: (b, i, 0))
    mods_spec = (pl.BlockSpec((1, 1, N_MOD * d), lambda b, i: (b, 0, 0)) if r == 1
                 else pl.BlockSpec((1, ts, N_MOD * d), lambda b, i: (b, i, 0)))
    in_specs = [row_spec(d), mods_spec, pl.BlockSpec((1, d), lambda b, i: (0, 0)),
                pl.BlockSpec((d, width), lambda b, i: (0, 0))]
    out_specs = [row_spec(ATTN_WIDTH)] * 3
    out_shape = [jax.ShapeDtypeStruct((nb, s, ATTN_WIDTH), BF16), jax.ShapeDtypeStruct((nb, s, ATTN_WIDTH), F32),
                 jax.ShapeDtypeStruct((nb, s, ATTN_WIDTH), F32)]
    args, scratch = [x, mods, g, w_in_b], []
    if pool_params is None:
        out_specs.append(row_spec(POOL_WIDTH))
        out_shape.append(jax.ShapeDtypeStruct((nb, s, POOL_WIDTH), F32))
    else:
        assert ts % POOL_HALO == 0
        w_pool_b, pool_scale = pool_params
        in_specs += [pl.BlockSpec(w_pool_b.shape, lambda b, i: (0, 0, 0)),
                     pl.BlockSpec((1, POOL_WIDTH), lambda b, i: (0, 0))]
        args += [w_pool_b, pool_scale]
        out_specs += [row_spec(POOL_WIDTH), pl.BlockSpec((1, POOL_HALO, POOL_WIDTH), lambda b, i: (b, 0, 0))]
        out_shape += [jax.ShapeDtypeStruct((nb, s, POOL_WIDTH), BF16),
                      jax.ShapeDtypeStruct((nb, POOL_HALO, POOL_WIDTH), F32)]
        scratch = [pltpu.VMEM((POOL_HALO, POOL_WIDTH), F32)]
    return pl.pallas_call(
        functools.partial(_inproj_kernel, fused_pool=pool_params is not None),
        grid=(nb, s // ts),
        in_specs=in_specs,
        out_specs=out_specs,
        out_shape=out_shape,
        scratch_shapes=scratch,
        compiler_params=pltpu.CompilerParams(dimension_semantics=("arbitrary", "arbitrary"),
                                             vmem_limit_bytes=VMEM_LIMIT),
        name="in_proj",
    )(*args)


def _bias_kernel(tab_ref, *refs):
    n = len(refs) // 2
    h = pl.program_id(0)
    for idx_ref, out_ref in zip(refs[:n], refs[n:]):
        idx = idx_ref[...]
        bias = jnp.zeros(idx.shape, F32)
        for b in range(N_BUCKETS):
            bias = jnp.where(idx == b, tab_ref[h, b], bias)
        out_ref[0] = jnp.where(idx < 0, NEG, bias * LOG2E)


def _bias_tiles(tab_t):
    blk = MOBA_BLOCK
    r = np.arange(blk)[None, :]
    c = np.arange(blk)[:, None]
    rows = np.arange(SAMPLE_ROWS)[:, None]
    keys = np.arange(blk)[None, :]
    own_keys = np.arange(LANES)[None, :]
    idx = [np.where(r >= c, _t5_bucket_np(r - c), -1), _t5_bucket_np(blk + r - c),
           _t5_bucket_np(blk + rows - keys), np.where(rows >= own_keys, _t5_bucket_np(rows - own_keys), -1)]
    idx = [jnp.asarray(t.astype(np.int32)) for t in idx]
    return pl.pallas_call(
        _bias_kernel,
        grid=(N_HEADS,),
        in_specs=[pl.BlockSpec(memory_space=pltpu.SMEM)] + [pl.BlockSpec(t.shape, lambda h: (0, 0)) for t in idx],
        out_specs=[pl.BlockSpec((1,) + t.shape, lambda h: (h, 0, 0)) for t in idx],
        out_shape=[jax.ShapeDtypeStruct((N_HEADS,) + t.shape, F32) for t in idx],
        compiler_params=pltpu.CompilerParams(dimension_semantics=("arbitrary",)),
        name="rel_bias_tiles",
    )(tab_t, *idx)


def _attn_prompt_kernel(tab_ref, q_ref, k_ref, v_ref, bown_ref, badj_ref, o_ref,
                        kb_ref, vt_ref, km_ref, qm_ref, madd_ref, m_ref, acc_ref, s_ref, *, nb):
    i = pl.program_id(1)
    blk = MOBA_BLOCK
    nbp = km_ref.shape[0]

    @pl.when(i == 0)
    def _():
        kb_ref[...] = k_ref[0].astype(BF16)
        ones = jnp.ones((VT_ROWS - HEAD_DIM, blk), BF16)
        for j in range(nb):
            vt = v_ref[0, j * blk:(j + 1) * blk, :].T.astype(BF16)
            for h in range(N_HEADS):
                vt_ref[j, h, 0:HEAD_DIM, :] = vt[h * HEAD_DIM:(h + 1) * HEAD_DIM, :]
                vt_ref[j, h, HEAD_DIM:, :] = ones
        means = [jnp.mean(k_ref[0, j * blk:(j + 1) * blk, :], axis=0, keepdims=True) for j in range(nb)]
        means += [jnp.zeros_like(means[0])] * (nbp - nb)
        km_ref[...] = jnp.concatenate(means, axis=0)

    lane = lax.broadcasted_iota(jnp.int32, (blk, LANES), 1)
    blk_row = lax.broadcasted_iota(jnp.int32, (nbp, blk), 0)
    valid = blk_row < i
    jm1 = jnp.maximum(i - 1, 0)
    own0 = pl.multiple_of(i * blk, blk)
    adj0 = pl.multiple_of(jm1 * blk, blk)
    heads_per_vreg = LANES // HEAD_DIM

    def lanes_of(h):
        p = h // heads_per_vreg
        return slice(p * LANES, (p + 1) * LANES)

    for h in range(N_HEADS):
        hh = h % heads_per_vreg
        q2 = q_ref[0, :, lanes_of(h)]
        hmask = (lane >= hh * HEAD_DIM) & (lane < (hh + 1) * HEAD_DIM)
        qm = jnp.where(hmask, q2, jnp.zeros_like(q2))
        qm_ref[h] = qm
        gate = _nt_dot(km_ref[:, lanes_of(h)].astype(BF16), qm)
        for j in range(nb):
            gj = gate[j:j + 1, :]
            beats = ((gate > gj) | ((gate == gj) & (blk_row < j))) & valid
            cnt = jnp.sum(beats.astype(F32), axis=0, keepdims=True)
            cnt = cnt + jnp.where(j < i, 0.0, float(nb))
            madd_ref[h, j:j + 1, :] = jnp.where(cnt < MOBA_TOPK, 0.0, NEG)

    def key_blocks(blocks):
        for n, (key0, _, tile_bias, _, _) in enumerate(blocks):
            for h in range(N_HEADS):
                s = _nt_dot(kb_ref[pl.ds(key0, blk), lanes_of(h)], qm_ref[h])
                s_ref[n, h] = s if tile_bias is None else s + tile_bias(h)
        for n, (_, blk_idx, _, query_bias, first) in enumerate(blocks):
            for h in range(N_HEADS):
                s = s_ref[n, h]
                m_blk = jnp.max(s, axis=0, keepdims=True)
                if query_bias is not None:
                    qb = query_bias(h)
                    m_blk = m_blk + qb
                if first:
                    m_new = m_blk
                else:
                    m_old = m_ref[h]
                    m_new = jnp.maximum(m_old, m_blk)
                ref_row = m_new if query_bias is None else m_new - qb
                p = jnp.exp2(s - ref_row).astype(BF16)
                pv = _dot(vt_ref[blk_idx, h], p)
                acc_ref[h] = pv if first else jnp.exp2(m_old - m_new) * acc_ref[h] + pv
                m_ref[h] = m_new

    def far_block(j):
        return (pl.multiple_of(j * blk, blk), j, None,
                lambda h: madd_ref[h, pl.ds(j, 1), :] + tab_ref[h, N_BUCKETS - 1] * LOG2E, False)

    key_blocks([(own0, i, lambda h: bown_ref[h], None, True),
                (adj0, jm1, lambda h: badj_ref[h], lambda h: madd_ref[h, pl.ds(jm1, 1), :], False)])

    def far_pair(jj, carry):
        key_blocks([far_block(FAR_GROUP * jj + n) for n in range(FAR_GROUP)])
        return carry

    lax.fori_loop(0, jm1 // FAR_GROUP, far_pair, 0)
    for n in range(1, FAR_GROUP):
        @pl.when(jm1 % FAR_GROUP >= n)
        def _():
            key_blocks([far_block(jm1 - n)])

    for p in range(ATTN_WIDTH // LANES):
        halves = []
        for h in range(p * heads_per_vreg, (p + 1) * heads_per_vreg):
            acc = acc_ref[h]
            halves.append(acc[0:HEAD_DIM, :] / acc[HEAD_DIM:HEAD_DIM + 1, :])
        o_ref[0, :, p * LANES:(p + 1) * LANES] = jnp.concatenate(halves, axis=0).T.astype(BF16)


def _attn_prompt(tab_t, q, k, v, bias_own_t, bias_adj_t):
    b, s, a = q.shape
    blk = MOBA_BLOCK
    nb = s // blk
    nbp = 8
    assert s % blk == 0 and nb <= nbp
    full = pl.BlockSpec((1, s, a), lambda bi, i: (bi, 0, 0))
    tile = pl.BlockSpec((1, blk, a), lambda bi, i: (bi, i, 0))
    bias = pl.BlockSpec((N_HEADS, blk, blk), lambda bi, i: (0, 0, 0))
    return pl.pallas_call(
        functools.partial(_attn_prompt_kernel, nb=nb),
        grid=(b, nb),
        in_specs=[pl.BlockSpec(memory_space=pltpu.SMEM), tile, full, full, bias, bias],
        out_specs=tile,
        out_shape=jax.ShapeDtypeStruct((b, s, a), BF16),
        scratch_shapes=[pltpu.VMEM((s, a), BF16), pltpu.VMEM((nb, N_HEADS, VT_ROWS, blk), BF16),
                        pltpu.VMEM((nbp, a), F32), pltpu.VMEM((N_HEADS, blk, LANES), BF16),
                        pltpu.VMEM((N_HEADS, nbp, blk), F32), pltpu.VMEM((N_HEADS, 1, blk), F32),
                        pltpu.VMEM((N_HEADS, VT_ROWS, blk), F32),
                        pltpu.VMEM((FAR_GROUP, N_HEADS, blk, blk), F32)],
        compiler_params=pltpu.CompilerParams(dimension_semantics=("arbitrary", "arbitrary"),
                                             vmem_limit_bytes=VMEM_LIMIT),
        name="moba_prompt",
    )(tab_t, q, k, v, bias_own_t, bias_adj_t)


def _pool_rows(u, hist, w_ref, scale_ref, first_row_pos):
    tp = u.shape[0]
    ext = jnp.concatenate([hist, u], axis=0)
    top_pos = first_row_pos + lax.broadcasted_iota(jnp.int32, (POOL_HALO, 1), 0)
    outs = []
    for g, w in enumerate(POOL_WINDOWS):
        gs = slice(g * POOL_GROUP, (g + 1) * POOL_GROUP)
        s = ext[:, gs]
        shift = 1
        while shift < w:
            s = s + pltpu.roll(s, shift, 0)
            shift *= 2
        means = [s[POOL_HALO:2 * POOL_HALO, :] / jnp.minimum(top_pos + 1, w).astype(F32)]
        if tp > POOL_HALO:
            means.append(s[2 * POOL_HALO:, :] * (1.0 / w))
        d = jnp.concatenate(means, axis=0) - u[:, gs]
        outs.append((_dot(d.astype(BF16), w_ref[g]) * scale_ref[:, gs]).astype(BF16))
    return jnp.concatenate(outs, axis=1)


def _pool_kernel(u_ref, hist_ref, w_ref, scale_ref, o_ref, *, first_pos):
    i = pl.program_id(1)
    hist = jnp.where(i > 0, hist_ref[0], 0.0)
    o_ref[0] = _pool_rows(u_ref[0], hist, w_ref, scale_ref, first_pos + i * u_ref.shape[1])


def _halo_spec(tile_rows, width):
    per_tile = tile_rows // POOL_HALO
    return pl.BlockSpec((1, POOL_HALO, width), lambda b, i, *_: (b, jnp.maximum(i * per_tile - 1, 0), 0))


def _pool(u, w_pool_b, pool_scale, first_pos):
    nb, s, c = u.shape
    tp = _row_tile(s, (512, 256, 128, 64, 32, 16))
    return pl.pallas_call(
        functools.partial(_pool_kernel, first_pos=first_pos),
        grid=(nb, s // tp),
        in_specs=[pl.BlockSpec((1, tp, c), lambda b, i: (b, i, 0)),
                  _halo_spec(tp, c),
                  pl.BlockSpec(w_pool_b.shape, lambda b, i: (0, 0, 0)),
                  pl.BlockSpec((1, c), lambda b, i: (0, 0))],
        out_specs=pl.BlockSpec((1, tp, c), lambda b, i: (b, i, 0)),
        out_shape=jax.ShapeDtypeStruct((nb, s, c), BF16),
        compiler_params=pltpu.CompilerParams(dimension_semantics=("arbitrary", "arbitrary"),
                                             vmem_limit_bytes=VMEM_LIMIT),
        name="multi_pool",
    )(u, u, w_pool_b, pool_scale)


def _out_kernel(pt_ref, x_ref, attn_ref, pool_ref, mods_ref, gm_ref, gf_ref, wo_ref, wu_ref, wd_ref, *refs,
                rider):
    n_chunks = wu_ref.shape[1] // FF_CHUNK
    if rider is None:
        (y_ref,) = refs
        n_pages = 0
    else:
        ck_ref, y_ref, kmean_ref, pbuf, psem = refs
        n_pages, groups_per_seq, n_groups = rider
        step = pl.program_id(0) * pl.num_programs(1) + pl.program_id(1)
        last = pl.num_programs(0) * pl.num_programs(1) - 1
        slot = step % 2
        nxt = jnp.minimum(step + 1, last)

        def page_copies(st, sl, first, count):
            g = jnp.minimum(st, n_groups - 1)
            seq, part = g // groups_per_seq, g % groups_per_seq
            return [pltpu.make_async_copy(ck_ref.at[0, pt_ref[seq, part * n_pages + n]], pbuf.at[sl, n],
                                          psem.at[sl]) for n in range(first, first + count)]

        @pl.when(step == 0)
        def _():
            for c in page_copies(step, slot, 0, n_pages):
                c.start()

        for c in page_copies(nxt, 1 - slot, 0, n_pages):
            c.start()
        for c in page_copies(step, slot, 0, n_pages):
            c.wait()
    n_blocks = n_pages // PAGES_PER_BLOCK
    d = x_ref.shape[-1]
    a = ATTN_WIDTH
    x = x_ref[0]
    g1 = mods_ref[0, :, 2 * d:3 * d]
    sh2 = mods_ref[0, :, 3 * d:4 * d]
    sc2 = mods_ref[0, :, 4 * d:5 * d]
    g2 = mods_ref[0, :, 5 * d:6 * d]
    mix = _dot(attn_ref[0], wo_ref[0:a, :]) + _dot(pool_ref[0], wo_ref[a:, :])
    x1 = x + g1 * mix
    hb = _rms_modulated(x1, gm_ref[...], sc2, sh2).astype(BF16)
    acc = jnp.zeros(x.shape, F32)
    for c in range(n_chunks):
        cs = slice(c * FF_CHUNK, (c + 1) * FF_CHUNK)
        t = jnp.maximum(_dot(hb, wu_ref[:, cs]), 0.0)
        acc = acc + _dot((t * t).astype(BF16), wd_ref[cs, :])
        if rider is not None:
            for blk in range(c * n_blocks // n_chunks, (c + 1) * n_blocks // n_chunks):
                ksum = pbuf[slot, blk * PAGES_PER_BLOCK]
                for pg in range(1, PAGES_PER_BLOCK):
                    ksum = ksum + pbuf[slot, blk * PAGES_PER_BLOCK + pg]
                kmean_ref[0, blk] = jnp.sum(ksum, axis=-1) * (1.0 / MOBA_BLOCK)
    x2 = x1 + g2 * acc
    y_ref[0] = _rms(x2, gf_ref[...])
    if rider is not None:
        @pl.when(step == last)
        def _():
            for cp in page_copies(nxt, 1 - slot, 0, n_pages):
                cp.wait()


def _out(x, attn, pool, mods, g_mlp, g_final, w_out_b, w_up_b, w_down_b, page_table, cache_kt=None, tm=None):
    nb, s, d = x.shape
    r = mods.shape[1]
    tm = tm or _row_tile(s)
    n_tiles = s // tm
    row_spec = lambda w: pl.BlockSpec((1, tm, w), lambda b, i, pt: (b, i, 0))
    mods_spec = (pl.BlockSpec((1, 1, N_MOD * d), lambda b, i, pt: (b, 0, 0)) if r == 1
                 else pl.BlockSpec((1, tm, N_MOD * d), lambda b, i, pt: (b, i, 0)))
    resident = lambda w: pl.BlockSpec(w.shape, lambda b, i, pt: (0, 0), pipeline_mode=pl.Buffered(1))
    vec = pl.BlockSpec((1, d), lambda b, i, pt: (0, 0))
    out_specs, out_shape = [row_spec(d)], [jax.ShapeDtypeStruct((nb, s, d), F32)]
    cache_specs, cache_args, scratch, rider = [], [], [], None
    if cache_kt is not None:
        db, n_pages = page_table.shape
        nblk = n_pages // PAGES_PER_BLOCK
        need = -(-db * nblk // (nb * n_tiles))
        group = min(g for g in range(1, nblk + 1) if nblk % g == 0 and g >= need)
        groups_per_seq = nblk // group
        n_groups = db * groups_per_seq
        pps = group * PAGES_PER_BLOCK
        rider = (pps, groups_per_seq, n_groups)

        def group_of(b, i):
            g = jnp.minimum(b * n_tiles + i, n_groups - 1)
            return g // groups_per_seq, g % groups_per_seq

        cache_specs, cache_args = [pl.BlockSpec(memory_space=pl.ANY)], [cache_kt]
        scratch = [pltpu.VMEM((2, pps) + cache_kt.shape[2:], F32), pltpu.SemaphoreType.DMA((2,))]
        out_specs.append(pl.BlockSpec((1, group, N_HEADS, HEAD_DIM), lambda b, i, pt: group_of(b, i) + (0, 0)))
        out_shape.append(jax.ShapeDtypeStruct((db, nblk, N_HEADS, HEAD_DIM), F32))
    grid_spec = pltpu.PrefetchScalarGridSpec(
        num_scalar_prefetch=1,
        grid=(nb, n_tiles),
        in_specs=[row_spec(d), row_spec(ATTN_WIDTH), row_spec(POOL_WIDTH), mods_spec, vec, vec,
                  resident(w_out_b), resident(w_up_b), resident(w_down_b)] + cache_specs,
        out_specs=out_specs,
        scratch_shapes=scratch,
    )
    out = pl.pallas_call(
        functools.partial(_out_kernel, rider=rider),
        grid_spec=grid_spec,
        out_shape=out_shape,
        compiler_params=pltpu.CompilerParams(dimension_semantics=("arbitrary", "arbitrary"),
                                             vmem_limit_bytes=VMEM_LIMIT),
        name="out_mlp",
    )(page_table, x, attn, pool, mods, g_mlp, g_final, w_out_b, w_up_b, w_down_b, *cache_args)
    return out if cache_kt is not None else out[0]


def _select_kernel(q_ref, km_ref, sel_ref):
    rows = q_ref.shape[1]
    nblk = km_ref.shape[1]
    for sq in range(q_ref.shape[0]):
        for h in range(N_HEADS):
            q_h = q_ref[sq, :, h * HEAD_DIM:(h + 1) * HEAD_DIM].astype(F32)
            g = lax.dot_general(q_h, km_ref[sq, :, h, :], (((1,), (1,)), ((), ())),
                                precision=lax.Precision.HIGHEST, preferred_element_type=F32)
            lane = lax.broadcasted_iota(jnp.int32, g.shape, 1)
            out_lane = lax.broadcasted_iota(jnp.int32, (rows, LANES), 1)
            out = jnp.zeros((rows, LANES), jnp.int32)
            for n in range(MOBA_TOPK):
                mx = jnp.max(g, axis=-1, keepdims=True)
                idx = jnp.min(jnp.where(g == mx, lane, nblk), axis=-1, keepdims=True)
                out = jnp.where(out_lane == n, idx, out)
                g = jnp.where(lane == idx, -jnp.inf, g)
            sel_ref[sq, h * rows:(h + 1) * rows, :] = out


def _select(q_rows, kmean):
    db, rows, a = q_rows.shape
    nblk = kmean.shape[1]
    seqs = math.gcd(db, SELECT_SEQS_PER_STEP)
    return pl.pallas_call(
        _select_kernel,
        grid=(db // seqs,),
        in_specs=[pl.BlockSpec((seqs, rows, a), lambda b: (b, 0, 0)),
                  pl.BlockSpec((seqs,) + kmean.shape[1:], lambda b: (b, 0, 0, 0))],
        out_specs=pl.BlockSpec((seqs, N_HEADS * rows, LANES), lambda b: (b, 0, 0)),
        out_shape=jax.ShapeDtypeStruct((db, N_HEADS * rows, LANES), jnp.int32),
        compiler_params=pltpu.CompilerParams(dimension_semantics=("arbitrary",)),
        name="moba_select",
    )(q_rows, kmean)


def _attn_sample_kernel(pt_ref, sel_ref, tab_ref, q_ref, kn_ref, vn_ref, badj_ref, bown_ref, ck_ref, cv_ref,
                        o_ref, kbuf, vbuf, sem, *, nblk, t_new):
    b = pl.program_id(0)
    n_seq = pl.num_programs(0)
    n_sel = MOBA_TOPK
    per_head = t_new * n_sel
    buf = b % 2

    def block_of(sb, h, t, n):
        return sel_ref[((sb * N_HEADS + h) * t_new + t) * n_sel + n]

    def copies(sb, h, bf):
        out = []
        for t in range(t_new):
            for n in range(n_sel):
                j = block_of(sb, h, t, n)
                slot = (bf * N_HEADS + h) * per_head + t * n_sel + n
                for pg in range(PAGES_PER_BLOCK):
                    page = pt_ref[sb, j * PAGES_PER_BLOCK + pg]
                    keys = pl.ds(pg * PAGE_SIZE, PAGE_SIZE)
                    out.append(pltpu.make_async_copy(ck_ref.at[0, page, h], kbuf.at[slot, :, keys], sem.at[0, bf]))
                    out.append(pltpu.make_async_copy(cv_ref.at[0, page, h], vbuf.at[slot, :, keys], sem.at[1, bf]))
        return out

    @pl.when(b == 0)
    def _():
        for h in range(N_HEADS):
            for c in copies(b, h, buf):
                c.start()

    for h in range(N_HEADS):
        for c in copies(b, h, buf):
            c.wait()

    tok_row = lax.broadcasted_iota(jnp.int32, (SAMPLE_ROWS, MOBA_BLOCK), 0)
    ones_rows = jnp.ones((BF16_SUBLANES, MOBA_BLOCK), BF16)

    def head_attention(h):
        slot0 = (buf * N_HEADS + h) * per_head
        hl = slice(h * HEAD_DIM, (h + 1) * HEAD_DIM)
        c_far = tab_ref[h, N_BUCKETS - 1] * LOG2E
        q_h = q_ref[0, :, hl]
        scores = []
        for t in range(t_new):
            for n in range(n_sel):
                j = block_of(b, h, t, n)
                s = _dot(q_h, kbuf[slot0 + t * n_sel + n].astype(BF16))
                bias = jnp.where(j == nblk - 1, badj_ref[h], c_far)
                scores.append(jnp.where(tok_row == t, s + bias, NEG))
        s_own = _nt_dot(q_h, kn_ref[0, :, hl].astype(BF16)) + bown_ref[h, :, 0:SAMPLE_ROWS]
        m_tile = scores[0]
        for s in scores[1:]:
            m_tile = jnp.maximum(m_tile, s)
        m = jnp.maximum(jnp.max(m_tile, axis=-1, keepdims=True), jnp.max(s_own, axis=-1, keepdims=True))
        p_own = jnp.exp2(s_own - m).astype(BF16)
        v_own = jnp.concatenate([vn_ref[0, h].astype(BF16), ones_rows[:, :SAMPLE_ROWS]], axis=0)
        acc = _nt_dot(v_own, p_own)
        for i_blk, s in enumerate(scores):
            p = jnp.exp2(s - m).astype(BF16)
            v_blk = jnp.concatenate([vbuf[slot0 + i_blk].astype(BF16), ones_rows], axis=0)
            acc = acc + _nt_dot(v_blk, p)
        o_ref[0, h] = acc[0:HEAD_DIM, :] / acc[HEAD_DIM:HEAD_DIM + 1, :]

    nxt = jnp.minimum(b + 1, n_seq - 1)
    for h in range(N_HEADS):
        for c in copies(nxt, h, 1 - buf):
            c.start()
        head_attention(h)

    @pl.when(b == n_seq - 1)
    def _():
        for h in range(N_HEADS):
            for c in copies(nxt, h, 1 - buf):
                c.wait()


def _attn_sample(page_table, sel_flat, tab_t, q_rows, k_rows, v_cols, badj, bown, cache_kt, cache_vt, nblk, t_new):
    db, rows, a = q_rows.shape
    nh, dh = v_cols.shape[1:3]
    n_slots = 2 * nh * t_new * MOBA_TOPK
    grid_spec = pltpu.PrefetchScalarGridSpec(
        num_scalar_prefetch=2,
        grid=(db,),
        in_specs=[pl.BlockSpec(memory_space=pltpu.SMEM),
                  pl.BlockSpec((1, rows, a), lambda b, pt, sel: (b, 0, 0)),
                  pl.BlockSpec((1, rows, a), lambda b, pt, sel: (b, 0, 0)),
                  pl.BlockSpec((1, nh, dh, rows), lambda b, pt, sel: (b, 0, 0, 0)),
                  pl.BlockSpec(badj.shape, lambda b, pt, sel: (0, 0, 0)),
                  pl.BlockSpec(bown.shape, lambda b, pt, sel: (0, 0, 0)),
                  pl.BlockSpec(memory_space=pl.ANY), pl.BlockSpec(memory_space=pl.ANY)],
        out_specs=pl.BlockSpec((1, nh, dh, rows), lambda b, pt, sel: (b, 0, 0, 0)),
        scratch_shapes=[pltpu.VMEM((n_slots, dh, MOBA_BLOCK), F32),
                        pltpu.VMEM((n_slots, dh, MOBA_BLOCK), F32),
                        pltpu.SemaphoreType.DMA((2, 2))],
    )
    return pl.pallas_call(
        functools.partial(_attn_sample_kernel, nblk=nblk, t_new=t_new),
        grid_spec=grid_spec,
        out_shape=jax.ShapeDtypeStruct((db, nh, dh, rows), F32),
        compiler_params=pltpu.CompilerParams(dimension_semantics=("arbitrary",),
                                             vmem_limit_bytes=VMEM_LIMIT),
        name="moba_sample",
    )(page_table, sel_flat, tab_t, q_rows, k_rows, v_cols, badj, bown, cache_kt, cache_vt)


def kernel(x_prompt, x_sample, cache_k, cache_v, state_pool, page_table, c_prompt, c_sample, w_ada, b_ada, norm_mix, w_in, rel_bias, w_pool, pool_scale, w_out, norm_mlp, w_up, w_down, norm_final):
    assert w_ada.shape[0] == 1, "single-layer decoder"
    b, s, d = x_prompt.shape
    db, t_new, _ = x_sample.shape
    n_pages = page_table.shape[1]
    past = n_pages * PAGE_SIZE
    assert past % MOBA_BLOCK == 0 and past // MOBA_BLOCK >= MOBA_TOPK
    nblk = past // MOBA_BLOCK

    w_in_b = w_in[0].astype(BF16)
    w_out_b = w_out[0].astype(BF16)
    w_up_b = w_up[0].astype(BF16)
    w_down_b = w_down[0].astype(BF16)
    w_pool_b = w_pool[0].astype(BF16)
    g_final = norm_final.reshape(1, d)
    tab_t = rel_bias.T

    mods = _ada(jnp.concatenate([c_prompt, c_sample], axis=0), w_ada, b_ada[0:1])
    mods_p = mods[:b].reshape(b, 1, N_MOD * d)
    mods_s = jnp.repeat(mods[b:], t_new, axis=0).reshape(1, db * t_new, N_MOD * d)
    bias_own, bias_adj, badj_rows, bown_rows = _bias_tiles(tab_t)

    cache_kt = cache_k.transpose(0, 1, 3, 4, 2)
    cache_vt = cache_v.transpose(0, 1, 3, 4, 2)

    q_p, k_p, v_p, pool_p, u_tail = _inproj(x_prompt, mods_p, norm_mix, w_in_b, pool_params=(w_pool_b, pool_scale))
    attn_p = _attn_prompt(tab_t, q_p, k_p, v_p, bias_own, bias_adj)
    y_p, kmean = _out(x_prompt, attn_p, pool_p, mods_p, norm_mlp, g_final, w_out_b, w_up_b, w_down_b,
                      page_table, cache_kt=cache_kt, tm=_row_tile(s, (OUT_TILE_WITH_PAGES, 128, 64, 32, 16, 8)))

    xs = x_sample.reshape(1, db * t_new, d)
    q_s, k_s, v_s, u_s = _inproj(xs, mods_s, norm_mix, w_in_b)
    assert t_new <= SAMPLE_ROWS
    pad_rows = lambda z: jnp.pad(z.reshape(db, t_new, ATTN_WIDTH), ((0, 0), (0, SAMPLE_ROWS - t_new), (0, 0)))
    q_rows = pad_rows(q_s)
    sel = _select(q_rows, kmean).reshape(db, N_HEADS, SAMPLE_ROWS, LANES)[:, :, :t_new, :MOBA_TOPK]
    v_cols = pad_rows(v_s).reshape(db, SAMPLE_ROWS, N_HEADS, HEAD_DIM).transpose(0, 2, 3, 1)
    attn_s = _attn_sample(page_table, sel.reshape(-1), tab_t, q_rows, pad_rows(k_s), v_cols,
                          badj_rows, bown_rows, cache_kt, cache_vt, nblk, t_new)
    attn_s = attn_s[..., :t_new].transpose(0, 3, 1, 2).reshape(1, db * t_new, ATTN_WIDTH).astype(BF16)

    u_full = jnp.concatenate([state_pool[0], u_s.reshape(db, t_new, POOL_WIDTH)], axis=1)
    ext_rows = POOL_HALO + 8
    u_ext = jnp.pad(u_full, ((0, 0), (1, ext_rows - 1 - u_full.shape[1]), (0, 0)))
    pool_ext = _pool(u_ext.reshape(1, db * ext_rows, POOL_WIDTH), w_pool_b, pool_scale, first_pos=past)
    pool_s = pool_ext.reshape(db, ext_rows, POOL_WIDTH)[:, POOL_HALO:POOL_HALO + t_new]
    pool_s = pool_s.reshape(1, db * t_new, POOL_WIDTH)
    y_s = _out(xs, attn_s, pool_s, mods_s, norm_mlp, g_final, w_out_b, w_up_b, w_down_b, page_table)

    heads = lambda z, n, l: z.reshape(1, n, l, N_HEADS, HEAD_DIM)
    return (y_p, y_s.reshape(db, t_new, d),
            heads(k_p, b, s), heads(v_p, b, s), u_tail[:, -POOL_STATE:][None],
            heads(k_s, db, t_new), heads(v_s, db, t_new), u_full[:, -POOL_STATE:][None])
```

```python
import functools
import math

import numpy as np
import jax
import jax.numpy as jnp
from jax import lax
from jax.experimental import pallas as pl
from jax.experimental.pallas import tpu as pltpu

HEAD_DIM = 64
N_HEADS = 8
ATTN_WIDTH = N_HEADS * HEAD_DIM
POOL_WINDOWS = (2, 4, 8, 16)
POOL_GROUP = 128
POOL_WIDTH = POOL_GROUP * len(POOL_WINDOWS)
POOL_STATE = max(POOL_WINDOWS) - 1
POOL_HALO = POOL_STATE + 1
MOBA_BLOCK = 256
MOBA_TOPK = 3
N_BUCKETS = 32
MAX_DISTANCE = 128
PAGE_SIZE = 128
PAGES_PER_BLOCK = MOBA_BLOCK // PAGE_SIZE
N_MOD = 6
EPS = 1e-6
NEG = -1e30
SCALE = HEAD_DIM ** -0.5
LOG2E = math.log2(math.e)
Q_SCALE = SCALE * LOG2E
LANES = 128
BF16_SUBLANES = 16
VT_ROWS = HEAD_DIM + BF16_SUBLANES
FF_CHUNK = 1024
INPROJ_TILE = 1024
SAMPLE_ROWS = 8
SELECT_SEQS_PER_STEP = 8
FAR_GROUP = 2
OUT_TILE_WITH_PAGES = 256
VMEM_LIMIT = 56 * 1024 * 1024

BF16 = jnp.bfloat16
F32 = jnp.float32


def _nt_dot(a, b):
    return lax.dot_general(a, b, (((1,), (1,)), ((), ())), preferred_element_type=F32)


def _dot(a, b):
    return jnp.dot(a, b, preferred_element_type=F32)


def _row_tile(n, candidates=(512, 256, 128, 64, 32, 16, 8)):
    for c in candidates:
        if n % c == 0:
            return c
    raise ValueError(f"row count {n} is not a multiple of 8")


def _t5_bucket_np(rel):
    n = np.maximum(rel, 0)
    max_exact = N_BUCKETS // 2
    nf = np.maximum(n, max_exact).astype(np.float32)
    large = max_exact + (np.log(nf / np.float32(max_exact)) / np.float32(math.log(MAX_DISTANCE / max_exact))
                         * np.float32(N_BUCKETS - max_exact)).astype(np.int32)
    large = np.minimum(large, N_BUCKETS - 1)
    return np.where(n < max_exact, n, large).astype(np.int32)


def _rms(x, g):
    return x * lax.rsqrt(jnp.mean(x * x, axis=-1, keepdims=True) + EPS) * g


def _rms_modulated(x, g, scale, shift):
    return x * lax.rsqrt(jnp.mean(x * x, axis=-1, keepdims=True) + EPS) * (g * (1.0 + scale)) + shift


def _ada_kernel(c_ref, w_ref, b_ref, o_ref):
    c = c_ref[...]
    s = c / (1.0 + jnp.exp(-c))
    o_ref[...] = _dot(s.astype(BF16), w_ref[0].astype(BF16)) + b_ref[...]


def _ada(c_all, w_ada, b_ada):
    n, d = c_all.shape
    width = w_ada.shape[-1]
    tn = 1024
    return pl.pallas_call(
        _ada_kernel,
        grid=(width // tn,),
        in_specs=[pl.BlockSpec((n, d), lambda j: (0, 0)),
                  pl.BlockSpec((1, d, tn), lambda j: (0, 0, j)),
                  pl.BlockSpec((1, tn), lambda j: (0, j))],
        out_specs=pl.BlockSpec((n, tn), lambda j: (0, j)),
        out_shape=jax.ShapeDtypeStruct((n, width), F32),
        compiler_params=pltpu.CompilerParams(dimension_semantics=("arbitrary",), vmem_limit_bytes=VMEM_LIMIT),
        name="ada_mod",
    )(c_all, w_ada, b_ada)


def _inproj_kernel(x_ref, mods_ref, g_ref, w_ref, *refs, fused_pool):
    if fused_pool:
        wp_ref, ps_ref, q_ref, k_ref, v_ref, pool_ref, tail_ref, hist_ref = refs
        i = pl.program_id(1)

        @pl.when(i == 0)
        def _():
            hist_ref[...] = jnp.zeros(hist_ref.shape, F32)
    else:
        q_ref, k_ref, v_ref, u_ref = refs
    d = x_ref.shape[-1]
    ts = x_ref.shape[1]
    x = x_ref[0]
    shift = mods_ref[0, :, 0:d]
    scale = mods_ref[0, :, d:2 * d]
    hb = _rms_modulated(x, g_ref[...], scale, shift).astype(BF16)
    a = ATTN_WIDTH
    u = _dot(hb, w_ref[:, 3 * a:])
    r = _dot(hb, w_ref[:, 0:3 * a])
    q_ref[0] = (r[:, 0:a] * Q_SCALE).astype(BF16)
    k_ref[0] = r[:, a:2 * a]
    v_ref[0] = r[:, 2 * a:3 * a]
    if fused_pool:
        pool_ref[0] = _pool_rows(u, hist_ref[...], wp_ref, ps_ref, i * ts)
        hist_ref[...] = u[ts - POOL_HALO:, :]
        tail_ref[0] = u[ts - POOL_HALO:, :]
    else:
        u_ref[0] = u


def _inproj(x, mods, g, w_in_b, pool_params=None):
    nb, s, d = x.shape
    r = mods.shape[1]
    ts = _row_tile(s, (INPROJ_TILE, 512, 256, 128, 64, 32, 16, 8))
    width = w_in_b.shape[1]
    row_spec = lambda w: pl.BlockSpec((1, ts, w), lambda b, i: (b, i, 0))
    mods_spec = (pl.BlockSpec((1, 1, N_MOD * d), lambda b, i: (b, 0, 0)) if r == 1
                 else pl.BlockSpec((1, ts, N_MOD * d), lambda b, i: (b, i, 0)))
    in_specs = [row_spec(d), mods_spec, pl.BlockSpec((1, d), lambda b, i: (0, 0)),
                pl.BlockSpec((d, width), lambda b, i: (0, 0))]
    out_specs = [row_spec(ATTN_WIDTH)] * 3
    out_shape = [jax.ShapeDtypeStruct((nb, s, ATTN_WIDTH), BF16), jax.ShapeDtypeStruct((nb, s, ATTN_WIDTH), F32),
                 jax.ShapeDtypeStruct((nb, s, ATTN_WIDTH), F32)]
    args, scratch = [x, mods, g, w_in_b], []
    if pool_params is None:
        out_specs.append(row_spec(POOL_WIDTH))
        out_shape.append(jax.ShapeDtypeStruct((nb, s, POOL_WIDTH), F32))
    else:
        assert ts % POOL_HALO == 0
        w_pool_b, pool_scale = pool_params
        in_specs += [pl.BlockSpec(w_pool_b.shape, lambda b, i: (0, 0, 0)),
                     pl.BlockSpec((1, POOL_WIDTH), lambda b, i: (0, 0))]
        args += [w_pool_b, pool_scale]
        out_specs += [row_spec(POOL_WIDTH), pl.BlockSpec((1, POOL_HALO, POOL_WIDTH), lambda b, i: (b, 0, 0))]
        out_shape += [jax.ShapeDtypeStruct((nb, s, POOL_WIDTH), BF16),
                      jax.ShapeDtypeStruct((nb, POOL_HALO, POOL_WIDTH), F32)]
        scratch = [pltpu.VMEM((POOL_HALO, POOL_WIDTH), F32)]
    return pl.pallas_call(
        functools.partial(_inproj_kernel, fused_pool=pool_params is not None),
        grid=(nb, s // ts),
        in_specs=in_specs,
        out_specs=out_specs,
        out_shape=out_shape,
        scratch_shapes=scratch,
        compiler_params=pltpu.CompilerParams(dimension_semantics=("arbitrary", "arbitrary"),
                                             vmem_limit_bytes=VMEM_LIMIT),
        name="in_proj",
    )(*args)


def _bias_kernel(tab_ref, *refs):
    n = len(refs) // 2
    h = pl.program_id(0)
    for idx_ref, out_ref in zip(refs[:n], refs[n:]):
        idx = idx_ref[...]
        bias = jnp.zeros(idx.shape, F32)
        for b in range(N_BUCKETS):
            bias = jnp.where(idx == b, tab_ref[h, b], bias)
        out_ref[0] = jnp.where(idx < 0, NEG, bias * LOG2E)


def _bias_tiles(tab_t):
    blk = MOBA_BLOCK
    r = np.arange(blk)[None, :]
    c = np.arange(blk)[:, None]
    rows = np.arange(SAMPLE_ROWS)[:, None]
    keys = np.arange(blk)[None, :]
    own_keys = np.arange(LANES)[None, :]
    idx = [np.where(r >= c, _t5_bucket_np(r - c), -1), _t5_bucket_np(blk + r - c),
           _t5_bucket_np(blk + rows - keys), np.where(rows >= own_keys, _t5_bucket_np(rows - own_keys), -1)]
    idx = [jnp.asarray(t.astype(np.int32)) for t in idx]
    return pl.pallas_call(
        _bias_kernel,
        grid=(N_HEADS,),
        in_specs=[pl.BlockSpec(memory_space=pltpu.SMEM)] + [pl.BlockSpec(t.shape, lambda h: (0, 0)) for t in idx],
        out_specs=[pl.BlockSpec((1,) + t.shape, lambda h: (h, 0, 0)) for t in idx],
        out_shape=[jax.ShapeDtypeStruct((N_HEADS,) + t.shape, F32) for t in idx],
        compiler_params=pltpu.CompilerParams(dimension_semantics=("arbitrary",)),
        name="rel_bias_tiles",
    )(tab_t, *idx)


def _attn_prompt_kernel(tab_ref, q_ref, k_ref, v_ref, bown_ref, badj_ref, o_ref,
                        kb_ref, vt_ref, km_ref, qm_ref, madd_ref, m_ref, acc_ref, s_ref, *, nb):
    i = pl.program_id(1)
    blk = MOBA_BLOCK
    nbp = km_ref.shape[0]

    @pl.when(i == 0)
    def _():
        kb_ref[...] = k_ref[0].astype(BF16)
        ones = jnp.ones((VT_ROWS - HEAD_DIM, blk), BF16)
        for j in range(nb):
            vt = v_ref[0, j * blk:(j + 1) * blk, :].T.astype(BF16)
            for h in range(N_HEADS):
                vt_ref[j, h, 0:HEAD_DIM, :] = vt[h * HEAD_DIM:(h + 1) * HEAD_DIM, :]
                vt_ref[j, h, HEAD_DIM:, :] = ones
        means = [jnp.mean(k_ref[0, j * blk:(j + 1) * blk, :], axis=0, keepdims=True) for j in range(nb)]
        means += [jnp.zeros_like(means[0])] * (nbp - nb)
        km_ref[...] = jnp.concatenate(means, axis=0)

    lane = lax.broadcasted_iota(jnp.int32, (blk, LANES), 1)
    blk_row = lax.broadcasted_iota(jnp.int32, (nbp, blk), 0)
    valid = blk_row < i
    jm1 = jnp.maximum(i - 1, 0)
    own0 = pl.multiple_of(i * blk, blk)
    adj0 = pl.multiple_of(jm1 * blk, blk)
    heads_per_vreg = LANES // HEAD_DIM

    def lanes_of(h):
        p = h // heads_per_vreg
        return slice(p * LANES, (p + 1) * LANES)

    for h in range(N_HEADS):
        hh = h % heads_per_vreg
        q2 = q_ref[0, :, lanes_of(h)]
        hmask = (lane >= hh * HEAD_DIM) & (lane < (hh + 1) * HEAD_DIM)
        qm = jnp.where(hmask, q2, jnp.zeros_like(q2))
        qm_ref[h] = qm
        gate = _nt_dot(km_ref[:, lanes_of(h)].astype(BF16), qm)
        for j in range(nb):
            gj = gate[j:j + 1, :]
            beats = ((gate > gj) | ((gate == gj) & (blk_row < j))) & valid
            cnt = jnp.sum(beats.astype(F32), axis=0, keepdims=True)
            cnt = cnt + jnp.where(j < i, 0.0, float(nb))
            madd_ref[h, j:j + 1, :] = jnp.where(cnt < MOBA_TOPK, 0.0, NEG)

    def key_blocks(blocks):
        for n, (key0, _, tile_bias, _, _) in enumerate(blocks):
            for h in range(N_HEADS):
                s = _nt_dot(kb_ref[pl.ds(key0, blk), lanes_of(h)], qm_ref[h])
                s_ref[n, h] = s if tile_bias is None else s + tile_bias(h)
        for n, (_, blk_idx, _, query_bias, first) in enumerate(blocks):
            for h in range(N_HEADS):
                s = s_ref[n, h]
                m_blk = jnp.max(s, axis=0, keepdims=True)
                if query_bias is not None:
                    qb = query_bias(h)
                    m_blk = m_blk + qb
                if first:
                    m_new = m_blk
                else:
                    m_old = m_ref[h]
                    m_new = jnp.maximum(m_old, m_blk)
                ref_row = m_new if query_bias is None else m_new - qb
                p = jnp.exp2(s - ref_row).astype(BF16)
                pv = _dot(vt_ref[blk_idx, h], p)
                acc_ref[h] = pv if first else jnp.exp2(m_old - m_new) * acc_ref[h] + pv
                m_ref[h] = m_new

    def far_block(j):
        return (pl.multiple_of(j * blk, blk), j, None,
                lambda h: madd_ref[h, pl.ds(j, 1), :] + tab_ref[h, N_BUCKETS - 1] * LOG2E, False)

    key_blocks([(own0, i, lambda h: bown_ref[h], None, True),
                (adj0, jm1, lambda h: badj_ref[h], lambda h: madd_ref[h, pl.ds(jm1, 1), :], False)])

    def far_pair(jj, carry):
        key_blocks([far_block(FAR_GROUP * jj + n) for n in range(FAR_GROUP)])
        return carry

    lax.fori_loop(0, jm1 // FAR_GROUP, far_pair, 0)
    for n in range(1, FAR_GROUP):
        @pl.when(jm1 % FAR_GROUP >= n)
        def _():
            key_blocks([far_block(jm1 - n)])

    for p in range(ATTN_WIDTH // LANES):
        halves = []
        for h in range(p * heads_per_vreg, (p + 1) * heads_per_vreg):
            acc = acc_ref[h]
            halves.append(acc[0:HEAD_DIM, :] / acc[HEAD_DIM:HEAD_DIM + 1, :])
        o_ref[0, :, p * LANES:(p + 1) * LANES] = jnp.concatenate(halves, axis=0).T.astype(BF16)


def _attn_prompt(tab_t, q, k, v, bias_own_t, bias_adj_t):
    b, s, a = q.shape
    blk = MOBA_BLOCK
    nb = s // blk
    nbp = 8
    assert s % blk == 0 and nb <= nbp
    full = pl.BlockSpec((1, s, a), lambda bi, i: (bi, 0, 0))
    tile = pl.BlockSpec((1, blk, a), lambda bi, i: (bi, i, 0))
    bias = pl.BlockSpec((N_HEADS, blk, blk), lambda bi, i: (0, 0, 0))
    return pl.pallas_call(
        functools.partial(_attn_prompt_kernel, nb=nb),
        grid=(b, nb),
        in_specs=[pl.BlockSpec(memory_space=pltpu.SMEM), tile, full, full, bias, bias],
        out_specs=tile,
        out_shape=jax.ShapeDtypeStruct((b, s, a), BF16),
        scratch_shapes=[pltpu.VMEM((s, a), BF16), pltpu.VMEM((nb, N_HEADS, VT_ROWS, blk), BF16),
                        pltpu.VMEM((nbp, a), F32), pltpu.VMEM((N_HEADS, blk, LANES), BF16),
                        pltpu.VMEM((N_HEADS, nbp, blk), F32), pltpu.VMEM((N_HEADS, 1, blk), F32),
                        pltpu.VMEM((N_HEADS, VT_ROWS, blk), F32),
                        pltpu.VMEM((FAR_GROUP, N_HEADS, blk, blk), F32)],
        compiler_params=pltpu.CompilerParams(dimension_semantics=("arbitrary", "arbitrary"),
                                             vmem_limit_bytes=VMEM_LIMIT),
        name="moba_prompt",
    )(tab_t, q, k, v, bias_own_t, bias_adj_t)


def _pool_rows(u, hist, w_ref, scale_ref, first_row_pos):
    tp = u.shape[0]
    ext = jnp.concatenate([hist, u], axis=0)
    top_pos = first_row_pos + lax.broadcasted_iota(jnp.int32, (POOL_HALO, 1), 0)
    outs = []
    for g, w in enumerate(POOL_WINDOWS):
        gs = slice(g * POOL_GROUP, (g + 1) * POOL_GROUP)
        s = ext[:, gs]
        shift = 1
        while shift < w:
            s = s + pltpu.roll(s, shift, 0)
            shift *= 2
        means = [s[POOL_HALO:2 * POOL_HALO, :] / jnp.minimum(top_pos + 1, w).astype(F32)]
        if tp > POOL_HALO:
            means.append(s[2 * POOL_HALO:, :] * (1.0 / w))
        d = jnp.concatenate(means, axis=0) - u[:, gs]
        outs.append((_dot(d.astype(BF16), w_ref[g]) * scale_ref[:, gs]).astype(BF16))
    return jnp.concatenate(outs, axis=1)


def _pool_kernel(u_ref, hist_ref, w_ref, scale_ref, o_ref, *, first_pos):
    i = pl.program_id(1)
    hist = jnp.where(i > 0, hist_ref[0], 0.0)
    o_ref[0] = _pool_rows(u_ref[0], hist, w_ref, scale_ref, first_pos + i * u_ref.shape[1])


def _halo_spec(tile_rows, width):
    per_tile = tile_rows // POOL_HALO
    return pl.BlockSpec((1, POOL_HALO, width), lambda b, i, *_: (b, jnp.maximum(i * per_tile - 1, 0), 0))


def _pool(u, w_pool_b, pool_scale, first_pos):
    nb, s, c = u.shape
    tp = _row_tile(s, (512, 256, 128, 64, 32, 16))
    return pl.pallas_call(
        functools.partial(_pool_kernel, first_pos=first_pos),
        grid=(nb, s // tp),
        in_specs=[pl.BlockSpec((1, tp, c), lambda b, i: (b, i, 0)),
                  _halo_spec(tp, c),
                  pl.BlockSpec(w_pool_b.shape, lambda b, i: (0, 0, 0)),
                  pl.BlockSpec((1, c), lambda b, i: (0, 0))],
        out_specs=pl.BlockSpec((1, tp, c), lambda b, i: (b, i, 0)),
        out_shape=jax.ShapeDtypeStruct((nb, s, c), BF16),
        compiler_params=pltpu.CompilerParams(dimension_semantics=("arbitrary", "arbitrary"),
                                             vmem_limit_bytes=VMEM_LIMIT),
        name="multi_pool",
    )(u, u, w_pool_b, pool_scale)


def _out_kernel(pt_ref, x_ref, attn_ref, pool_ref, mods_ref, gm_ref, gf_ref, wo_ref, wu_ref, wd_ref, *refs,
                rider):
    n_chunks = wu_ref.shape[1] // FF_CHUNK
    if rider is None:
        (y_ref,) = refs
        n_pages = 0
    else:
        ck_ref, y_ref, kmean_ref, pbuf, psem = refs
        n_pages, groups_per_seq, n_groups = rider
        step = pl.program_id(0) * pl.num_programs(1) + pl.program_id(1)
        last = pl.num_programs(0) * pl.num_programs(1) - 1
        slot = step % 2
        nxt = jnp.minimum(step + 1, last)

        def page_copies(st, sl, first, count):
            g = jnp.minimum(st, n_groups - 1)
            seq, part = g // groups_per_seq, g % groups_per_seq
            return [pltpu.make_async_copy(ck_ref.at[0, pt_ref[seq, part * n_pages + n]], pbuf.at[sl, n],
                                          psem.at[sl]) for n in range(first, first + count)]

        @pl.when(step == 0)
        def _():
            for c in page_copies(step, slot, 0, n_pages):
                c.start()

        for c in page_copies(nxt, 1 - slot, 0, n_pages):
            c.start()
        for c in page_copies(step, slot, 0, n_pages):
            c.wait()
    n_blocks = n_pages // PAGES_PER_BLOCK
    d = x_ref.shape[-1]
    a = ATTN_WIDTH
    x = x_ref[0]
    g1 = mods_ref[0, :, 2 * d:3 * d]
    sh2 = mods_ref[0, :, 3 * d:4 * d]
    sc2 = mods_ref[0, :, 4 * d:5 * d]
    g2 = mods_ref[0, :, 5 * d:6 * d]
    mix = _dot(attn_ref[0], wo_ref[0:a, :]) + _dot(pool_ref[0], wo_ref[a:, :])
    x1 = x + g1 * mix
    hb = _rms_modulated(x1, gm_ref[...], sc2, sh2).astype(BF16)
    acc = jnp.zeros(x.shape, F32)
    for c in range(n_chunks):
        cs = slice(c * FF_CHUNK, (c + 1) * FF_CHUNK)
        t = jnp.maximum(_dot(hb, wu_ref[:, cs]), 0.0)
        acc = acc + _dot((t * t).astype(BF16), wd_ref[cs, :])
        if rider is not None:
            for blk in range(c * n_blocks // n_chunks, (c + 1) * n_blocks // n_chunks):
                ksum = pbuf[slot, blk * PAGES_PER_BLOCK]
                for pg in range(1, PAGES_PER_BLOCK):
                    ksum = ksum + pbuf[slot, blk * PAGES_PER_BLOCK + pg]
                kmean_ref[0, blk] = jnp.sum(ksum, axis=-1) * (1.0 / MOBA_BLOCK)
    x2 = x1 + g2 * acc
    y_ref[0] = _rms(x2, gf_ref[...])
    if rider is not None:
        @pl.when(step == last)
        def _():
            for cp in page_copies(nxt, 1 - slot, 0, n_pages):
                cp.wait()


def _out(x, attn, pool, mods, g_mlp, g_final, w_out_b, w_up_b, w_down_b, page_table, cache_kt=None, tm=None):
    nb, s, d = x.shape
    r = mods.shape[1]
    tm = tm or _row_tile(s)
    n_tiles = s // tm
    row_spec = lambda w: pl.BlockSpec((1, tm, w), lambda b, i, pt: (b, i, 0))
    mods_spec = (pl.BlockSpec((1, 1, N_MOD * d), lambda b, i, pt: (b, 0, 0)) if r == 1
                 else pl.BlockSpec((1, tm, N_MOD * d), lambda b, i, pt: (b, i, 0)))
    resident = lambda w: pl.BlockSpec(w.shape, lambda b, i, pt: (0, 0), pipeline_mode=pl.Buffered(1))
    vec = pl.BlockSpec((1, d), lambda b, i, pt: (0, 0))
    out_specs, out_shape = [row_spec(d)], [jax.ShapeDtypeStruct((nb, s, d), F32)]
    cache_specs, cache_args, scratch, rider = [], [], [], None
    if cache_kt is not None:
        db, n_pages = page_table.shape
        nblk = n_pages // PAGES_PER_BLOCK
        need = -(-db * nblk // (nb * n_tiles))
        group = min(g for g in range(1, nblk + 1) if nblk % g == 0 and g >= need)
        groups_per_seq = nblk // group
        n_groups = db * groups_per_seq
        pps = group * PAGES_PER_BLOCK
        rider = (pps, groups_per_seq, n_groups)

        def group_of(b, i):
            g = jnp.minimum(b * n_tiles + i, n_groups - 1)
            return g // groups_per_seq, g % groups_per_seq

        cache_specs, cache_args = [pl.BlockSpec(memory_space=pl.ANY)], [cache_kt]
        scratch = [pltpu.VMEM((2, pps) + cache_kt.shape[2:], F32), pltpu.SemaphoreType.DMA((2,))]
        out_specs.append(pl.BlockSpec((1, group, N_HEADS, HEAD_DIM), lambda b, i, pt: group_of(b, i) + (0, 0)))
        out_shape.append(jax.ShapeDtypeStruct((db, nblk, N_HEADS, HEAD_DIM), F32))
    grid_spec = pltpu.PrefetchScalarGridSpec(
        num_scalar_prefetch=1,
        grid=(nb, n_tiles),
        in_specs=[row_spec(d), row_spec(ATTN_WIDTH), row_spec(POOL_WIDTH), mods_spec, vec, vec,
                  resident(w_out_b), resident(w_up_b), resident(w_down_b)] + cache_specs,
        out_specs=out_specs,
        scratch_shapes=scratch,
    )
    out = pl.pallas_call(
        functools.partial(_out_kernel, rider=rider),
        grid_spec=grid_spec,
        out_shape=out_shape,
        compiler_params=pltpu.CompilerParams(dimension_semantics=("arbitrary", "arbitrary"),
                                             vmem_limit_bytes=VMEM_LIMIT),
        name="out_mlp",
    )(page_table, x, attn, pool, mods, g_mlp, g_final, w_out_b, w_up_b, w_down_b, *cache_args)
    return out if cache_kt is not None else out[0]


def _select_kernel(q_ref, km_ref, sel_ref):
    rows = q_ref.shape[1]
    nblk = km_ref.shape[1]
    for sq in range(q_ref.shape[0]):
        for h in range(N_HEADS):
            q_h = q_ref[sq, :, h * HEAD_DIM:(h + 1) * HEAD_DIM].astype(F32)
            g = lax.dot_general(q_h, km_ref[sq, :, h, :], (((1,), (1,)), ((), ())),
                                precision=lax.Precision.HIGHEST, preferred_element_type=F32)
            lane = lax.broadcasted_iota(jnp.int32, g.shape, 1)
            out_lane = lax.broadcasted_iota(jnp.int32, (rows, LANES), 1)
            out = jnp.zeros((rows, LANES), jnp.int32)
            for n in range(MOBA_TOPK):
                mx = jnp.max(g, axis=-1, keepdims=True)
                idx = jnp.min(jnp.where(g == mx, lane, nblk), axis=-1, keepdims=True)
                out = jnp.where(out_lane == n, idx, out)
                g = jnp.where(lane == idx, -jnp.inf, g)
            sel_ref[sq, h * rows:(h + 1) * rows, :] = out


def _select(q_rows, kmean):
    db, rows, a = q_rows.shape
    nblk = kmean.shape[1]
    seqs = math.gcd(db, SELECT_SEQS_PER_STEP)
    return pl.pallas_call(
        _select_kernel,
        grid=(db // seqs,),
        in_specs=[pl.BlockSpec((seqs, rows, a), lambda b: (b, 0, 0)),
                  pl.BlockSpec((seqs,) + kmean.shape[1:], lambda b: (b, 0, 0, 0))],
        out_specs=pl.BlockSpec((seqs, N_HEADS * rows, LANES), lambda b: (b, 0, 0)),
        out_shape=jax.ShapeDtypeStruct((db, N_HEADS * rows, LANES), jnp.int32),
        compiler_params=pltpu.CompilerParams(dimension_semantics=("arbitrary",)),
        name="moba_select",
    )(q_rows, kmean)


def _attn_sample_kernel(pt_ref, sel_ref, tab_ref, q_ref, kn_ref, vn_ref, badj_ref, bown_ref, ck_ref, cv_ref,
                        o_ref, kbuf, vbuf, sem, *, nblk, t_new):
    b = pl.program_id(0)
    n_seq = pl.num_programs(0)
    n_sel = MOBA_TOPK
    per_head = t_new * n_sel
    buf = b % 2

    def block_of(sb, h, t, n):
        return sel_ref[((sb * N_HEADS + h) * t_new + t) * n_sel + n]

    def copies(sb, h, bf):
        out = []
        for t in range(t_new):
            for n in range(n_sel):
                j = block_of(sb, h, t, n)
                slot = (bf * N_HEADS + h) * per_head + t * n_sel + n
                for pg in range(PAGES_PER_BLOCK):
                    page = pt_ref[sb, j * PAGES_PER_BLOCK + pg]
                    keys = pl.ds(pg * PAGE_SIZE, PAGE_SIZE)
                    out.append(pltpu.make_async_copy(ck_ref.at[0, page, h], kbuf.at[slot, :, keys], sem.at[0, bf]))
                    out.append(pltpu.make_async_copy(cv_ref.at[0, page, h], vbuf.at[slot, :, keys], sem.at[1, bf]))
        return out

    @pl.when(b == 0)
    def _():
        for h in range(N_HEADS):
            for c in copies(b, h, buf):
                c.start()

    for h in range(N_HEADS):
        for c in copies(b, h, buf):
            c.wait()

    tok_row = lax.broadcasted_iota(jnp.int32, (SAMPLE_ROWS, MOBA_BLOCK), 0)
    ones_rows = jnp.ones((BF16_SUBLANES, MOBA_BLOCK), BF16)

    def head_attention(h):
        slot0 = (buf * N_HEADS + h) * per_head
        hl = slice(h * HEAD_DIM, (h + 1) * HEAD_DIM)
        c_far = tab_ref[h, N_BUCKETS - 1] * LOG2E
        q_h = q_ref[0, :, hl]
        scores = []
        for t in range(t_new):
            for n in range(n_sel):
                j = block_of(b, h, t, n)
                s = _dot(q_h, kbuf[slot0 + t * n_sel + n].astype(BF16))
                bias = jnp.where(j == nblk - 1, badj_ref[h], c_far)
                scores.append(jnp.where(tok_row == t, s + bias, NEG))
        s_own = _nt_dot(q_h, kn_ref[0, :, hl].astype(BF16)) + bown_ref[h, :, 0:SAMPLE_ROWS]
        m_tile = scores[0]
        for s in scores[1:]:
            m_tile = jnp.maximum(m_tile, s)
        m = jnp.maximum(jnp.max(m_tile, axis=-1, keepdims=True), jnp.max(s_own, axis=-1, keepdims=True))
        p_own = jnp.exp2(s_own - m).astype(BF16)
        v_own = jnp.concatenate([vn_ref[0, h].astype(BF16), ones_rows[:, :SAMPLE_ROWS]], axis=0)
        acc = _nt_dot(v_own, p_own)
        for i_blk, s in enumerate(scores):
            p = jnp.exp2(s - m).astype(BF16)
            v_blk = jnp.concatenate([vbuf[slot0 + i_blk].astype(BF16), ones_rows], axis=0)
            acc = acc + _nt_dot(v_blk, p)
        o_ref[0, h] = acc[0:HEAD_DIM, :] / acc[HEAD_DIM:HEAD_DIM + 1, :]

    nxt = jnp.minimum(b + 1, n_seq - 1)
    for h in range(N_HEADS):
        for c in copies(nxt, h, 1 - buf):
            c.start()
        head_attention(h)

    @pl.when(b == n_seq - 1)
    def _():
        for h in range(N_HEADS):
            for c in copies(nxt, h, 1 - buf):
                c.wait()


def _attn_sample(page_table, sel_flat, tab_t, q_rows, k_rows, v_cols, badj, bown, cache_kt, cache_vt, nblk, t_new):
    db, rows, a = q_rows.shape
    nh, dh = v_cols.shape[1:3]
    n_slots = 2 * nh * t_new * MOBA_TOPK
    grid_spec = pltpu.PrefetchScalarGridSpec(
        num_scalar_prefetch=2,
        grid=(db,),
        in_specs=[pl.BlockSpec(memory_space=pltpu.SMEM),
                  pl.BlockSpec((1, rows, a), lambda b, pt, sel: (b, 0, 0)),
                  pl.BlockSpec((1, rows, a), lambda b, pt, sel: (b, 0, 0)),
                  pl.BlockSpec((1, nh, dh, rows), lambda b, pt, sel: (b, 0, 0, 0)),
                  pl.BlockSpec(badj.shape, lambda b, pt, sel: (0, 0, 0)),
                  pl.BlockSpec(bown.shape, lambda b, pt, sel: (0, 0, 0)),
                  pl.BlockSpec(memory_space=pl.ANY), pl.BlockSpec(memory_space=pl.ANY)],
        out_specs=pl.BlockSpec((1, nh, dh, rows), lambda b, pt, sel: (b, 0, 0, 0)),
        scratch_shapes=[pltpu.VMEM((n_slots, dh, MOBA_BLOCK), F32),
                        pltpu.VMEM((n_slots, dh, MOBA_BLOCK), F32),
                        pltpu.SemaphoreType.DMA((2, 2))],
    )
    return pl.pallas_call(
        functools.partial(_attn_sample_kernel, nblk=nblk, t_new=t_new),
        grid_spec=grid_spec,
        out_shape=jax.ShapeDtypeStruct((db, nh, dh, rows), F32),
        compiler_params=pltpu.CompilerParams(dimension_semantics=("arbitrary",),
                                             vmem_limit_bytes=VMEM_LIMIT),
        name="moba_sample",
    )(page_table, sel_flat, tab_t, q_rows, k_rows, v_cols, badj, bown, cache_kt, cache_vt)


def kernel(x_prompt, x_sample, cache_k, cache_v, state_pool, page_table, c_prompt, c_sample, w_ada, b_ada, norm_mix, w_in, rel_bias, w_pool, pool_scale, w_out, norm_mlp, w_up, w_down, norm_final):
    assert w_ada.shape[0] == 1, "single-layer decoder"
    b, s, d = x_prompt.shape
    db, t_new, _ = x_sample.shape
    n_pages = page_table.shape[1]
    past = n_pages * PAGE_SIZE
    assert past % MOBA_BLOCK == 0 and past // MOBA_BLOCK >= MOBA_TOPK
    nblk = past // MOBA_BLOCK

    w_in_b = w_in[0].astype(BF16)
    w_out_b = w_out[0].astype(BF16)
    w_up_b = w_up[0].astype(BF16)
    w_down_b = w_down[0].astype(BF16)
    w_pool_b = w_pool[0].astype(BF16)
    g_final = norm_final.reshape(1, d)
    tab_t = rel_bias.T

    mods = _ada(jnp.concatenate([c_prompt, c_sample], axis=0), w_ada, b_ada[0:1])
    mods_p = mods[:b].reshape(b, 1, N_MOD * d)
    mods_s = jnp.repeat(mods[b:], t_new, axis=0).reshape(1, db * t_new, N_MOD * d)
    bias_own, bias_adj, badj_rows, bown_rows = _bias_tiles(tab_t)

    cache_kt = cache_k.transpose(0, 1, 3, 4, 2)
    cache_vt = cache_v.transpose(0, 1, 3, 4, 2)

    q_p, k_p, v_p, pool_p, u_tail = _inproj(x_prompt, mods_p, norm_mix, w_in_b, pool_params=(w_pool_b, pool_scale))
    attn_p = _attn_prompt(tab_t, q_p, k_p, v_p, bias_own, bias_adj)
    y_p, kmean = _out(x_prompt, attn_p, pool_p, mods_p, norm_mlp, g_final, w_out_b, w_up_b, w_down_b,
                      page_table, cache_kt=cache_kt, tm=_row_tile(s, (OUT_TILE_WITH_PAGES, 128, 64, 32, 16, 8)))

    xs = x_sample.reshape(1, db * t_new, d)
    q_s, k_s, v_s, u_s = _inproj(xs, mods_s, norm_mix, w_in_b)
    assert t_new <= SAMPLE_ROWS
    pad_rows = lambda z: jnp.pad(z.reshape(db, t_new, ATTN_WIDTH), ((0, 0), (0, SAMPLE_ROWS - t_new), (0, 0)))
    q_rows = pad_rows(q_s)
    sel = _select(q_rows, kmean).reshape(db, N_HEADS, SAMPLE_ROWS, LANES)[:, :, :t_new, :MOBA_TOPK]
    v_cols = pad_rows(v_s).reshape(db, SAMPLE_ROWS, N_HEADS, HEAD_DIM).transpose(0, 2, 3, 1)
    attn_s = _attn_sample(page_table, sel.reshape(-1), tab_t, q_rows, pad_rows(k_s), v_cols,
                          badj_rows, bown_rows, cache_kt, cache_vt, nblk, t_new)
    attn_s = attn_s[..., :t_new].transpose(0, 3, 1, 2).reshape(1, db * t_new, ATTN_WIDTH).astype(BF16)

    u_full = jnp.concatenate([state_pool[0], u_s.reshape(db, t_new, POOL_WIDTH)], axis=1)
    ext_rows = POOL_HALO + 8
    u_ext = jnp.pad(u_full, ((0, 0), (1, ext_rows - 1 - u_full.shape[1]), (0, 0)))
    pool_ext = _pool(u_ext.reshape(1, db * ext_rows, POOL_WIDTH), w_pool_b, pool_scale, first_pos=past)
    pool_s = pool_ext.reshape(db, ext_rows, POOL_WIDTH)[:, POOL_HALO:POOL_HALO + t_new]
    pool_s = pool_s.reshape(1, db * t_new, POOL_WIDTH)
    y_s = _out(xs, attn_s, pool_s, mods_s, norm_mlp, g_final, w_out_b, w_up_b, w_down_b, page_table)

    heads = lambda z, n, l: z.reshape(1, n, l, N_HEADS, HEAD_DIM)
    return (y_p, y_s.reshape(db, t_new, d),
            heads(k_p, b, s), heads(v_p, b, s), u_tail[:, -POOL_STATE:][None],
            heads(k_s, db, t_new), heads(v_s, db, t_new), u_full[:, -POOL_STATE:][None])
```

```python
import functools
import math

import numpy as np
import jax
import jax.numpy as jnp
from jax import lax
from jax.experimental import pallas as pl
from jax.experimental.pallas import tpu as pltpu

HEAD_DIM = 64
N_HEADS = 8
ATTN_WIDTH = N_HEADS * HEAD_DIM
POOL_WINDOWS = (2, 4, 8, 16)
POOL_GROUP = 128
POOL_WIDTH = POOL_GROUP * len(POOL_WINDOWS)
POOL_STATE = max(POOL_WINDOWS) - 1
POOL_HALO = POOL_STATE + 1
MOBA_BLOCK = 256
MOBA_TOPK = 3
N_BUCKETS = 32
MAX_DISTANCE = 128
PAGE_SIZE = 128
PAGES_PER_BLOCK = MOBA_BLOCK // PAGE_SIZE
N_MOD = 6
EPS = 1e-6
NEG = -1e30
SCALE = HEAD_DIM ** -0.5
LOG2E = math.log2(math.e)
Q_SCALE = SCALE * LOG2E
LANES = 128
BF16_SUBLANES = 16
VT_ROWS = HEAD_DIM + BF16_SUBLANES
FF_CHUNK = 1024
INPROJ_TILE = 1024
SAMPLE_ROWS = 8
SELECT_SEQS_PER_STEP = 8
FAR_GROUP = 2
OUT_TILE_WITH_PAGES = 256
VMEM_LIMIT = 56 * 1024 * 1024

BF16 = jnp.bfloat16
F32 = jnp.float32


def _nt_dot(a, b):
    return lax.dot_general(a, b, (((1,), (1,)), ((), ())), preferred_element_type=F32)


def _dot(a, b):
    return jnp.dot(a, b, preferred_element_type=F32)


def _row_tile(n, candidates=(512, 256, 128, 64, 32, 16, 8)):
    for c in candidates:
        if n % c == 0:
            return c
    raise ValueError(f"row count {n} is not a multiple of 8")


def _t5_bucket_np(rel):
    n = np.maximum(rel, 0)
    max_exact = N_BUCKETS // 2
    nf = np.maximum(n, max_exact).astype(np.float32)
    large = max_exact + (np.log(nf / np.float32(max_exact)) / np.float32(math.log(MAX_DISTANCE / max_exact))
                         * np.float32(N_BUCKETS - max_exact)).astype(np.int32)
    large = np.minimum(large, N_BUCKETS - 1)
    return np.where(n < max_exact, n, large).astype(np.int32)


def _rms(x, g):
    return x * lax.rsqrt(jnp.mean(x * x, axis=-1, keepdims=True) + EPS) * g


def _rms_modulated(x, g, scale, shift):
    return x * lax.rsqrt(jnp.mean(x * x, axis=-1, keepdims=True) + EPS) * (g * (1.0 + scale)) + shift


def _ada_kernel(c_ref, w_ref, b_ref, o_ref):
    c = c_ref[...]
    s = c / (1.0 + jnp.exp(-c))
    o_ref[...] = _dot(s.astype(BF16), w_ref[0].astype(BF16)) + b_ref[...]


def _ada(c_all, w_ada, b_ada):
    n, d = c_all.shape
    width = w_ada.shape[-1]
    tn = 1024
    return pl.pallas_call(
        _ada_kernel,
        grid=(width // tn,),
        in_specs=[pl.BlockSpec((n, d), lambda j: (0, 0)),
                  pl.BlockSpec((1, d, tn), lambda j: (0, 0, j)),
                  pl.BlockSpec((1, tn), lambda j: (0, j))],
        out_specs=pl.BlockSpec((n, tn), lambda j: (0, j)),
        out_shape=jax.ShapeDtypeStruct((n, width), F32),
        compiler_params=pltpu.CompilerParams(dimension_semantics=("arbitrary",), vmem_limit_bytes=VMEM_LIMIT),
        name="ada_mod",
    )(c_all, w_ada, b_ada)


def _inproj_kernel(x_ref, mods_ref, g_ref, w_ref, *refs, fused_pool):
    if fused_pool:
        wp_ref, ps_ref, q_ref, k_ref, v_ref, pool_ref, tail_ref, hist_ref = refs
        i = pl.program_id(1)

        @pl.when(i == 0)
        def _():
            hist_ref[...] = jnp.zeros(hist_ref.shape, F32)
    else:
        q_ref, k_ref, v_ref, u_ref = refs
    d = x_ref.shape[-1]
    ts = x_ref.shape[1]
    x = x_ref[0]
    shift = mods_ref[0, :, 0:d]
    scale = mods_ref[0, :, d:2 * d]
    hb = _rms_modulated(x, g_ref[...], scale, shift).astype(BF16)
    a = ATTN_WIDTH
    u = _dot(hb, w_ref[:, 3 * a:])
    r = _dot(hb, w_ref[:, 0:3 * a])
    q_ref[0] = (r[:, 0:a] * Q_SCALE).astype(BF16)
    k_ref[0] = r[:, a:2 * a]
    v_ref[0] = r[:, 2 * a:3 * a]
    if fused_pool:
        pool_ref[0] = _pool_rows(u, hist_ref[...], wp_ref, ps_ref, i * ts)
        hist_ref[...] = u[ts - POOL_HALO:, :]
        tail_ref[0] = u[ts - POOL_HALO:, :]
    else:
        u_ref[0] = u


def _inproj(x, mods, g, w_in_b, pool_params=None):
    nb, s, d = x.shape
    r = mods.shape[1]
    ts = _row_tile(s, (INPROJ_TILE, 512, 256, 128, 64, 32, 16, 8))
    width = w_in_b.shape[1]
    row_spec = lambda w: pl.BlockSpec((1, ts, w), lambda b, i: (b, i, 0))
    mods_spec = (pl.BlockSpec((1, 1, N_MOD * d), lambda b, i: (b, 0, 0)) if r == 1
                 else pl.BlockSpec((1, ts, N_MOD * d), lambda b, i: (b, i, 0)))
    in_specs = [row_spec(d), mods_spec, pl.BlockSpec((1, d), lambda b, i: (0, 0)),
                pl.BlockSpec((d, width), lambda b, i: (0, 0))]
    out_specs = [row_spec(ATTN_WIDTH)] * 3
    out_shape = [jax.ShapeDtypeStruct((nb, s, ATTN_WIDTH), BF16), jax.ShapeDtypeStruct((nb, s, ATTN_WIDTH), F32),
                 jax.ShapeDtypeStruct((nb, s, ATTN_WIDTH), F32)]
    args, scratch = [x, mods, g, w_in_b], []
    if pool_params is None:
        out_specs.append(row_spec(POOL_WIDTH))
        out_shape.append(jax.ShapeDtypeStruct((nb, s, POOL_WIDTH), F32))
    else:
        assert ts % POOL_HALO == 0
        w_pool_b, pool_scale = pool_params
        in_specs += [pl.BlockSpec(w_pool_b.shape, lambda b, i: (0, 0, 0)),
                     pl.BlockSpec((1, POOL_WIDTH), lambda b, i: (0, 0))]
        args += [w_pool_b, pool_scale]
        out_specs += [row_spec(POOL_WIDTH), pl.BlockSpec((1, POOL_HALO, POOL_WIDTH), lambda b, i: (b, 0, 0))]
        out_shape += [jax.ShapeDtypeStruct((nb, s, POOL_WIDTH), BF16),
                      jax.ShapeDtypeStruct((nb, POOL_HALO, POOL_WIDTH), F32)]
        scratch = [pltpu.VMEM((POOL_HALO, POOL_WIDTH), F32)]
    return pl.pallas_call(
        functools.partial(_inproj_kernel, fused_pool=pool_params is not None),
        grid=(nb, s // ts),
        in_specs=in_specs,
        out_specs=out_specs,
        out_shape=out_shape,
        scratch_shapes=scratch,
        compiler_params=pltpu.CompilerParams(dimension_semantics=("arbitrary", "arbitrary"),
                                             vmem_limit_bytes=VMEM_LIMIT),
        name="in_proj",
    )(*args)


def _bias_kernel(tab_ref, *refs):
    n = len(refs) // 2
    h = pl.program_id(0)
    for idx_ref, out_ref in zip(refs[:n], refs[n:]):
        idx = idx_ref[...]
        bias = jnp.zeros(idx.shape, F32)
        for b in range(N_BUCKETS):
            bias = jnp.where(idx == b, tab_ref[h, b], bias)
        out_ref[0] = jnp.where(idx < 0, NEG, bias * LOG2E)


def _bias_tiles(tab_t):
    blk = MOBA_BLOCK
    r = np.arange(blk)[None, :]
    c = np.arange(blk)[:, None]
    rows = np.arange(SAMPLE_ROWS)[:, None]
    keys = np.arange(blk)[None, :]
    own_keys = np.arange(LANES)[None, :]
    idx = [np.where(r >= c, _t5_bucket_np(r - c), -1), _t5_bucket_np(blk + r - c),
           _t5_bucket_np(blk + rows - keys), np.where(rows >= own_keys, _t5_bucket_np(rows - own_keys), -1)]
    idx = [jnp.asarray(t.astype(np.int32)) for t in idx]
    return pl.pallas_call(
        _bias_kernel,
        grid=(N_HEADS,),
        in_specs=[pl.BlockSpec(memory_space=pltpu.SMEM)] + [pl.BlockSpec(t.shape, lambda h: (0, 0)) for t in idx],
        out_specs=[pl.BlockSpec((1,) + t.shape, lambda h: (h, 0, 0)) for t in idx],
        out_shape=[jax.ShapeDtypeStruct((N_HEADS,) + t.shape, F32) for t in idx],
        compiler_params=pltpu.CompilerParams(dimension_semantics=("arbitrary",)),
        name="rel_bias_tiles",
    )(tab_t, *idx)


def _attn_prompt_kernel(tab_ref, q_ref, k_ref, v_ref, bown_ref, badj_ref, o_ref,
                        kb_ref, vt_ref, km_ref, qm_ref, madd_ref, m_ref, acc_ref, s_ref, *, nb):
    i = pl.program_id(1)
    blk = MOBA_BLOCK
    nbp = km_ref.shape[0]

    @pl.when(i == 0)
    def _():
        kb_ref[...] = k_ref[0].astype(BF16)
        ones = jnp.ones((VT_ROWS - HEAD_DIM, blk), BF16)
        for j in range(nb):
            vt = v_ref[0, j * blk:(j + 1) * blk, :].T.astype(BF16)
            for h in range(N_HEADS):
                vt_ref[j, h, 0:HEAD_DIM, :] = vt[h * HEAD_DIM:(h + 1) * HEAD_DIM, :]
                vt_ref[j, h, HEAD_DIM:, :] = ones
        means = [jnp.mean(k_ref[0, j * blk:(j + 1) * blk, :], axis=0, keepdims=True) for j in range(nb)]
        means += [jnp.zeros_like(means[0])] * (nbp - nb)
        km_ref[...] = jnp.concatenate(means, axis=0)

    lane = lax.broadcasted_iota(jnp.int32, (blk, LANES), 1)
    blk_row = lax.broadcasted_iota(jnp.int32, (nbp, blk), 0)
    valid = blk_row < i
    jm1 = jnp.maximum(i - 1, 0)
    own0 = pl.multiple_of(i * blk, blk)
    adj0 = pl.multiple_of(jm1 * blk, blk)
    heads_per_vreg = LANES // HEAD_DIM

    def lanes_of(h):
        p = h // heads_per_vreg
        return slice(p * LANES, (p + 1) * LANES)

    for h in range(N_HEADS):
        hh = h % heads_per_vreg
        q2 = q_ref[0, :, lanes_of(h)]
        hmask = (lane >= hh * HEAD_DIM) & (lane < (hh + 1) * HEAD_DIM)
        qm = jnp.where(hmask, q2, jnp.zeros_like(q2))
        qm_ref[h] = qm
        gate = _nt_dot(km_ref[:, lanes_of(h)].astype(BF16), qm)
        for j in range(nb):
            gj = gate[j:j + 1, :]
            beats = ((gate > gj) | ((gate == gj) & (blk_row < j))) & valid
            cnt = jnp.sum(beats.astype(F32), axis=0, keepdims=True)
            cnt = cnt + jnp.where(j < i, 0.0, float(nb))
            madd_ref[h, j:j + 1, :] = jnp.where(cnt < MOBA_TOPK, 0.0, NEG)

    def key_blocks(blocks):
        for n, (key0, _, tile_bias, _, _) in enumerate(blocks):
            for h in range(N_HEADS):
                s = _nt_dot(kb_ref[pl.ds(key0, blk), lanes_of(h)], qm_ref[h])
                s_ref[n, h] = s if tile_bias is None else s + tile_bias(h)
        for n, (_, blk_idx, _, query_bias, first) in enumerate(blocks):
            for h in range(N_HEADS):
                s = s_ref[n, h]
                m_blk = jnp.max(s, axis=0, keepdims=True)
                if query_bias is not None:
                    qb = query_bias(h)
                    m_blk = m_blk + qb
                if first:
                    m_new = m_blk
                else:
                    m_old = m_ref[h]
                    m_new = jnp.maximum(m_old, m_blk)
                ref_row = m_new if query_bias is None else m_new - qb
                p = jnp.exp2(s - ref_row).astype(BF16)
                pv = _dot(vt_ref[blk_idx, h], p)
                acc_ref[h] = pv if first else jnp.exp2(m_old - m_new) * acc_ref[h] + pv
                m_ref[h] = m_new

    def far_block(j):
        return (pl.multiple_of(j * blk, blk), j, None,
                lambda h: madd_ref[h, pl.ds(j, 1), :] + tab_ref[h, N_BUCKETS - 1] * LOG2E, False)

    key_blocks([(own0, i, lambda h: bown_ref[h], None, True),
                (adj0, jm1, lambda h: badj_ref[h], lambda h: madd_ref[h, pl.ds(jm1, 1), :], False)])

    def far_pair(jj, carry):
        key_blocks([far_block(FAR_GROUP * jj + n) for n in range(FAR_GROUP)])
        return carry

    lax.fori_loop(0, jm1 // FAR_GROUP, far_pair, 0)
    for n in range(1, FAR_GROUP):
        @pl.when(jm1 % FAR_GROUP >= n)
        def _():
            key_blocks([far_block(jm1 - n)])

    for p in range(ATTN_WIDTH // LANES):
        halves = []
        for h in range(p * heads_per_vreg, (p + 1) * heads_per_vreg):
            acc = acc_ref[h]
            halves.append(acc[0:HEAD_DIM, :] / acc[HEAD_DIM:HEAD_DIM + 1, :])
        o_ref[0, :, p * LANES:(p + 1) * LANES] = jnp.concatenate(halves, axis=0).T.astype(BF16)


def _attn_prompt(tab_t, q, k, v, bias_own_t, bias_adj_t):
    b, s, a = q.shape
    blk = MOBA_BLOCK
    nb = s // blk
    nbp = 8
    assert s % blk == 0 and nb <= nbp
    full = pl.BlockSpec((1, s, a), lambda bi, i: (bi, 0, 0))
    tile = pl.BlockSpec((1, blk, a), lambda bi, i: (bi, i, 0))
    bias = pl.BlockSpec((N_HEADS, blk, blk), lambda bi, i: (0, 0, 0))
    return pl.pallas_call(
        functools.partial(_attn_prompt_kernel, nb=nb),
        grid=(b, nb),
        in_specs=[pl.BlockSpec(memory_space=pltpu.SMEM), tile, full, full, bias, bias],
        out_specs=tile,
        out_shape=jax.ShapeDtypeStruct((b, s, a), BF16),
        scratch_shapes=[pltpu.VMEM((s, a), BF16), pltpu.VMEM((nb, N_HEADS, VT_ROWS, blk), BF16),
                        pltpu.VMEM((nbp, a), F32), pltpu.VMEM((N_HEADS, blk, LANES), BF16),
                        pltpu.VMEM((N_HEADS, nbp, blk), F32), pltpu.VMEM((N_HEADS, 1, blk), F32),
                        pltpu.VMEM((N_HEADS, VT_ROWS, blk), F32),
                        pltpu.VMEM((FAR_GROUP, N_HEADS, blk, blk), F32)],
        compiler_params=pltpu.CompilerParams(dimension_semantics=("arbitrary", "arbitrary"),
                                             vmem_limit_bytes=VMEM_LIMIT),
        name="moba_prompt",
    )(tab_t, q, k, v, bias_own_t, bias_adj_t)


def _pool_rows(u, hist, w_ref, scale_ref, first_row_pos):
    tp = u.shape[0]
    ext = jnp.concatenate([hist, u], axis=0)
    top_pos = first_row_pos + lax.broadcasted_iota(jnp.int32, (POOL_HALO, 1), 0)
    outs = []
    for g, w in enumerate(POOL_WINDOWS):
        gs = slice(g * POOL_GROUP, (g + 1) * POOL_GROUP)
        s = ext[:, gs]
        shift = 1
        while shift < w:
            s = s + pltpu.roll(s, shift, 0)
            shift *= 2
        means = [s[POOL_HALO:2 * POOL_HALO, :] / jnp.minimum(top_pos + 1, w).astype(F32)]
        if tp > POOL_HALO:
            means.append(s[2 * POOL_HALO:, :] * (1.0 / w))
        d = jnp.concatenate(means, axis=0) - u[:, gs]
        outs.append((_dot(d.astype(BF16), w_ref[g]) * scale_ref[:, gs]).astype(BF16))
    return jnp.concatenate(outs, axis=1)


def _pool_kernel(u_ref, hist_ref, w_ref, scale_ref, o_ref, *, first_pos):
    i = pl.program_id(1)
    hist = jnp.where(i > 0, hist_ref[0], 0.0)
    o_ref[0] = _pool_rows(u_ref[0], hist, w_ref, scale_ref, first_pos + i * u_ref.shape[1])


def _halo_spec(tile_rows, width):
    per_tile = tile_rows // POOL_HALO
    return pl.BlockSpec((1, POOL_HALO, width), lambda b, i, *_: (b, jnp.maximum(i * per_tile - 1, 0), 0))


def _pool(u, w_pool_b, pool_scale, first_pos):
    nb, s, c = u.shape
    tp = _row_tile(s, (512, 256, 128, 64, 32, 16))
    return pl.pallas_call(
        functools.partial(_pool_kernel, first_pos=first_pos),
        grid=(nb, s // tp),
        in_specs=[pl.BlockSpec((1, tp, c), lambda b, i: (b, i, 0)),
                  _halo_spec(tp, c),
                  pl.BlockSpec(w_pool_b.shape, lambda b, i: (0, 0, 0)),
                  pl.BlockSpec((1, c), lambda b, i: (0, 0))],
        out_specs=pl.BlockSpec((1, tp, c), lambda b, i: (b, i, 0)),
        out_shape=jax.ShapeDtypeStruct((nb, s, c), BF16),
        compiler_params=pltpu.CompilerParams(dimension_semantics=("arbitrary", "arbitrary"),
                                             vmem_limit_bytes=VMEM_LIMIT),
        name="multi_pool",
    )(u, u, w_pool_b, pool_scale)


def _out_kernel(pt_ref, x_ref, attn_ref, pool_ref, mods_ref, gm_ref, gf_ref, wo_ref, wu_ref, wd_ref, *refs,
                rider):
    n_chunks = wu_ref.shape[1] // FF_CHUNK
    if rider is None:
        (y_ref,) = refs
        n_pages = 0
    else:
        ck_ref, y_ref, kmean_ref, pbuf, psem = refs
        n_pages, groups_per_seq, n_groups = rider
        step = pl.program_id(0) * pl.num_programs(1) + pl.program_id(1)
        last = pl.num_programs(0) * pl.num_programs(1) - 1
        slot = step % 2
        nxt = jnp.minimum(step + 1, last)

        def page_copies(st, sl, first, count):
            g = jnp.minimum(st, n_groups - 1)
            seq, part = g // groups_per_seq, g % groups_per_seq
            return [pltpu.make_async_copy(ck_ref.at[0, pt_ref[seq, part * n_pages + n]], pbuf.at[sl, n],
                                          psem.at[sl]) for n in range(first, first + count)]

        @pl.when(step == 0)
        def _():
            for c in page_copies(step, slot, 0, n_pages):
                c.start()

        for c in page_copies(nxt, 1 - slot, 0, n_pages):
            c.start()
        for c in page_copies(step, slot, 0, n_pages):
            c.wait()
    n_blocks = n_pages // PAGES_PER_BLOCK
    d = x_ref.shape[-1]
    a = ATTN_WIDTH
    x = x_ref[0]
    g1 = mods_ref[0, :, 2 * d:3 * d]
    sh2 = mods_ref[0, :, 3 * d:4 * d]
    sc2 = mods_ref[0, :, 4 * d:5 * d]
    g2 = mods_ref[0, :, 5 * d:6 * d]
    mix = _dot(attn_ref[0], wo_ref[0:a, :]) + _dot(pool_ref[0], wo_ref[a:, :])
    x1 = x + g1 * mix
    hb = _rms_modulated(x1, gm_ref[...], sc2, sh2).astype(BF16)
    acc = jnp.zeros(x.shape, F32)
    for c in range(n_chunks):
        cs = slice(c * FF_CHUNK, (c + 1) * FF_CHUNK)
        t = jnp.maximum(_dot(hb, wu_ref[:, cs]), 0.0)
        acc = acc + _dot((t * t).astype(BF16), wd_ref[cs, :])
        if rider is not None:
            for blk in range(c * n_blocks // n_chunks, (c + 1) * n_blocks // n_chunks):
                ksum = pbuf[slot, blk * PAGES_PER_BLOCK]
                for pg in range(1, PAGES_PER_BLOCK):
                    ksum = ksum + pbuf[slot, blk * PAGES_PER_BLOCK + pg]
                kmean_ref[0, blk] = jnp.sum(ksum, axis=-1) * (1.0 / MOBA_BLOCK)
    x2 = x1 + g2 * acc
    y_ref[0] = _rms(x2, gf_ref[...])
    if rider is not None:
        @pl.when(step == last)
        def _():
            for cp in page_copies(nxt, 1 - slot, 0, n_pages):
                cp.wait()


def _out(x, attn, pool, mods, g_mlp, g_final, w_out_b, w_up_b, w_down_b, page_table, cache_kt=None, tm=None):
    nb, s, d = x.shape
    r = mods.shape[1]
    tm = tm or _row_tile(s)
    n_tiles = s // tm
    row_spec = lambda w: pl.BlockSpec((1, tm, w), lambda b, i, pt: (b, i, 0))
    mods_spec = (pl.BlockSpec((1, 1, N_MOD * d), lambda b, i, pt: (b, 0, 0)) if r == 1
                 else pl.BlockSpec((1, tm, N_MOD * d), lambda b, i, pt: (b, i, 0)))
    resident = lambda w: pl.BlockSpec(w.shape, lambda b, i, pt: (0, 0), pipeline_mode=pl.Buffered(1))
    vec = pl.BlockSpec((1, d), lambda b, i, pt: (0, 0))
    out_specs, out_shape = [row_spec(d)], [jax.ShapeDtypeStruct((nb, s, d), F32)]
    cache_specs, cache_args, scratch, rider = [], [], [], None
    if cache_kt is not None:
        db, n_pages = page_table.shape
        nblk = n_pages // PAGES_PER_BLOCK
        need = -(-db * nblk // (nb * n_tiles))
        group = min(g for g in range(1, nblk + 1) if nblk % g == 0 and g >= need)
        groups_per_seq = nblk // group
        n_groups = db * groups_per_seq
        pps = group * PAGES_PER_BLOCK
        rider = (pps, groups_per_seq, n_groups)

        def group_of(b, i):
            g = jnp.minimum(b * n_tiles + i, n_groups - 1)
            return g // groups_per_seq, g % groups_per_seq

        cache_specs, cache_args = [pl.BlockSpec(memory_space=pl.ANY)], [cache_kt]
        scratch = [pltpu.VMEM((2, pps) + cache_kt.shape[2:], F32), pltpu.SemaphoreType.DMA((2,))]
        out_specs.append(pl.BlockSpec((1, group, N_HEADS, HEAD_DIM), lambda b, i, pt: group_of(b, i) + (0, 0)))
        out_shape.append(jax.ShapeDtypeStruct((db, nblk, N_HEADS, HEAD_DIM), F32))
    grid_spec = pltpu.PrefetchScalarGridSpec(
        num_scalar_prefetch=1,
        grid=(nb, n_tiles),
        in_specs=[row_spec(d), row_spec(ATTN_WIDTH), row_spec(POOL_WIDTH), mods_spec, vec, vec,
                  resident(w_out_b), resident(w_up_b), resident(w_down_b)] + cache_specs,
        out_specs=out_specs,
        scratch_shapes=scratch,
    )
    out = pl.pallas_call(
        functools.partial(_out_kernel, rider=rider),
        grid_spec=grid_spec,
        out_shape=out_shape,
        compiler_params=pltpu.CompilerParams(dimension_semantics=("arbitrary", "arbitrary"),
                                             vmem_limit_bytes=VMEM_LIMIT),
        name="out_mlp",
    )(page_table, x, attn, pool, mods, g_mlp, g_final, w_out_b, w_up_b, w_down_b, *cache_args)
    return out if cache_kt is not None else out[0]


def _select_kernel(q_ref, km_ref, sel_ref):
    rows = q_ref.shape[1]
    nblk = km_ref.shape[1]
    for sq in range(q_ref.shape[0]):
        for h in range(N_HEADS):
            q_h = q_ref[sq, :, h * HEAD_DIM:(h + 1) * HEAD_DIM].astype(F32)
            g = lax.dot_general(q_h, km_ref[sq, :, h, :], (((1,), (1,)), ((), ())),
                                precision=lax.Precision.HIGHEST, preferred_element_type=F32)
            lane = lax.broadcasted_iota(jnp.int32, g.shape, 1)
            out_lane = lax.broadcasted_iota(jnp.int32, (rows, LANES), 1)
            out = jnp.zeros((rows, LANES), jnp.int32)
            for n in range(MOBA_TOPK):
                mx = jnp.max(g, axis=-1, keepdims=True)
                idx = jnp.min(jnp.where(g == mx, lane, nblk), axis=-1, keepdims=True)
                out = jnp.where(out_lane == n, idx, out)
                g = jnp.where(lane == idx, -jnp.inf, g)
            sel_ref[sq, h * rows:(h + 1) * rows, :] = out


def _select(q_rows, kmean):
    db, rows, a = q_rows.shape
    nblk = kmean.shape[1]
    seqs = math.gcd(db, SELECT_SEQS_PER_STEP)
    return pl.pallas_call(
        _select_kernel,
        grid=(db // seqs,),
        in_specs=[pl.BlockSpec((seqs, rows, a), lambda b: (b, 0, 0)),
                  pl.BlockSpec((seqs,) + kmean.shape[1:], lambda b: (b, 0, 0, 0))],
        out_specs=pl.BlockSpec((seqs, N_HEADS * rows, LANES), lambda b: (b, 0, 0)),
        out_shape=jax.ShapeDtypeStruct((db, N_HEADS * rows, LANES), jnp.int32),
        compiler_params=pltpu.CompilerParams(dimension_semantics=("arbitrary",)),
        name="moba_select",
    )(q_rows, kmean)


def _attn_sample_kernel(pt_ref, sel_ref, tab_ref, q_ref, kn_ref, vn_ref, badj_ref, bown_ref, ck_ref, cv_ref,
                        o_ref, kbuf, vbuf, sem, *, nblk, t_new):
    b = pl.program_id(0)
    n_seq = pl.num_programs(0)
    n_sel = MOBA_TOPK
    per_head = t_new * n_sel
    buf = b % 2

    def block_of(sb, h, t, n):
        return sel_ref[((sb * N_HEADS + h) * t_new + t) * n_sel + n]

    def copies(sb, h, bf):
        out = []
        for t in range(t_new):
            for n in range(n_sel):
                j = block_of(sb, h, t, n)
                slot = (bf * N_HEADS + h) * per_head + t * n_sel + n
                for pg in range(PAGES_PER_BLOCK):
                    page = pt_ref[sb, j * PAGES_PER_BLOCK + pg]
                    keys = pl.ds(pg * PAGE_SIZE, PAGE_SIZE)
                    out.append(pltpu.make_async_copy(ck_ref.at[0, page, h], kbuf.at[slot, :, keys], sem.at[0, bf]))
                    out.append(pltpu.make_async_copy(cv_ref.at[0, page, h], vbuf.at[slot, :, keys], sem.at[1, bf]))
        return out

    @pl.when(b == 0)
    def _():
        for h in range(N_HEADS):
            for c in copies(b, h, buf):
                c.start()

    def wait_half(bf):
        half = pl.ds(bf * N_HEADS * per_head, N_HEADS * per_head)
        pltpu.make_async_copy(kbuf.at[half], kbuf.at[half], sem.at[0, bf]).wait()
        pltpu.make_async_copy(vbuf.at[half], vbuf.at[half], sem.at[1, bf]).wait()

    wait_half(buf)

    tok_row = lax.broadcasted_iota(jnp.int32, (SAMPLE_ROWS, MOBA_BLOCK), 0)
    ones_rows = jnp.ones((BF16_SUBLANES, MOBA_BLOCK), BF16)

    def head_attention(h):
        slot0 = (buf * N_HEADS + h) * per_head
        hl = slice(h * HEAD_DIM, (h + 1) * HEAD_DIM)
        c_far = tab_ref[h, N_BUCKETS - 1] * LOG2E
        q_h = q_ref[0, :, hl]
        scores = []
        for t in range(t_new):
            for n in range(n_sel):
                j = block_of(b, h, t, n)
                s = _dot(q_h, kbuf[slot0 + t * n_sel + n].astype(BF16))
                bias = jnp.where(j == nblk - 1, badj_ref[h], c_far)
                scores.append(jnp.where(tok_row == t, s + bias, NEG))
        s_own = _nt_dot(q_h, kn_ref[0, :, hl].astype(BF16)) + bown_ref[h, :, 0:SAMPLE_ROWS]
        m_tile = scores[0]
        for s in scores[1:]:
            m_tile = jnp.maximum(m_tile, s)
        m = jnp.maximum(jnp.max(m_tile, axis=-1, keepdims=True), jnp.max(s_own, axis=-1, keepdims=True))
        p_own = jnp.exp2(s_own - m).astype(BF16)
        v_own = jnp.concatenate([vn_ref[0, h].astype(BF16), ones_rows[:, :SAMPLE_ROWS]], axis=0)
        acc = _nt_dot(v_own, p_own)
        for i_blk, s in enumerate(scores):
            p = jnp.exp2(s - m).astype(BF16)
            v_blk = jnp.concatenate([vbuf[slot0 + i_blk].astype(BF16), ones_rows], axis=0)
            acc = acc + _nt_dot(v_blk, p)
        o_ref[0, h] = acc[0:HEAD_DIM, :] / acc[HEAD_DIM:HEAD_DIM + 1, :]

    nxt = jnp.minimum(b + 1, n_seq - 1)
    for h in range(N_HEADS):
        for c in copies(nxt, h, 1 - buf):
            c.start()
        head_attention(h)

    @pl.when(b == n_seq - 1)
    def _():
        wait_half(1 - buf)


def _attn_sample(page_table, sel_flat, tab_t, q_rows, k_rows, v_cols, badj, bown, cache_kt, cache_vt, nblk, t_new):
    db, rows, a = q_rows.shape
    nh, dh = v_cols.shape[1:3]
    n_slots = 2 * nh * t_new * MOBA_TOPK
    grid_spec = pltpu.PrefetchScalarGridSpec(
        num_scalar_prefetch=2,
        grid=(db,),
        in_specs=[pl.BlockSpec(memory_space=pltpu.SMEM),
                  pl.BlockSpec((1, rows, a), lambda b, pt, sel: (b, 0, 0)),
                  pl.BlockSpec((1, rows, a), lambda b, pt, sel: (b, 0, 0)),
                  pl.BlockSpec((1, nh, dh, rows), lambda b, pt, sel: (b, 0, 0, 0)),
                  pl.BlockSpec(badj.shape, lambda b, pt, sel: (0, 0, 0)),
                  pl.BlockSpec(bown.shape, lambda b, pt, sel: (0, 0, 0)),
                  pl.BlockSpec(memory_space=pl.ANY), pl.BlockSpec(memory_space=pl.ANY)],
        out_specs=pl.BlockSpec((1, nh, dh, rows), lambda b, pt, sel: (b, 0, 0, 0)),
        scratch_shapes=[pltpu.VMEM((n_slots, dh, MOBA_BLOCK), F32),
                        pltpu.VMEM((n_slots, dh, MOBA_BLOCK), F32),
                        pltpu.SemaphoreType.DMA((2, 2))],
    )
    return pl.pallas_call(
        functools.partial(_attn_sample_kernel, nblk=nblk, t_new=t_new),
        grid_spec=grid_spec,
        out_shape=jax.ShapeDtypeStruct((db, nh, dh, rows), F32),
        compiler_params=pltpu.CompilerParams(dimension_semantics=("arbitrary",),
                                             vmem_limit_bytes=VMEM_LIMIT),
        name="moba_sample",
    )(page_table, sel_flat, tab_t, q_rows, k_rows, v_cols, badj, bown, cache_kt, cache_vt)


def kernel(x_prompt, x_sample, cache_k, cache_v, state_pool, page_table, c_prompt, c_sample, w_ada, b_ada, norm_mix, w_in, rel_bias, w_pool, pool_scale, w_out, norm_mlp, w_up, w_down, norm_final):
    assert w_ada.shape[0] == 1, "single-layer decoder"
    b, s, d = x_prompt.shape
    db, t_new, _ = x_sample.shape
    n_pages = page_table.shape[1]
    past = n_pages * PAGE_SIZE
    assert past % MOBA_BLOCK == 0 and past // MOBA_BLOCK >= MOBA_TOPK
    nblk = past // MOBA_BLOCK

    w_in_b = w_in[0].astype(BF16)
    w_out_b = w_out[0].astype(BF16)
    w_up_b = w_up[0].astype(BF16)
    w_down_b = w_down[0].astype(BF16)
    w_pool_b = w_pool[0].astype(BF16)
    g_final = norm_final.reshape(1, d)
    tab_t = rel_bias.T

    mods = _ada(jnp.concatenate([c_prompt, c_sample], axis=0), w_ada, b_ada[0:1])
    mods_p = mods[:b].reshape(b, 1, N_MOD * d)
    mods_s = jnp.repeat(mods[b:], t_new, axis=0).reshape(1, db * t_new, N_MOD * d)
    bias_own, bias_adj, badj_rows, bown_rows = _bias_tiles(tab_t)

    cache_kt = cache_k.transpose(0, 1, 3, 4, 2)
    cache_vt = cache_v.transpose(0, 1, 3, 4, 2)

    q_p, k_p, v_p, pool_p, u_tail = _inproj(x_prompt, mods_p, norm_mix, w_in_b, pool_params=(w_pool_b, pool_scale))
    attn_p = _attn_prompt(tab_t, q_p, k_p, v_p, bias_own, bias_adj)
    y_p, kmean = _out(x_prompt, attn_p, pool_p, mods_p, norm_mlp, g_final, w_out_b, w_up_b, w_down_b,
                      page_table, cache_kt=cache_kt, tm=_row_tile(s, (OUT_TILE_WITH_PAGES, 128, 64, 32, 16, 8)))

    xs = x_sample.reshape(1, db * t_new, d)
    q_s, k_s, v_s, u_s = _inproj(xs, mods_s, norm_mix, w_in_b)
    assert t_new <= SAMPLE_ROWS
    pad_rows = lambda z: jnp.pad(z.reshape(db, t_new, ATTN_WIDTH), ((0, 0), (0, SAMPLE_ROWS - t_new), (0, 0)))
    q_rows = pad_rows(q_s)
    sel = _select(q_rows, kmean).reshape(db, N_HEADS, SAMPLE_ROWS, LANES)[:, :, :t_new, :MOBA_TOPK]
    v_cols = pad_rows(v_s).reshape(db, SAMPLE_ROWS, N_HEADS, HEAD_DIM).transpose(0, 2, 3, 1)
    attn_s = _attn_sample(page_table, sel.reshape(-1), tab_t, q_rows, pad_rows(k_s), v_cols,
                          badj_rows, bown_rows, cache_kt, cache_vt, nblk, t_new)
    attn_s = attn_s[..., :t_new].transpose(0, 3, 1, 2).reshape(1, db * t_new, ATTN_WIDTH).astype(BF16)

    u_full = jnp.concatenate([state_pool[0], u_s.reshape(db, t_new, POOL_WIDTH)], axis=1)
    ext_rows = POOL_HALO + 8
    u_ext = jnp.pad(u_full, ((0, 0), (1, ext_rows - 1 - u_full.shape[1]), (0, 0)))
    pool_ext = _pool(u_ext.reshape(1, db * ext_rows, POOL_WIDTH), w_pool_b, pool_scale, first_pos=past)
    pool_s = pool_ext.reshape(db, ext_rows, POOL_WIDTH)[:, POOL_HALO:POOL_HALO + t_new]
    pool_s = pool_s.reshape(1, db * t_new, POOL_WIDTH)
    y_s = _out(xs, attn_s, pool_s, mods_s, norm_mlp, g_final, w_out_b, w_up_b, w_down_b, page_table)

    heads = lambda z, n, l: z.reshape(1, n, l, N_HEADS, HEAD_DIM)
    return (y_p, y_s.reshape(db, t_new, d),
            heads(k_p, b, s), heads(v_p, b, s), u_tail[:, -POOL_STATE:][None],
            heads(k_s, db, t_new), heads(v_s, db, t_new), u_full[:, -POOL_STATE:][None])
```

```python
import functools
import math

import numpy as np
import jax
import jax.numpy as jnp
from jax import lax
from jax.experimental import pallas as pl
from jax.experimental.pallas import tpu as pltpu

HEAD_DIM = 64
N_HEADS = 8
ATTN_WIDTH = N_HEADS * HEAD_DIM
POOL_WINDOWS = (2, 4, 8, 16)
POOL_GROUP = 128
POOL_WIDTH = POOL_GROUP * len(POOL_WINDOWS)
POOL_STATE = max(POOL_WINDOWS) - 1
POOL_HALO = POOL_STATE + 1
MOBA_BLOCK = 256
MOBA_TOPK = 3
N_BUCKETS = 32
MAX_DISTANCE = 128
PAGE_SIZE = 128
PAGES_PER_BLOCK = MOBA_BLOCK // PAGE_SIZE
N_MOD = 6
EPS = 1e-6
NEG = -1e30
SCALE = HEAD_DIM ** -0.5
LOG2E = math.log2(math.e)
Q_SCALE = SCALE * LOG2E
LANES = 128
BF16_SUBLANES = 16
VT_ROWS = HEAD_DIM + BF16_SUBLANES
FF_CHUNK = 4096
INPROJ_TILE = 1024
SAMPLE_ROWS = 8
SELECT_SEQS_PER_STEP = 8
FAR_GROUP = 2
OUT_TILE_WITH_PAGES = 256
VMEM_LIMIT = 56 * 1024 * 1024

BF16 = jnp.bfloat16
F32 = jnp.float32


def _nt_dot(a, b):
    return lax.dot_general(a, b, (((1,), (1,)), ((), ())), preferred_element_type=F32)


def _dot(a, b):
    return jnp.dot(a, b, preferred_element_type=F32)


def _row_tile(n, candidates=(512, 256, 128, 64, 32, 16, 8)):
    for c in candidates:
        if n % c == 0:
            return c
    raise ValueError(f"row count {n} is not a multiple of 8")


def _t5_bucket_np(rel):
    n = np.maximum(rel, 0)
    max_exact = N_BUCKETS // 2
    nf = np.maximum(n, max_exact).astype(np.float32)
    large = max_exact + (np.log(nf / np.float32(max_exact)) / np.float32(math.log(MAX_DISTANCE / max_exact))
                         * np.float32(N_BUCKETS - max_exact)).astype(np.int32)
    large = np.minimum(large, N_BUCKETS - 1)
    return np.where(n < max_exact, n, large).astype(np.int32)


def _rms(x, g):
    return x * lax.rsqrt(jnp.mean(x * x, axis=-1, keepdims=True) + EPS) * g


def _rms_modulated(x, g, scale, shift):
    return x * lax.rsqrt(jnp.mean(x * x, axis=-1, keepdims=True) + EPS) * (g * (1.0 + scale)) + shift


def _ada_kernel(c_ref, w_ref, b_ref, o_ref):
    c = c_ref[...]
    s = c / (1.0 + jnp.exp(-c))
    o_ref[...] = _dot(s.astype(BF16), w_ref[0].astype(BF16)) + b_ref[...]


def _ada(c_all, w_ada, b_ada):
    n, d = c_all.shape
    width = w_ada.shape[-1]
    tn = 1024
    return pl.pallas_call(
        _ada_kernel,
        grid=(width // tn,),
        in_specs=[pl.BlockSpec((n, d), lambda j: (0, 0)),
                  pl.BlockSpec((1, d, tn), lambda j: (0, 0, j)),
                  pl.BlockSpec((1, tn), lambda j: (0, j))],
        out_specs=pl.BlockSpec((n, tn), lambda j: (0, j)),
        out_shape=jax.ShapeDtypeStruct((n, width), F32),
        compiler_params=pltpu.CompilerParams(dimension_semantics=("arbitrary",), vmem_limit_bytes=VMEM_LIMIT),
        name="ada_mod",
    )(c_all, w_ada, b_ada)


def _inproj_kernel(x_ref, mods_ref, g_ref, w_ref, *refs, fused_pool):
    if fused_pool:
        wp_ref, ps_ref, q_ref, k_ref, v_ref, pool_ref, tail_ref, hist_ref = refs
        i = pl.program_id(1)

        @pl.when(i == 0)
        def _():
            hist_ref[...] = jnp.zeros(hist_ref.shape, F32)
    else:
        q_ref, k_ref, v_ref, u_ref = refs
    d = x_ref.shape[-1]
    ts = x_ref.shape[1]
    x = x_ref[0]
    shift = mods_ref[0, :, 0:d]
    scale = mods_ref[0, :, d:2 * d]
    hb = _rms_modulated(x, g_ref[...], scale, shift).astype(BF16)
    a = ATTN_WIDTH
    u = _dot(hb, w_ref[:, 3 * a:])
    r = _dot(hb, w_ref[:, 0:3 * a])
    q_ref[0] = (r[:, 0:a] * Q_SCALE).astype(BF16)
    k_ref[0] = r[:, a:2 * a]
    v_ref[0] = r[:, 2 * a:3 * a]
    if fused_pool:
        pool_ref[0] = _pool_rows(u, hist_ref[...], wp_ref, ps_ref, i * ts)
        hist_ref[...] = u[ts - POOL_HALO:, :]
        tail_ref[0] = u[ts - POOL_HALO:, :]
    else:
        u_ref[0] = u


def _inproj(x, mods, g, w_in_b, pool_params=None):
    nb, s, d = x.shape
    r = mods.shape[1]
    ts = _row_tile(s, (INPROJ_TILE, 512, 256, 128, 64, 32, 16, 8))
    width = w_in_b.shape[1]
    row_spec = lambda w: pl.BlockSpec((1, ts, w), lambda b, i: (b, i, 0))
    mods_spec = (pl.BlockSpec((1, 1, N_MOD * d), lambda b, i: (b, 0, 0)) if r == 1
                 else pl.BlockSpec((1, ts, N_MOD * d), lambda b, i: (b, i, 0)))
    in_specs = [row_spec(d), mods_spec, pl.BlockSpec((1, d), lambda b, i: (0, 0)),
                pl.BlockSpec((d, width), lambda b, i: (0, 0))]
    out_specs = [row_spec(ATTN_WIDTH)] * 3
    out_shape = [jax.ShapeDtypeStruct((nb, s, ATTN_WIDTH), BF16), jax.ShapeDtypeStruct((nb, s, ATTN_WIDTH), F32),
                 jax.ShapeDtypeStruct((nb, s, ATTN_WIDTH), F32)]
    args, scratch = [x, mods, g, w_in_b], []
    if pool_params is None:
        out_specs.append(row_spec(POOL_WIDTH))
        out_shape.append(jax.ShapeDtypeStruct((nb, s, POOL_WIDTH), F32))
    else:
        assert ts % POOL_HALO == 0
        w_pool_b, pool_scale = pool_params
        in_specs += [pl.BlockSpec(w_pool_b.shape, lambda b, i: (0, 0, 0)),
                     pl.BlockSpec((1, POOL_WIDTH), lambda b, i: (0, 0))]
        args += [w_pool_b, pool_scale]
        out_specs += [row_spec(POOL_WIDTH), pl.BlockSpec((1, POOL_HALO, POOL_WIDTH), lambda b, i: (b, 0, 0))]
        out_shape += [jax.ShapeDtypeStruct((nb, s, POOL_WIDTH), BF16),
                      jax.ShapeDtypeStruct((nb, POOL_HALO, POOL_WIDTH), F32)]
        scratch = [pltpu.VMEM((POOL_HALO, POOL_WIDTH), F32)]
    return pl.pallas_call(
        functools.partial(_inproj_kernel, fused_pool=pool_params is not None),
        grid=(nb, s // ts),
        in_specs=in_specs,
        out_specs=out_specs,
        out_shape=out_shape,
        scratch_shapes=scratch,
        compiler_params=pltpu.CompilerParams(dimension_semantics=("arbitrary", "arbitrary"),
                                             vmem_limit_bytes=VMEM_LIMIT),
        name="in_proj",
    )(*args)


def _bias_kernel(tab_ref, *refs):
    n = len(refs) // 2
    h = pl.program_id(0)
    for idx_ref, out_ref in zip(refs[:n], refs[n:]):
        idx = idx_ref[...]
        bias = jnp.zeros(idx.shape, F32)
        for b in range(N_BUCKETS):
            bias = jnp.where(idx == b, tab_ref[h, b], bias)
        out_ref[0] = jnp.where(idx < 0, NEG, bias * LOG2E)


def _bias_tiles(tab_t):
    blk = MOBA_BLOCK
    r = np.arange(blk)[None, :]
    c = np.arange(blk)[:, None]
    rows = np.arange(SAMPLE_ROWS)[:, None]
    keys = np.arange(blk)[None, :]
    own_keys = np.arange(LANES)[None, :]
    idx = [np.where(r >= c, _t5_bucket_np(r - c), -1), _t5_bucket_np(blk + r - c),
           _t5_bucket_np(blk + rows - keys), np.where(rows >= own_keys, _t5_bucket_np(rows - own_keys), -1)]
    idx = [jnp.asarray(t.astype(np.int32)) for t in idx]
    return pl.pallas_call(
        _bias_kernel,
        grid=(N_HEADS,),
        in_specs=[pl.BlockSpec(memory_space=pltpu.SMEM)] + [pl.BlockSpec(t.shape, lambda h: (0, 0)) for t in idx],
        out_specs=[pl.BlockSpec((1,) + t.shape, lambda h: (h, 0, 0)) for t in idx],
        out_shape=[jax.ShapeDtypeStruct((N_HEADS,) + t.shape, F32) for t in idx],
        compiler_params=pltpu.CompilerParams(dimension_semantics=("arbitrary",)),
        name="rel_bias_tiles",
    )(tab_t, *idx)


def _attn_prompt_kernel(tab_ref, q_ref, k_ref, v_ref, bown_ref, badj_ref, o_ref,
                        kb_ref, vt_ref, km_ref, qm_ref, madd_ref, m_ref, acc_ref, s_ref, *, nb):
    i = pl.program_id(1)
    blk = MOBA_BLOCK
    nbp = km_ref.shape[0]

    @pl.when(i == 0)
    def _():
        kb_ref[...] = k_ref[0].astype(BF16)
        ones = jnp.ones((VT_ROWS - HEAD_DIM, blk), BF16)
        for j in range(nb):
            vt = v_ref[0, j * blk:(j + 1) * blk, :].T.astype(BF16)
            for h in range(N_HEADS):
                vt_ref[j, h, 0:HEAD_DIM, :] = vt[h * HEAD_DIM:(h + 1) * HEAD_DIM, :]
                vt_ref[j, h, HEAD_DIM:, :] = ones
        means = [jnp.mean(k_ref[0, j * blk:(j + 1) * blk, :], axis=0, keepdims=True) for j in range(nb)]
        means += [jnp.zeros_like(means[0])] * (nbp - nb)
        km_ref[...] = jnp.concatenate(means, axis=0)

    lane = lax.broadcasted_iota(jnp.int32, (blk, LANES), 1)
    blk_row = lax.broadcasted_iota(jnp.int32, (nbp, blk), 0)
    valid = blk_row < i
    jm1 = jnp.maximum(i - 1, 0)
    own0 = pl.multiple_of(i * blk, blk)
    adj0 = pl.multiple_of(jm1 * blk, blk)
    heads_per_vreg = LANES // HEAD_DIM

    def lanes_of(h):
        p = h // heads_per_vreg
        return slice(p * LANES, (p + 1) * LANES)

    for h in range(N_HEADS):
        hh = h % heads_per_vreg
        q2 = q_ref[0, :, lanes_of(h)]
        hmask = (lane >= hh * HEAD_DIM) & (lane < (hh + 1) * HEAD_DIM)
        qm = jnp.where(hmask, q2, jnp.zeros_like(q2))
        qm_ref[h] = qm
        gate = _nt_dot(km_ref[:, lanes_of(h)].astype(BF16), qm)
        for j in range(nb):
            gj = gate[j:j + 1, :]
            beats = ((gate > gj) | ((gate == gj) & (blk_row < j))) & valid
            cnt = jnp.sum(beats.astype(F32), axis=0, keepdims=True)
            cnt = cnt + jnp.where(j < i, 0.0, float(nb))
            madd_ref[h, j:j + 1, :] = jnp.where(cnt < MOBA_TOPK, 0.0, NEG)

    def key_blocks(blocks):
        for n, (key0, _, tile_bias, _, _) in enumerate(blocks):
            for h in range(N_HEADS):
                s = _nt_dot(kb_ref[pl.ds(key0, blk), lanes_of(h)], qm_ref[h])
                s_ref[n, h] = s if tile_bias is None else s + tile_bias(h)
        for n, (_, blk_idx, _, query_bias, first) in enumerate(blocks):
            for h in range(N_HEADS):
                s = s_ref[n, h]
                m_blk = jnp.max(s, axis=0, keepdims=True)
                if query_bias is not None:
                    qb = query_bias(h)
                    m_blk = m_blk + qb
                if first:
                    m_new = m_blk
                else:
                    m_old = m_ref[h]
                    m_new = jnp.maximum(m_old, m_blk)
                ref_row = m_new if query_bias is None else m_new - qb
                p = jnp.exp2(s - ref_row).astype(BF16)
                pv = _dot(vt_ref[blk_idx, h], p)
                acc_ref[h] = pv if first else jnp.exp2(m_old - m_new) * acc_ref[h] + pv
                m_ref[h] = m_new

    def far_block(j):
        return (pl.multiple_of(j * blk, blk), j, None,
                lambda h: madd_ref[h, pl.ds(j, 1), :] + tab_ref[h, N_BUCKETS - 1] * LOG2E, False)

    key_blocks([(own0, i, lambda h: bown_ref[h], None, True),
                (adj0, jm1, lambda h: badj_ref[h], lambda h: madd_ref[h, pl.ds(jm1, 1), :], False)])

    def far_pair(jj, carry):
        key_blocks([far_block(FAR_GROUP * jj + n) for n in range(FAR_GROUP)])
        return carry

    lax.fori_loop(0, jm1 // FAR_GROUP, far_pair, 0)
    for n in range(1, FAR_GROUP):
        @pl.when(jm1 % FAR_GROUP >= n)
        def _():
            key_blocks([far_block(jm1 - n)])

    for p in range(ATTN_WIDTH // LANES):
        halves = []
        for h in range(p * heads_per_vreg, (p + 1) * heads_per_vreg):
            acc = acc_ref[h]
            halves.append(acc[0:HEAD_DIM, :] / acc[HEAD_DIM:HEAD_DIM + 1, :])
        o_ref[0, :, p * LANES:(p + 1) * LANES] = jnp.concatenate(halves, axis=0).T.astype(BF16)


def _attn_prompt(tab_t, q, k, v, bias_own_t, bias_adj_t):
    b, s, a = q.shape
    blk = MOBA_BLOCK
    nb = s // blk
    nbp = 8
    assert s % blk == 0 and nb <= nbp
    full = pl.BlockSpec((1, s, a), lambda bi, i: (bi, 0, 0))
    tile = pl.BlockSpec((1, blk, a), lambda bi, i: (bi, i, 0))
    bias = pl.BlockSpec((N_HEADS, blk, blk), lambda bi, i: (0, 0, 0))
    return pl.pallas_call(
        functools.partial(_attn_prompt_kernel, nb=nb),
        grid=(b, nb),
        in_specs=[pl.BlockSpec(memory_space=pltpu.SMEM), tile, full, full, bias, bias],
        out_specs=tile,
        out_shape=jax.ShapeDtypeStruct((b, s, a), BF16),
        scratch_shapes=[pltpu.VMEM((s, a), BF16), pltpu.VMEM((nb, N_HEADS, VT_ROWS, blk), BF16),
                        pltpu.VMEM((nbp, a), F32), pltpu.VMEM((N_HEADS, blk, LANES), BF16),
                        pltpu.VMEM((N_HEADS, nbp, blk), F32), pltpu.VMEM((N_HEADS, 1, blk), F32),
                        pltpu.VMEM((N_HEADS, VT_ROWS, blk), F32),
                        pltpu.VMEM((FAR_GROUP, N_HEADS, blk, blk), F32)],
        compiler_params=pltpu.CompilerParams(dimension_semantics=("arbitrary", "arbitrary"),
                                             vmem_limit_bytes=VMEM_LIMIT),
        name="moba_prompt",
    )(tab_t, q, k, v, bias_own_t, bias_adj_t)


def _pool_rows(u, hist, w_ref, scale_ref, first_row_pos):
    tp = u.shape[0]
    ext = jnp.concatenate([hist, u], axis=0)
    top_pos = first_row_pos + lax.broadcasted_iota(jnp.int32, (POOL_HALO, 1), 0)
    outs = []
    for g, w in enumerate(POOL_WINDOWS):
        gs = slice(g * POOL_GROUP, (g + 1) * POOL_GROUP)
        s = ext[:, gs]
        shift = 1
        while shift < w:
            s = s + pltpu.roll(s, shift, 0)
            shift *= 2
        means = [s[POOL_HALO:2 * POOL_HALO, :] / jnp.minimum(top_pos + 1, w).astype(F32)]
        if tp > POOL_HALO:
            means.append(s[2 * POOL_HALO:, :] * (1.0 / w))
        d = jnp.concatenate(means, axis=0) - u[:, gs]
        outs.append((_dot(d.astype(BF16), w_ref[g]) * scale_ref[:, gs]).astype(BF16))
    return jnp.concatenate(outs, axis=1)


def _pool_kernel(u_ref, hist_ref, w_ref, scale_ref, o_ref, *, first_pos):
    i = pl.program_id(1)
    hist = jnp.where(i > 0, hist_ref[0], 0.0)
    o_ref[0] = _pool_rows(u_ref[0], hist, w_ref, scale_ref, first_pos + i * u_ref.shape[1])


def _halo_spec(tile_rows, width):
    per_tile = tile_rows // POOL_HALO
    return pl.BlockSpec((1, POOL_HALO, width), lambda b, i, *_: (b, jnp.maximum(i * per_tile - 1, 0), 0))


def _pool(u, w_pool_b, pool_scale, first_pos):
    nb, s, c = u.shape
    tp = _row_tile(s, (512, 256, 128, 64, 32, 16))
    return pl.pallas_call(
        functools.partial(_pool_kernel, first_pos=first_pos),
        grid=(nb, s // tp),
        in_specs=[pl.BlockSpec((1, tp, c), lambda b, i: (b, i, 0)),
                  _halo_spec(tp, c),
                  pl.BlockSpec(w_pool_b.shape, lambda b, i: (0, 0, 0)),
                  pl.BlockSpec((1, c), lambda b, i: (0, 0))],
        out_specs=pl.BlockSpec((1, tp, c), lambda b, i: (b, i, 0)),
        out_shape=jax.ShapeDtypeStruct((nb, s, c), BF16),
        compiler_params=pltpu.CompilerParams(dimension_semantics=("arbitrary", "arbitrary"),
                                             vmem_limit_bytes=VMEM_LIMIT),
        name="multi_pool",
    )(u, u, w_pool_b, pool_scale)


def _out_kernel(pt_ref, x_ref, attn_ref, pool_ref, mods_ref, gm_ref, gf_ref, wo_ref, wu_ref, wd_ref, *refs,
                rider):
    n_chunks = wu_ref.shape[1] // FF_CHUNK
    if rider is None:
        (y_ref,) = refs
        n_pages = 0
    else:
        ck_ref, y_ref, kmean_ref, pbuf, psem = refs
        n_pages, groups_per_seq, n_groups = rider
        step = pl.program_id(0) * pl.num_programs(1) + pl.program_id(1)
        last = pl.num_programs(0) * pl.num_programs(1) - 1
        slot = step % 2
        nxt = jnp.minimum(step + 1, last)

        def page_copies(st, sl, first, count):
            g = jnp.minimum(st, n_groups - 1)
            seq, part = g // groups_per_seq, g % groups_per_seq
            return [pltpu.make_async_copy(ck_ref.at[0, pt_ref[seq, part * n_pages + n]], pbuf.at[sl, n],
                                          psem.at[sl]) for n in range(first, first + count)]

        @pl.when(step == 0)
        def _():
            for c in page_copies(step, slot, 0, n_pages):
                c.start()

        for c in page_copies(nxt, 1 - slot, 0, n_pages):
            c.start()
        for c in page_copies(step, slot, 0, n_pages):
            c.wait()
    n_blocks = n_pages // PAGES_PER_BLOCK
    d = x_ref.shape[-1]
    a = ATTN_WIDTH
    x = x_ref[0]
    g1 = mods_ref[0, :, 2 * d:3 * d]
    sh2 = mods_ref[0, :, 3 * d:4 * d]
    sc2 = mods_ref[0, :, 4 * d:5 * d]
    g2 = mods_ref[0, :, 5 * d:6 * d]
    mix = _dot(attn_ref[0], wo_ref[0:a, :]) + _dot(pool_ref[0], wo_ref[a:, :])
    x1 = x + g1 * mix
    hb = _rms_modulated(x1, gm_ref[...], sc2, sh2).astype(BF16)
    acc = jnp.zeros(x.shape, F32)
    for c in range(n_chunks):
        cs = slice(c * FF_CHUNK, (c + 1) * FF_CHUNK)
        t = jnp.maximum(_dot(hb, wu_ref[:, cs]), 0.0)
        acc = acc + _dot((t * t).astype(BF16), wd_ref[cs, :])
        if rider is not None:
            for blk in range(c * n_blocks // n_chunks, (c + 1) * n_blocks // n_chunks):
                ksum = pbuf[slot, blk * PAGES_PER_BLOCK]
                for pg in range(1, PAGES_PER_BLOCK):
                    ksum = ksum + pbuf[slot, blk * PAGES_PER_BLOCK + pg]
                kmean_ref[0, blk] = jnp.sum(ksum, axis=-1) * (1.0 / MOBA_BLOCK)
    x2 = x1 + g2 * acc
    y_ref[0] = _rms(x2, gf_ref[...])
    if rider is not None:
        @pl.when(step == last)
        def _():
            for cp in page_copies(nxt, 1 - slot, 0, n_pages):
                cp.wait()


def _out(x, attn, pool, mods, g_mlp, g_final, w_out_b, w_up_b, w_down_b, page_table, cache_kt=None, tm=None):
    nb, s, d = x.shape
    r = mods.shape[1]
    tm = tm or _row_tile(s)
    n_tiles = s // tm
    row_spec = lambda w: pl.BlockSpec((1, tm, w), lambda b, i, pt: (b, i, 0))
    mods_spec = (pl.BlockSpec((1, 1, N_MOD * d), lambda b, i, pt: (b, 0, 0)) if r == 1
                 else pl.BlockSpec((1, tm, N_MOD * d), lambda b, i, pt: (b, i, 0)))
    resident = lambda w: pl.BlockSpec(w.shape, lambda b, i, pt: (0, 0), pipeline_mode=pl.Buffered(1))
    vec = pl.BlockSpec((1, d), lambda b, i, pt: (0, 0))
    out_specs, out_shape = [row_spec(d)], [jax.ShapeDtypeStruct((nb, s, d), F32)]
    cache_specs, cache_args, scratch, rider = [], [], [], None
    if cache_kt is not None:
        db, n_pages = page_table.shape
        nblk = n_pages // PAGES_PER_BLOCK
        need = -(-db * nblk // (nb * n_tiles))
        group = min(g for g in range(1, nblk + 1) if nblk % g == 0 and g >= need)
        groups_per_seq = nblk // group
        n_groups = db * groups_per_seq
        pps = group * PAGES_PER_BLOCK
        rider = (pps, groups_per_seq, n_groups)

        def group_of(b, i):
            g = jnp.minimum(b * n_tiles + i, n_groups - 1)
            return g // groups_per_seq, g % groups_per_seq

        cache_specs, cache_args = [pl.BlockSpec(memory_space=pl.ANY)], [cache_kt]
        scratch = [pltpu.VMEM((2, pps) + cache_kt.shape[2:], F32), pltpu.SemaphoreType.DMA((2,))]
        out_specs.append(pl.BlockSpec((1, group, N_HEADS, HEAD_DIM), lambda b, i, pt: group_of(b, i) + (0, 0)))
        out_shape.append(jax.ShapeDtypeStruct((db, nblk, N_HEADS, HEAD_DIM), F32))
    grid_spec = pltpu.PrefetchScalarGridSpec(
        num_scalar_prefetch=1,
        grid=(nb, n_tiles),
        in_specs=[row_spec(d), row_spec(ATTN_WIDTH), row_spec(POOL_WIDTH), mods_spec, vec, vec,
                  resident(w_out_b), resident(w_up_b), resident(w_down_b)] + cache_specs,
        out_specs=out_specs,
        scratch_shapes=scratch,
    )
    out = pl.pallas_call(
        functools.partial(_out_kernel, rider=rider),
        grid_spec=grid_spec,
        out_shape=out_shape,
        compiler_params=pltpu.CompilerParams(dimension_semantics=("arbitrary", "arbitrary"),
                                             vmem_limit_bytes=VMEM_LIMIT),
        name="out_mlp",
    )(page_table, x, attn, pool, mods, g_mlp, g_final, w_out_b, w_up_b, w_down_b, *cache_args)
    return out if cache_kt is not None else out[0]


def _select_kernel(q_ref, km_ref, sel_ref):
    rows = q_ref.shape[1]
    nblk = km_ref.shape[1]
    for sq in range(q_ref.shape[0]):
        for h in range(N_HEADS):
            q_h = q_ref[sq, :, h * HEAD_DIM:(h + 1) * HEAD_DIM].astype(F32)
            g = lax.dot_general(q_h, km_ref[sq, :, h, :], (((1,), (1,)), ((), ())),
                                precision=lax.Precision.HIGHEST, preferred_element_type=F32)
            lane = lax.broadcasted_iota(jnp.int32, g.shape, 1)
            out_lane = lax.broadcasted_iota(jnp.int32, (rows, LANES), 1)
            out = jnp.zeros((rows, LANES), jnp.int32)
            for n in range(MOBA_TOPK):
                mx = jnp.max(g, axis=-1, keepdims=True)
                idx = jnp.min(jnp.where(g == mx, lane, nblk), axis=-1, keepdims=True)
                out = jnp.where(out_lane == n, idx, out)
                g = jnp.where(lane == idx, -jnp.inf, g)
            sel_ref[sq, h * rows:(h + 1) * rows, :] = out


def _select(q_rows, kmean):
    db, rows, a = q_rows.shape
    nblk = kmean.shape[1]
    seqs = math.gcd(db, SELECT_SEQS_PER_STEP)
    return pl.pallas_call(
        _select_kernel,
        grid=(db // seqs,),
        in_specs=[pl.BlockSpec((seqs, rows, a), lambda b: (b, 0, 0)),
                  pl.BlockSpec((seqs,) + kmean.shape[1:], lambda b: (b, 0, 0, 0))],
        out_specs=pl.BlockSpec((seqs, N_HEADS * rows, LANES), lambda b: (b, 0, 0)),
        out_shape=jax.ShapeDtypeStruct((db, N_HEADS * rows, LANES), jnp.int32),
        compiler_params=pltpu.CompilerParams(dimension_semantics=("arbitrary",)),
        name="moba_select",
    )(q_rows, kmean)


def _attn_sample_kernel(pt_ref, sel_ref, tab_ref, q_ref, kn_ref, vn_ref, badj_ref, bown_ref, ck_ref, cv_ref,
                        o_ref, kbuf, vbuf, sem, *, nblk, t_new):
    b = pl.program_id(0)
    n_seq = pl.num_programs(0)
    n_sel = MOBA_TOPK
    per_head = t_new * n_sel
    buf = b % 2

    def block_of(sb, h, t, n):
        return sel_ref[((sb * N_HEADS + h) * t_new + t) * n_sel + n]

    def copies(sb, h, bf):
        out = []
        for t in range(t_new):
            for n in range(n_sel):
                j = block_of(sb, h, t, n)
                slot = (bf * N_HEADS + h) * per_head + t * n_sel + n
                for pg in range(PAGES_PER_BLOCK):
                    page = pt_ref[sb, j * PAGES_PER_BLOCK + pg]
                    keys = pl.ds(pg * PAGE_SIZE, PAGE_SIZE)
                    out.append(pltpu.make_async_copy(ck_ref.at[0, page, h], kbuf.at[slot, :, keys], sem.at[0, bf]))
                    out.append(pltpu.make_async_copy(cv_ref.at[0, page, h], vbuf.at[slot, :, keys], sem.at[1, bf]))
        return out

    @pl.when(b == 0)
    def _():
        for h in range(N_HEADS):
            for c in copies(b, h, buf):
                c.start()

    for h in range(N_HEADS):
        for c in copies(b, h, buf):
            c.wait()

    tok_row = lax.broadcasted_iota(jnp.int32, (SAMPLE_ROWS, MOBA_BLOCK), 0)
    ones_rows = jnp.ones((BF16_SUBLANES, MOBA_BLOCK), BF16)

    def head_attention(h):
        slot0 = (buf * N_HEADS + h) * per_head
        hl = slice(h * HEAD_DIM, (h + 1) * HEAD_DIM)
        c_far = tab_ref[h, N_BUCKETS - 1] * LOG2E
        q_h = q_ref[0, :, hl]
        scores = []
        for t in range(t_new):
            for n in range(n_sel):
                j = block_of(b, h, t, n)
                s = _dot(q_h, kbuf[slot0 + t * n_sel + n].astype(BF16))
                bias = jnp.where(j == nblk - 1, badj_ref[h], c_far)
                scores.append(jnp.where(tok_row == t, s + bias, NEG))
        s_own = _nt_dot(q_h, kn_ref[0, :, hl].astype(BF16)) + bown_ref[h, :, 0:SAMPLE_ROWS]
        m_tile = scores[0]
        for s in scores[1:]:
            m_tile = jnp.maximum(m_tile, s)
        m = jnp.maximum(jnp.max(m_tile, axis=-1, keepdims=True), jnp.max(s_own, axis=-1, keepdims=True))
        p_own = jnp.exp2(s_own - m).astype(BF16)
        v_own = jnp.concatenate([vn_ref[0, h].astype(BF16), ones_rows[:, :SAMPLE_ROWS]], axis=0)
        acc = _nt_dot(v_own, p_own)
        for i_blk, s in enumerate(scores):
            p = jnp.exp2(s - m).astype(BF16)
            v_blk = jnp.concatenate([vbuf[slot0 + i_blk].astype(BF16), ones_rows], axis=0)
            acc = acc + _nt_dot(v_blk, p)
        o_ref[0, h] = acc[0:HEAD_DIM, :] / acc[HEAD_DIM:HEAD_DIM + 1, :]

    nxt = jnp.minimum(b + 1, n_seq - 1)
    for h in range(N_HEADS):
        for c in copies(nxt, h, 1 - buf):
            c.start()
        head_attention(h)

    @pl.when(b == n_seq - 1)
    def _():
        for h in range(N_HEADS):
            for c in copies(nxt, h, 1 - buf):
                c.wait()


def _attn_sample(page_table, sel_flat, tab_t, q_rows, k_rows, v_cols, badj, bown, cache_kt, cache_vt, nblk, t_new):
    db, rows, a = q_rows.shape
    nh, dh = v_cols.shape[1:3]
    n_slots = 2 * nh * t_new * MOBA_TOPK
    grid_spec = pltpu.PrefetchScalarGridSpec(
        num_scalar_prefetch=2,
        grid=(db,),
        in_specs=[pl.BlockSpec(memory_space=pltpu.SMEM),
                  pl.BlockSpec((1, rows, a), lambda b, pt, sel: (b, 0, 0)),
                  pl.BlockSpec((1, rows, a), lambda b, pt, sel: (b, 0, 0)),
                  pl.BlockSpec((1, nh, dh, rows), lambda b, pt, sel: (b, 0, 0, 0)),
                  pl.BlockSpec(badj.shape, lambda b, pt, sel: (0, 0, 0)),
                  pl.BlockSpec(bown.shape, lambda b, pt, sel: (0, 0, 0)),
                  pl.BlockSpec(memory_space=pl.ANY), pl.BlockSpec(memory_space=pl.ANY)],
        out_specs=pl.BlockSpec((1, nh, dh, rows), lambda b, pt, sel: (b, 0, 0, 0)),
        scratch_shapes=[pltpu.VMEM((n_slots, dh, MOBA_BLOCK), F32),
                        pltpu.VMEM((n_slots, dh, MOBA_BLOCK), F32),
                        pltpu.SemaphoreType.DMA((2, 2))],
    )
    return pl.pallas_call(
        functools.partial(_attn_sample_kernel, nblk=nblk, t_new=t_new),
        grid_spec=grid_spec,
        out_shape=jax.ShapeDtypeStruct((db, nh, dh, rows), F32),
        compiler_params=pltpu.CompilerParams(dimension_semantics=("arbitrary",),
                                             vmem_limit_bytes=VMEM_LIMIT),
        name="moba_sample",
    )(page_table, sel_flat, tab_t, q_rows, k_rows, v_cols, badj, bown, cache_kt, cache_vt)


def kernel(x_prompt, x_sample, cache_k, cache_v, state_pool, page_table, c_prompt, c_sample, w_ada, b_ada, norm_mix, w_in, rel_bias, w_pool, pool_scale, w_out, norm_mlp, w_up, w_down, norm_final):
    assert w_ada.shape[0] == 1, "single-layer decoder"
    b, s, d = x_prompt.shape
    db, t_new, _ = x_sample.shape
    n_pages = page_table.shape[1]
    past = n_pages * PAGE_SIZE
    assert past % MOBA_BLOCK == 0 and past // MOBA_BLOCK >= MOBA_TOPK
    nblk = past // MOBA_BLOCK

    w_in_b = w_in[0].astype(BF16)
    w_out_b = w_out[0].astype(BF16)
    w_up_b = w_up[0].astype(BF16)
    w_down_b = w_down[0].astype(BF16)
    w_pool_b = w_pool[0].astype(BF16)
    g_final = norm_final.reshape(1, d)
    tab_t = rel_bias.T

    mods = _ada(jnp.concatenate([c_prompt, c_sample], axis=0), w_ada, b_ada[0:1])
    mods_p = mods[:b].reshape(b, 1, N_MOD * d)
    mods_s = jnp.repeat(mods[b:], t_new, axis=0).reshape(1, db * t_new, N_MOD * d)
    bias_own, bias_adj, badj_rows, bown_rows = _bias_tiles(tab_t)

    cache_kt = cache_k.transpose(0, 1, 3, 4, 2)
    cache_vt = cache_v.transpose(0, 1, 3, 4, 2)

    q_p, k_p, v_p, pool_p, u_tail = _inproj(x_prompt, mods_p, norm_mix, w_in_b, pool_params=(w_pool_b, pool_scale))
    attn_p = _attn_prompt(tab_t, q_p, k_p, v_p, bias_own, bias_adj)
    y_p, kmean = _out(x_prompt, attn_p, pool_p, mods_p, norm_mlp, g_final, w_out_b, w_up_b, w_down_b,
                      page_table, cache_kt=cache_kt, tm=_row_tile(s, (OUT_TILE_WITH_PAGES, 128, 64, 32, 16, 8)))

    xs = x_sample.reshape(1, db * t_new, d)
    q_s, k_s, v_s, u_s = _inproj(xs, mods_s, norm_mix, w_in_b)
    assert t_new <= SAMPLE_ROWS
    pad_rows = lambda z: jnp.pad(z.reshape(db, t_new, ATTN_WIDTH), ((0, 0), (0, SAMPLE_ROWS - t_new), (0, 0)))
    q_rows = pad_rows(q_s)
    sel = _select(q_rows, kmean).reshape(db, N_HEADS, SAMPLE_ROWS, LANES)[:, :, :t_new, :MOBA_TOPK]
    v_cols = pad_rows(v_s).reshape(db, SAMPLE_ROWS, N_HEADS, HEAD_DIM).transpose(0, 2, 3, 1)
    attn_s = _attn_sample(page_table, sel.reshape(-1), tab_t, q_rows, pad_rows(k_s), v_cols,
                          badj_rows, bown_rows, cache_kt, cache_vt, nblk, t_new)
    attn_s = attn_s[..., :t_new].transpose(0, 3, 1, 2).reshape(1, db * t_new, ATTN_WIDTH).astype(BF16)

    u_full = jnp.concatenate([state_pool[0], u_s.reshape(db, t_new, POOL_WIDTH)], axis=1)
    ext_rows = POOL_HALO + 8
    u_ext = jnp.pad(u_full, ((0, 0), (1, ext_rows - 1 - u_full.shape[1]), (0, 0)))
    pool_ext = _pool(u_ext.reshape(1, db * ext_rows, POOL_WIDTH), w_pool_b, pool_scale, first_pos=past)
    pool_s = pool_ext.reshape(db, ext_rows, POOL_WIDTH)[:, POOL_HALO:POOL_HALO + t_new]
    pool_s = pool_s.reshape(1, db * t_new, POOL_WIDTH)
    y_s = _out(xs, attn_s, pool_s, mods_s, norm_mlp, g_final, w_out_b, w_up_b, w_down_b, page_table)

    heads = lambda z, n, l: z.reshape(1, n, l, N_HEADS, HEAD_DIM)
    return (y_p, y_s.reshape(db, t_new, d),
            heads(k_p, b, s), heads(v_p, b, s), u_tail[:, -POOL_STATE:][None],
            heads(k_s, db, t_new), heads(v_s, db, t_new), u_full[:, -POOL_STATE:][None])
```

```python
import functools
import math

import numpy as np
import jax
import jax.numpy as jnp
from jax import lax
from jax.experimental import pallas as pl
from jax.experimental.pallas import tpu as pltpu

HEAD_DIM = 64
N_HEADS = 8
ATTN_WIDTH = N_HEADS * HEAD_DIM
POOL_WINDOWS = (2, 4, 8, 16)
POOL_GROUP = 128
POOL_WIDTH = POOL_GROUP * len(POOL_WINDOWS)
POOL_STATE = max(POOL_WINDOWS) - 1
POOL_HALO = POOL_STATE + 1
MOBA_BLOCK = 256
MOBA_TOPK = 3
N_BUCKETS = 32
MAX_DISTANCE = 128
PAGE_SIZE = 128
PAGES_PER_BLOCK = MOBA_BLOCK // PAGE_SIZE
N_MOD = 6
EPS = 1e-6
NEG = -1e30
SCALE = HEAD_DIM ** -0.5
LOG2E = math.log2(math.e)
Q_SCALE = SCALE * LOG2E
LANES = 128
BF16_SUBLANES = 16
VT_ROWS = HEAD_DIM + BF16_SUBLANES
FF_CHUNK = 4096
INPROJ_TILE = 1024
SAMPLE_ROWS = 8
SELECT_SEQS_PER_STEP = 8
FAR_GROUP = 2
OUT_TILE_WITH_PAGES = 256
VMEM_LIMIT = 56 * 1024 * 1024

BF16 = jnp.bfloat16
F32 = jnp.float32


def _nt_dot(a, b):
    return lax.dot_general(a, b, (((1,), (1,)), ((), ())), preferred_element_type=F32)


def _dot(a, b):
    return jnp.dot(a, b, preferred_element_type=F32)


def _row_tile(n, candidates=(512, 256, 128, 64, 32, 16, 8)):
    for c in candidates:
        if n % c == 0:
            return c
    raise ValueError(f"row count {n} is not a multiple of 8")


def _t5_bucket_np(rel):
    n = np.maximum(rel, 0)
    max_exact = N_BUCKETS // 2
    nf = np.maximum(n, max_exact).astype(np.float32)
    large = max_exact + (np.log(nf / np.float32(max_exact)) / np.float32(math.log(MAX_DISTANCE / max_exact))
                         * np.float32(N_BUCKETS - max_exact)).astype(np.int32)
    large = np.minimum(large, N_BUCKETS - 1)
    return np.where(n < max_exact, n, large).astype(np.int32)


def _rms(x, g):
    return x * lax.rsqrt(jnp.mean(x * x, axis=-1, keepdims=True) + EPS) * g


def _rms_modulated(x, g, scale, shift):
    return x * lax.rsqrt(jnp.mean(x * x, axis=-1, keepdims=True) + EPS) * (g * (1.0 + scale)) + shift


def _ada_kernel(c_ref, w_ref, b_ref, o_ref):
    c = c_ref[...]
    s = c / (1.0 + jnp.exp(-c))
    o_ref[...] = _dot(s.astype(BF16), w_ref[0].astype(BF16)) + b_ref[...]


def _ada(c_all, w_ada, b_ada):
    n, d = c_all.shape
    width = w_ada.shape[-1]
    tn = 1024
    return pl.pallas_call(
        _ada_kernel,
        grid=(width // tn,),
        in_specs=[pl.BlockSpec((n, d), lambda j: (0, 0)),
                  pl.BlockSpec((1, d, tn), lambda j: (0, 0, j)),
                  pl.BlockSpec((1, tn), lambda j: (0, j))],
        out_specs=pl.BlockSpec((n, tn), lambda j: (0, j)),
        out_shape=jax.ShapeDtypeStruct((n, width), F32),
        compiler_params=pltpu.CompilerParams(dimension_semantics=("arbitrary",), vmem_limit_bytes=VMEM_LIMIT),
        name="ada_mod",
    )(c_all, w_ada, b_ada)


def _inproj_kernel(x_ref, mods_ref, g_ref, w_ref, *refs, fused_pool):
    if fused_pool:
        wp_ref, ps_ref, q_ref, k_ref, v_ref, pool_ref, tail_ref, hist_ref = refs
        i = pl.program_id(1)

        @pl.when(i == 0)
        def _():
            hist_ref[...] = jnp.zeros(hist_ref.shape, F32)
    else:
        q_ref, k_ref, v_ref, u_ref = refs
    d = x_ref.shape[-1]
    ts = x_ref.shape[1]
    x = x_ref[0]
    shift = mods_ref[0, :, 0:d]
    scale = mods_ref[0, :, d:2 * d]
    hb = _rms_modulated(x, g_ref[...], scale, shift).astype(BF16)
    a = ATTN_WIDTH
    u = _dot(hb, w_ref[:, 3 * a:])
    r = _dot(hb, w_ref[:, 0:3 * a])
    q_ref[0] = (r[:, 0:a] * Q_SCALE).astype(BF16)
    k_ref[0] = r[:, a:2 * a]
    v_ref[0] = r[:, 2 * a:3 * a]
    if fused_pool:
        pool_ref[0] = _pool_rows(u, hist_ref[...], wp_ref, ps_ref, i * ts)
        hist_ref[...] = u[ts - POOL_HALO:, :]
        tail_ref[0] = u[ts - POOL_HALO:, :]
    else:
        u_ref[0] = u


def _inproj(x, mods, g, w_in_b, pool_params=None):
    nb, s, d = x.shape
    r = mods.shape[1]
    ts = _row_tile(s, (INPROJ_TILE, 512, 256, 128, 64, 32, 16, 8))
    width = w_in_b.shape[1]
    row_spec = lambda w: pl.BlockSpec((1, ts, w), lambda b, i: (b, i, 0))
    mods_spec = (pl.BlockSpec((1, 1, N_MOD * d), lambda b, i: (b, 0, 0)) if r == 1
                 else pl.BlockSpec((1, ts, N_MOD * d), lambda b, i: (b, i, 0)))
    in_specs = [row_spec(d), mods_spec, pl.BlockSpec((1, d), lambda b, i: (0, 0)),
                pl.BlockSpec((d, width), lambda b, i: (0, 0))]
    out_specs = [row_spec(ATTN_WIDTH)] * 3
    out_shape = [jax.ShapeDtypeStruct((nb, s, ATTN_WIDTH), BF16), jax.ShapeDtypeStruct((nb, s, ATTN_WIDTH), F32),
                 jax.ShapeDtypeStruct((nb, s, ATTN_WIDTH), F32)]
    args, scratch = [x, mods, g, w_in_b], []
    if pool_params is None:
        out_specs.append(row_spec(POOL_WIDTH))
        out_shape.append(jax.ShapeDtypeStruct((nb, s, POOL_WIDTH), F32))
    else:
        assert ts % POOL_HALO == 0
        w_pool_b, pool_scale = pool_params
        in_specs += [pl.BlockSpec(w_pool_b.shape, lambda b, i: (0, 0, 0)),
                     pl.BlockSpec((1, POOL_WIDTH), lambda b, i: (0, 0))]
        args += [w_pool_b, pool_scale]
        out_specs += [row_spec(POOL_WIDTH), pl.BlockSpec((1, POOL_HALO, POOL_WIDTH), lambda b, i: (b, 0, 0))]
        out_shape += [jax.ShapeDtypeStruct((nb, s, POOL_WIDTH), BF16),
                      jax.ShapeDtypeStruct((nb, POOL_HALO, POOL_WIDTH), F32)]
        scratch = [pltpu.VMEM((POOL_HALO, POOL_WIDTH), F32)]
    return pl.pallas_call(
        functools.partial(_inproj_kernel, fused_pool=pool_params is not None),
        grid=(nb, s // ts),
        in_specs=in_specs,
        out_specs=out_specs,
        out_shape=out_shape,
        scratch_shapes=scratch,
        compiler_params=pltpu.CompilerParams(dimension_semantics=("arbitrary", "arbitrary"),
                                             vmem_limit_bytes=VMEM_LIMIT),
        name="in_proj",
    )(*args)


def _bias_kernel(tab_ref, *refs):
    n = len(refs) // 2
    h = pl.program_id(0)
    for idx_ref, out_ref in zip(refs[:n], refs[n:]):
        idx = idx_ref[...]
        bias = jnp.zeros(idx.shape, F32)
        for b in range(N_BUCKETS):
            bias = jnp.where(idx == b, tab_ref[h, b], bias)
        out_ref[0] = jnp.where(idx < 0, NEG, bias * LOG2E)


def _bias_tiles(tab_t):
    blk = MOBA_BLOCK
    r = np.arange(blk)[None, :]
    c = np.arange(blk)[:, None]
    rows = np.arange(SAMPLE_ROWS)[:, None]
    keys = np.arange(blk)[None, :]
    own_keys = np.arange(LANES)[None, :]
    idx = [np.where(r >= c, _t5_bucket_np(r - c), -1), _t5_bucket_np(blk + r - c),
           _t5_bucket_np(blk + rows - keys), np.where(rows >= own_keys, _t5_bucket_np(rows - own_keys), -1)]
    idx = [jnp.asarray(t.astype(np.int32)) for t in idx]
    return pl.pallas_call(
        _bias_kernel,
        grid=(N_HEADS,),
        in_specs=[pl.BlockSpec(memory_space=pltpu.SMEM)] + [pl.BlockSpec(t.shape, lambda h: (0, 0)) for t in idx],
        out_specs=[pl.BlockSpec((1,) + t.shape, lambda h: (h, 0, 0)) for t in idx],
        out_shape=[jax.ShapeDtypeStruct((N_HEADS,) + t.shape, F32) for t in idx],
        compiler_params=pltpu.CompilerParams(dimension_semantics=("arbitrary",)),
        name="rel_bias_tiles",
    )(tab_t, *idx)


def _attn_prompt_kernel(tab_ref, q_ref, k_ref, v_ref, bown_ref, badj_ref, o_ref,
                        kb_ref, vt_ref, km_ref, qm_ref, madd_ref, m_ref, acc_ref, s_ref, *, nb):
    i = pl.program_id(1)
    blk = MOBA_BLOCK
    nbp = km_ref.shape[0]

    @pl.when(i == 0)
    def _():
        kb_ref[...] = k_ref[0].astype(BF16)
        ones = jnp.ones((VT_ROWS - HEAD_DIM, blk), BF16)
        for j in range(nb):
            vt = v_ref[0, j * blk:(j + 1) * blk, :].T.astype(BF16)
            for h in range(N_HEADS):
                vt_ref[j, h, 0:HEAD_DIM, :] = vt[h * HEAD_DIM:(h + 1) * HEAD_DIM, :]
                vt_ref[j, h, HEAD_DIM:, :] = ones
        means = [jnp.mean(k_ref[0, j * blk:(j + 1) * blk, :], axis=0, keepdims=True) for j in range(nb)]
        means += [jnp.zeros_like(means[0])] * (nbp - nb)
        km_ref[...] = jnp.concatenate(means, axis=0)

    lane = lax.broadcasted_iota(jnp.int32, (blk, LANES), 1)
    blk_row = lax.broadcasted_iota(jnp.int32, (nbp, blk), 0)
    valid = blk_row < i
    jm1 = jnp.maximum(i - 1, 0)
    own0 = pl.multiple_of(i * blk, blk)
    adj0 = pl.multiple_of(jm1 * blk, blk)
    heads_per_vreg = LANES // HEAD_DIM

    def lanes_of(h):
        p = h // heads_per_vreg
        return slice(p * LANES, (p + 1) * LANES)

    for h in range(N_HEADS):
        hh = h % heads_per_vreg
        q2 = q_ref[0, :, lanes_of(h)]
        hmask = (lane >= hh * HEAD_DIM) & (lane < (hh + 1) * HEAD_DIM)
        qm = jnp.where(hmask, q2, jnp.zeros_like(q2))
        qm_ref[h] = qm
        gate = _nt_dot(km_ref[:, lanes_of(h)].astype(BF16), qm)
        for j in range(nb):
            gj = gate[j:j + 1, :]
            beats = ((gate > gj) | ((gate == gj) & (blk_row < j))) & valid
            cnt = jnp.sum(beats.astype(F32), axis=0, keepdims=True)
            cnt = cnt + jnp.where(j < i, 0.0, float(nb))
            madd_ref[h, j:j + 1, :] = jnp.where(cnt < MOBA_TOPK, 0.0, NEG)

    def key_blocks(blocks):
        for n, (key0, _, tile_bias, _, _) in enumerate(blocks):
            for h in range(N_HEADS):
                s = _nt_dot(kb_ref[pl.ds(key0, blk), lanes_of(h)], qm_ref[h])
                s_ref[n, h] = s if tile_bias is None else s + tile_bias(h)
        for n, (_, blk_idx, _, query_bias, first) in enumerate(blocks):
            for h in range(N_HEADS):
                s = s_ref[n, h]
                m_blk = jnp.max(s, axis=0, keepdims=True)
                if query_bias is not None:
                    qb = query_bias(h)
                    m_blk = m_blk + qb
                if first:
                    m_new = m_blk
                else:
                    m_old = m_ref[h]
                    m_new = jnp.maximum(m_old, m_blk)
                ref_row = m_new if query_bias is None else m_new - qb
                p = jnp.exp2(s - ref_row).astype(BF16)
                pv = _dot(vt_ref[blk_idx, h], p)
                acc_ref[h] = pv if first else jnp.exp2(m_old - m_new) * acc_ref[h] + pv
                m_ref[h] = m_new

    def far_block(j):
        return (pl.multiple_of(j * blk, blk), j, None,
                lambda h: madd_ref[h, pl.ds(j, 1), :] + tab_ref[h, N_BUCKETS - 1] * LOG2E, False)

    key_blocks([(own0, i, lambda h: bown_ref[h], None, True),
                (adj0, jm1, lambda h: badj_ref[h], lambda h: madd_ref[h, pl.ds(jm1, 1), :], False)])

    def far_pair(jj, carry):
        key_blocks([far_block(FAR_GROUP * jj + n) for n in range(FAR_GROUP)])
        return carry

    lax.fori_loop(0, jm1 // FAR_GROUP, far_pair, 0)
    for n in range(1, FAR_GROUP):
        @pl.when(jm1 % FAR_GROUP >= n)
        def _():
            key_blocks([far_block(jm1 - n)])

    for p in range(ATTN_WIDTH // LANES):
        halves = []
        for h in range(p * heads_per_vreg, (p + 1) * heads_per_vreg):
            acc = acc_ref[h]
            halves.append(acc[0:HEAD_DIM, :] / acc[HEAD_DIM:HEAD_DIM + 1, :])
        o_ref[0, :, p * LANES:(p + 1) * LANES] = jnp.concatenate(halves, axis=0).T.astype(BF16)


def _attn_prompt(tab_t, q, k, v, bias_own_t, bias_adj_t):
    b, s, a = q.shape
    blk = MOBA_BLOCK
    nb = s // blk
    nbp = 8
    assert s % blk == 0 and nb <= nbp
    full = pl.BlockSpec((1, s, a), lambda bi, i: (bi, 0, 0))
    tile = pl.BlockSpec((1, blk, a), lambda bi, i: (bi, i, 0))
    bias = pl.BlockSpec((N_HEADS, blk, blk), lambda bi, i: (0, 0, 0))
    return pl.pallas_call(
        functools.partial(_attn_prompt_kernel, nb=nb),
        grid=(b, nb),
        in_specs=[pl.BlockSpec(memory_space=pltpu.SMEM), tile, full, full, bias, bias],
        out_specs=tile,
        out_shape=jax.ShapeDtypeStruct((b, s, a), BF16),
        scratch_shapes=[pltpu.VMEM((s, a), BF16), pltpu.VMEM((nb, N_HEADS, VT_ROWS, blk), BF16),
                        pltpu.VMEM((nbp, a), F32), pltpu.VMEM((N_HEADS, blk, LANES), BF16),
                        pltpu.VMEM((N_HEADS, nbp, blk), F32), pltpu.VMEM((N_HEADS, 1, blk), F32),
                        pltpu.VMEM((N_HEADS, VT_ROWS, blk), F32),
                        pltpu.VMEM((FAR_GROUP, N_HEADS, blk, blk), F32)],
        compiler_params=pltpu.CompilerParams(dimension_semantics=("arbitrary", "arbitrary"),
                                             vmem_limit_bytes=VMEM_LIMIT),
        name="moba_prompt",
    )(tab_t, q, k, v, bias_own_t, bias_adj_t)


def _pool_rows(u, hist, w_ref, scale_ref, first_row_pos):
    tp = u.shape[0]
    ext = jnp.concatenate([hist, u], axis=0)
    top_pos = first_row_pos + lax.broadcasted_iota(jnp.int32, (POOL_HALO, 1), 0)
    outs = []
    for g, w in enumerate(POOL_WINDOWS):
        gs = slice(g * POOL_GROUP, (g + 1) * POOL_GROUP)
        s = ext[:, gs]
        shift = 1
        while shift < w:
            s = s + pltpu.roll(s, shift, 0)
            shift *= 2
        means = [s[POOL_HALO:2 * POOL_HALO, :] / jnp.minimum(top_pos + 1, w).astype(F32)]
        if tp > POOL_HALO:
            means.append(s[2 * POOL_HALO:, :] * (1.0 / w))
        d = jnp.concatenate(means, axis=0) - u[:, gs]
        outs.append((_dot(d.astype(BF16), w_ref[g]) * scale_ref[:, gs]).astype(BF16))
    return jnp.concatenate(outs, axis=1)


def _pool_kernel(u_ref, hist_ref, w_ref, scale_ref, o_ref, *, first_pos):
    i = pl.program_id(1)
    hist = jnp.where(i > 0, hist_ref[0], 0.0)
    o_ref[0] = _pool_rows(u_ref[0], hist, w_ref, scale_ref, first_pos + i * u_ref.shape[1])


def _halo_spec(tile_rows, width):
    per_tile = tile_rows // POOL_HALO
    return pl.BlockSpec((1, POOL_HALO, width), lambda b, i, *_: (b, jnp.maximum(i * per_tile - 1, 0), 0))


def _pool(u, w_pool_b, pool_scale, first_pos):
    nb, s, c = u.shape
    tp = _row_tile(s, (512, 256, 128, 64, 32, 16))
    return pl.pallas_call(
        functools.partial(_pool_kernel, first_pos=first_pos),
        grid=(nb, s // tp),
        in_specs=[pl.BlockSpec((1, tp, c), lambda b, i: (b, i, 0)),
                  _halo_spec(tp, c),
                  pl.BlockSpec(w_pool_b.shape, lambda b, i: (0, 0, 0)),
                  pl.BlockSpec((1, c), lambda b, i: (0, 0))],
        out_specs=pl.BlockSpec((1, tp, c), lambda b, i: (b, i, 0)),
        out_shape=jax.ShapeDtypeStruct((nb, s, c), BF16),
        compiler_params=pltpu.CompilerParams(dimension_semantics=("arbitrary", "arbitrary"),
                                             vmem_limit_bytes=VMEM_LIMIT),
        name="multi_pool",
    )(u, u, w_pool_b, pool_scale)


def _out_kernel(pt_ref, x_ref, attn_ref, pool_ref, mods_ref, gm_ref, gf_ref, wo_ref, wu_ref, wd_ref, *refs,
                rider):
    n_chunks = wu_ref.shape[1] // FF_CHUNK
    if rider is None:
        (y_ref,) = refs
        n_pages = 0
    else:
        ck_ref, y_ref, kmean_ref, pbuf, psem = refs
        n_pages, groups_per_seq, n_groups = rider
        step = pl.program_id(0) * pl.num_programs(1) + pl.program_id(1)
        last = pl.num_programs(0) * pl.num_programs(1) - 1
        slot = step % 2
        nxt = jnp.minimum(step + 1, last)

        def page_copies(st, sl, first, count):
            g = jnp.minimum(st, n_groups - 1)
            seq, part = g // groups_per_seq, g % groups_per_seq
            return [pltpu.make_async_copy(ck_ref.at[0, pt_ref[seq, part * n_pages + n]], pbuf.at[sl, n],
                                          psem.at[sl]) for n in range(first, first + count)]

        @pl.when(step == 0)
        def _():
            for c in page_copies(step, slot, 0, n_pages):
                c.start()

        for c in page_copies(nxt, 1 - slot, 0, n_pages):
            c.start()
        for c in page_copies(step, slot, 0, n_pages):
            c.wait()
    n_blocks = n_pages // PAGES_PER_BLOCK
    d = x_ref.shape[-1]
    a = ATTN_WIDTH
    x = x_ref[0]
    g1 = mods_ref[0, :, 2 * d:3 * d]
    sh2 = mods_ref[0, :, 3 * d:4 * d]
    sc2 = mods_ref[0, :, 4 * d:5 * d]
    g2 = mods_ref[0, :, 5 * d:6 * d]
    mix = _dot(attn_ref[0], wo_ref[0:a, :]) + _dot(pool_ref[0], wo_ref[a:, :])
    x1 = x + g1 * mix
    hb = _rms_modulated(x1, gm_ref[...], sc2, sh2).astype(BF16)
    acc = jnp.zeros(x.shape, F32)
    for c in range(n_chunks):
        cs = slice(c * FF_CHUNK, (c + 1) * FF_CHUNK)
        t = jnp.maximum(_dot(hb, wu_ref[:, cs]), 0.0)
        acc = acc + _dot((t * t).astype(BF16), wd_ref[cs, :])
        if rider is not None:
            for blk in range(c * n_blocks // n_chunks, (c + 1) * n_blocks // n_chunks):
                ksum = pbuf[slot, blk * PAGES_PER_BLOCK]
                for pg in range(1, PAGES_PER_BLOCK):
                    ksum = ksum + pbuf[slot, blk * PAGES_PER_BLOCK + pg]
                kmean_ref[0, blk] = jnp.sum(ksum, axis=-1) * (1.0 / MOBA_BLOCK)
    x2 = x1 + g2 * acc
    y_ref[0] = _rms(x2, gf_ref[...])
    if rider is not None:
        @pl.when(step == last)
        def _():
            for cp in page_copies(nxt, 1 - slot, 0, n_pages):
                cp.wait()


def _out(x, attn, pool, mods, g_mlp, g_final, w_out_b, w_up_b, w_down_b, page_table, cache_kt=None, tm=None):
    nb, s, d = x.shape
    r = mods.shape[1]
    tm = tm or _row_tile(s)
    n_tiles = s // tm
    row_spec = lambda w: pl.BlockSpec((1, tm, w), lambda b, i, pt: (b, i, 0))
    mods_spec = (pl.BlockSpec((1, 1, N_MOD * d), lambda b, i, pt: (b, 0, 0)) if r == 1
                 else pl.BlockSpec((1, tm, N_MOD * d), lambda b, i, pt: (b, i, 0)))
    resident = lambda w: pl.BlockSpec(w.shape, lambda b, i, pt: (0, 0), pipeline_mode=pl.Buffered(1))
    vec = pl.BlockSpec((1, d), lambda b, i, pt: (0, 0))
    out_specs, out_shape = [row_spec(d)], [jax.ShapeDtypeStruct((nb, s, d), F32)]
    cache_specs, cache_args, scratch, rider = [], [], [], None
    if cache_kt is not None:
        db, n_pages = page_table.shape
        nblk = n_pages // PAGES_PER_BLOCK
        need = -(-db * nblk // (nb * n_tiles))
        group = min(g for g in range(1, nblk + 1) if nblk % g == 0 and g >= need)
        groups_per_seq = nblk // group
        n_groups = db * groups_per_seq
        pps = group * PAGES_PER_BLOCK
        rider = (pps, groups_per_seq, n_groups)

        def group_of(b, i):
            g = jnp.minimum(b * n_tiles + i, n_groups - 1)
            return g // groups_per_seq, g % groups_per_seq

        cache_specs, cache_args = [pl.BlockSpec(memory_space=pl.ANY)], [cache_kt]
        scratch = [pltpu.VMEM((2, pps) + cache_kt.shape[2:], F32), pltpu.SemaphoreType.DMA((2,))]
        out_specs.append(pl.BlockSpec((1, group, N_HEADS, HEAD_DIM), lambda b, i, pt: group_of(b, i) + (0, 0)))
        out_shape.append(jax.ShapeDtypeStruct((db, nblk, N_HEADS, HEAD_DIM), F32))
    grid_spec = pltpu.PrefetchScalarGridSpec(
        num_scalar_prefetch=1,
        grid=(nb, n_tiles),
        in_specs=[row_spec(d), row_spec(ATTN_WIDTH), row_spec(POOL_WIDTH), mods_spec, vec, vec,
                  resident(w_out_b), resident(w_up_b), resident(w_down_b)] + cache_specs,
        out_specs=out_specs,
        scratch_shapes=scratch,
    )
    out = pl.pallas_call(
        functools.partial(_out_kernel, rider=rider),
        grid_spec=grid_spec,
        out_shape=out_shape,
        compiler_params=pltpu.CompilerParams(dimension_semantics=("arbitrary", "arbitrary"),
                                             vmem_limit_bytes=VMEM_LIMIT),
        name="out_mlp",
    )(page_table, x, attn, pool, mods, g_mlp, g_final, w_out_b, w_up_b, w_down_b, *cache_args)
    return out if cache_kt is not None else out[0]


def _select_kernel(q_ref, km_ref, sel_ref):
    rows = q_ref.shape[1]
    nblk = km_ref.shape[1]
    for sq in range(q_ref.shape[0]):
        for h in range(N_HEADS):
            q_h = q_ref[sq, :, h * HEAD_DIM:(h + 1) * HEAD_DIM].astype(F32)
            g = lax.dot_general(q_h, km_ref[sq, :, h, :], (((1,), (1,)), ((), ())),
                                precision=lax.Precision.HIGHEST, preferred_element_type=F32)
            lane = lax.broadcasted_iota(jnp.int32, g.shape, 1)
            out_lane = lax.broadcasted_iota(jnp.int32, (rows, LANES), 1)
            out = jnp.zeros((rows, LANES), jnp.int32)
            for n in range(MOBA_TOPK):
                mx = jnp.max(g, axis=-1, keepdims=True)
                idx = jnp.min(jnp.where(g == mx, lane, nblk), axis=-1, keepdims=True)
                out = jnp.where(out_lane == n, idx, out)
                g = jnp.where(lane == idx, -jnp.inf, g)
            sel_ref[sq, h * rows:(h + 1) * rows, :] = out


def _select(q_rows, kmean):
    db, rows, a = q_rows.shape
    nblk = kmean.shape[1]
    seqs = math.gcd(db, SELECT_SEQS_PER_STEP)
    return pl.pallas_call(
        _select_kernel,
        grid=(db // seqs,),
        in_specs=[pl.BlockSpec((seqs, rows, a), lambda b: (b, 0, 0)),
                  pl.BlockSpec((seqs,) + kmean.shape[1:], lambda b: (b, 0, 0, 0))],
        out_specs=pl.BlockSpec((seqs, N_HEADS * rows, LANES), lambda b: (b, 0, 0)),
        out_shape=jax.ShapeDtypeStruct((db, N_HEADS * rows, LANES), jnp.int32),
        compiler_params=pltpu.CompilerParams(dimension_semantics=("arbitrary",)),
        name="moba_select",
    )(q_rows, kmean)


def _attn_sample_kernel(pt_ref, sel_ref, tab_ref, q_ref, kn_ref, vn_ref, badj_ref, bown_ref, ck_ref, cv_ref,
                        o_ref, kbuf, vbuf, sem, *, nblk, t_new):
    b = pl.program_id(0)
    n_seq = pl.num_programs(0)
    n_sel = MOBA_TOPK
    per_head = t_new * n_sel
    buf = b % 2

    def block_of(sb, h, t, n):
        return sel_ref[((sb * N_HEADS + h) * t_new + t) * n_sel + n]

    def copies(sb, h, bf):
        out = []
        for t in range(t_new):
            for n in range(n_sel):
                j = block_of(sb, h, t, n)
                slot = (bf * N_HEADS + h) * per_head + t * n_sel + n
                for pg in range(PAGES_PER_BLOCK):
                    page = pt_ref[sb, j * PAGES_PER_BLOCK + pg]
                    keys = pl.ds(pg * PAGE_SIZE, PAGE_SIZE)
                    out.append(pltpu.make_async_copy(ck_ref.at[0, page, h], kbuf.at[slot, :, keys], sem.at[0, bf]))
                    out.append(pltpu.make_async_copy(cv_ref.at[0, page, h], vbuf.at[slot, :, keys], sem.at[1, bf]))
        return out

    @pl.when(b == 0)
    def _():
        for h in range(N_HEADS):
            for n, c in enumerate(copies(b, h, buf)):
                c.start(priority=n % 2)

    for h in range(N_HEADS):
        for c in copies(b, h, buf):
            c.wait()

    tok_row = lax.broadcasted_iota(jnp.int32, (SAMPLE_ROWS, MOBA_BLOCK), 0)
    ones_rows = jnp.ones((BF16_SUBLANES, MOBA_BLOCK), BF16)

    def head_attention(h):
        slot0 = (buf * N_HEADS + h) * per_head
        hl = slice(h * HEAD_DIM, (h + 1) * HEAD_DIM)
        c_far = tab_ref[h, N_BUCKETS - 1] * LOG2E
        q_h = q_ref[0, :, hl]
        scores = []
        for t in range(t_new):
            for n in range(n_sel):
                j = block_of(b, h, t, n)
                s = _dot(q_h, kbuf[slot0 + t * n_sel + n].astype(BF16))
                bias = jnp.where(j == nblk - 1, badj_ref[h], c_far)
                scores.append(jnp.where(tok_row == t, s + bias, NEG))
        s_own = _nt_dot(q_h, kn_ref[0, :, hl].astype(BF16)) + bown_ref[h, :, 0:SAMPLE_ROWS]
        m_tile = scores[0]
        for s in scores[1:]:
            m_tile = jnp.maximum(m_tile, s)
        m = jnp.maximum(jnp.max(m_tile, axis=-1, keepdims=True), jnp.max(s_own, axis=-1, keepdims=True))
        p_own = jnp.exp2(s_own - m).astype(BF16)
        v_own = jnp.concatenate([vn_ref[0, h].astype(BF16), ones_rows[:, :SAMPLE_ROWS]], axis=0)
        acc = _nt_dot(v_own, p_own)
        for i_blk, s in enumerate(scores):
            p = jnp.exp2(s - m).astype(BF16)
            v_blk = jnp.concatenate([vbuf[slot0 + i_blk].astype(BF16), ones_rows], axis=0)
            acc = acc + _nt_dot(v_blk, p)
        o_ref[0, h] = acc[0:HEAD_DIM, :] / acc[HEAD_DIM:HEAD_DIM + 1, :]

    nxt = jnp.minimum(b + 1, n_seq - 1)
    for h in range(N_HEADS):
        for n, c in enumerate(copies(nxt, h, 1 - buf)):
            c.start(priority=n % 2)
        head_attention(h)

    @pl.when(b == n_seq - 1)
    def _():
        for h in range(N_HEADS):
            for c in copies(nxt, h, 1 - buf):
                c.wait()


def _attn_sample(page_table, sel_flat, tab_t, q_rows, k_rows, v_cols, badj, bown, cache_kt, cache_vt, nblk, t_new):
    db, rows, a = q_rows.shape
    nh, dh = v_cols.shape[1:3]
    n_slots = 2 * nh * t_new * MOBA_TOPK
    grid_spec = pltpu.PrefetchScalarGridSpec(
        num_scalar_prefetch=2,
        grid=(db,),
        in_specs=[pl.BlockSpec(memory_space=pltpu.SMEM),
                  pl.BlockSpec((1, rows, a), lambda b, pt, sel: (b, 0, 0)),
                  pl.BlockSpec((1, rows, a), lambda b, pt, sel: (b, 0, 0)),
                  pl.BlockSpec((1, nh, dh, rows), lambda b, pt, sel: (b, 0, 0, 0)),
                  pl.BlockSpec(badj.shape, lambda b, pt, sel: (0, 0, 0)),
                  pl.BlockSpec(bown.shape, lambda b, pt, sel: (0, 0, 0)),
                  pl.BlockSpec(memory_space=pl.ANY), pl.BlockSpec(memory_space=pl.ANY)],
        out_specs=pl.BlockSpec((1, nh, dh, rows), lambda b, pt, sel: (b, 0, 0, 0)),
        scratch_shapes=[pltpu.VMEM((n_slots, dh, MOBA_BLOCK), F32),
                        pltpu.VMEM((n_slots, dh, MOBA_BLOCK), F32),
                        pltpu.SemaphoreType.DMA((2, 2))],
    )
    return pl.pallas_call(
        functools.partial(_attn_sample_kernel, nblk=nblk, t_new=t_new),
        grid_spec=grid_spec,
        out_shape=jax.ShapeDtypeStruct((db, nh, dh, rows), F32),
        compiler_params=pltpu.CompilerParams(dimension_semantics=("arbitrary",),
                                             vmem_limit_bytes=VMEM_LIMIT),
        name="moba_sample",
    )(page_table, sel_flat, tab_t, q_rows, k_rows, v_cols, badj, bown, cache_kt, cache_vt)


def kernel(x_prompt, x_sample, cache_k, cache_v, state_pool, page_table, c_prompt, c_sample, w_ada, b_ada, norm_mix, w_in, rel_bias, w_pool, pool_scale, w_out, norm_mlp, w_up, w_down, norm_final):
    assert w_ada.shape[0] == 1, "single-layer decoder"
    b, s, d = x_prompt.shape
    db, t_new, _ = x_sample.shape
    n_pages = page_table.shape[1]
    past = n_pages * PAGE_SIZE
    assert past % MOBA_BLOCK == 0 and past // MOBA_BLOCK >= MOBA_TOPK
    nblk = past // MOBA_BLOCK

    w_in_b = w_in[0].astype(BF16)
    w_out_b = w_out[0].astype(BF16)
    w_up_b = w_up[0].astype(BF16)
    w_down_b = w_down[0].astype(BF16)
    w_pool_b = w_pool[0].astype(BF16)
    g_final = norm_final.reshape(1, d)
    tab_t = rel_bias.T

    mods = _ada(jnp.concatenate([c_prompt, c_sample], axis=0), w_ada, b_ada[0:1])
    mods_p = mods[:b].reshape(b, 1, N_MOD * d)
    mods_s = jnp.repeat(mods[b:], t_new, axis=0).reshape(1, db * t_new, N_MOD * d)
    bias_own, bias_adj, badj_rows, bown_rows = _bias_tiles(tab_t)

    cache_kt = cache_k.transpose(0, 1, 3, 4, 2)
    cache_vt = cache_v.transpose(0, 1, 3, 4, 2)

    q_p, k_p, v_p, pool_p, u_tail = _inproj(x_prompt, mods_p, norm_mix, w_in_b, pool_params=(w_pool_b, pool_scale))
    attn_p = _attn_prompt(tab_t, q_p, k_p, v_p, bias_own, bias_adj)
    y_p, kmean = _out(x_prompt, attn_p, pool_p, mods_p, norm_mlp, g_final, w_out_b, w_up_b, w_down_b,
                      page_table, cache_kt=cache_kt, tm=_row_tile(s, (OUT_TILE_WITH_PAGES, 128, 64, 32, 16, 8)))

    xs = x_sample.reshape(1, db * t_new, d)
    q_s, k_s, v_s, u_s = _inproj(xs, mods_s, norm_mix, w_in_b)
    assert t_new <= SAMPLE_ROWS
    pad_rows = lambda z: jnp.pad(z.reshape(db, t_new, ATTN_WIDTH), ((0, 0), (0, SAMPLE_ROWS - t_new), (0, 0)))
    q_rows = pad_rows(q_s)
    sel = _select(q_rows, kmean).reshape(db, N_HEADS, SAMPLE_ROWS, LANES)[:, :, :t_new, :MOBA_TOPK]
    v_cols = pad_rows(v_s).reshape(db, SAMPLE_ROWS, N_HEADS, HEAD_DIM).transpose(0, 2, 3, 1)
    attn_s = _attn_sample(page_table, sel.reshape(-1), tab_t, q_rows, pad_rows(k_s), v_cols,
                          badj_rows, bown_rows, cache_kt, cache_vt, nblk, t_new)
    attn_s = attn_s[..., :t_new].transpose(0, 3, 1, 2).reshape(1, db * t_new, ATTN_WIDTH).astype(BF16)

    u_full = jnp.concatenate([state_pool[0], u_s.reshape(db, t_new, POOL_WIDTH)], axis=1)
    ext_rows = POOL_HALO + 8
    u_ext = jnp.pad(u_full, ((0, 0), (1, ext_rows - 1 - u_full.shape[1]), (0, 0)))
    pool_ext = _pool(u_ext.reshape(1, db * ext_rows, POOL_WIDTH), w_pool_b, pool_scale, first_pos=past)
    pool_s = pool_ext.reshape(db, ext_rows, POOL_WIDTH)[:, POOL_HALO:POOL_HALO + t_new]
    pool_s = pool_s.reshape(1, db * t_new, POOL_WIDTH)
    y_s = _out(xs, attn_s, pool_s, mods_s, norm_mlp, g_final, w_out_b, w_up_b, w_down_b, page_table)

    heads = lambda z, n, l: z.reshape(1, n, l, N_HEADS, HEAD_DIM)
    return (y_p, y_s.reshape(db, t_new, d),
            heads(k_p, b, s), heads(v_p, b, s), u_tail[:, -POOL_STATE:][None],
            heads(k_s, db, t_new), heads(v_s, db, t_new), u_full[:, -POOL_STATE:][None])
```
